```python
import math
import jax, jax.numpy as jnp
from jax import lax
import numpy as np

D_MODEL = 1024
BATCH = 2
SEQ = 8192
DEPTH = 2
DEC_BATCH = 32
DEC_SEQ = 8
PAST_LEN = 8192
PAGE_SIZE = 128

N_EVEN = (DEPTH + 1) // 2
N_ODD = DEPTH // 2
EPS = 1e-6
CONV_K = 4
SSD_HEADS = 16
SSD_HEAD_DIM = 64
SSD_INNER = SSD_HEADS * SSD_HEAD_DIM
SSD_GROUPS = 2
SSD_STATE = 64
SSD_CONV_DIM = SSD_INNER + 2 * SSD_GROUPS * SSD_STATE
SSD_CHUNK = 64
GDN_HEADS = 8
GDN_DK = 128
GDN_DV = 128
GDN_QK = GDN_HEADS * GDN_DK
GDN_VW = GDN_HEADS * GDN_DV
GDN_CONV_DIM = 2 * GDN_QK + GDN_VW
GDN_CHUNK = 64
IN_HYB = SSD_INNER + SSD_CONV_DIM + SSD_HEADS + GDN_CONV_DIM + GDN_VW + 2 * GDN_HEADS
MIX_HYB = SSD_INNER + GDN_VW
FOX_HEADS = 16
FOX_HEAD_DIM = 64
FOX_W = FOX_HEADS * FOX_HEAD_DIM
IN_FOX = 3 * FOX_W + FOX_HEADS
FOX_QBLOCK = 128
FOX_BIAS_LO = 3.0
FOX_BIAS_HI = 10.0
MEM_LEN = 256
X_HEADS = 4
X_HEAD_DIM = 128
X_W = X_HEADS * X_HEAD_DIM
D_FF = ((8 * D_MODEL // 3 + 255) // 256) * 256

kernel_name = 'hybrid_ssd_gdn_fox_decoder_step'


def _rms(x):
    return x * lax.rsqrt(jnp.mean(x * x, axis=-1, keepdims=True) + EPS)


def rmsnorm(x, g):
    return (_rms(x.astype(jnp.float32)) * g.astype(jnp.float32)).astype(x.dtype)


def _l2norm(x):
    return x * lax.rsqrt(jnp.sum(x * x, axis=-1, keepdims=True) + 1e-6)


def _split(x, sizes):
    return jnp.split(x, np.cumsum(sizes)[:-1].tolist(), axis=-1)


def _causal_conv(x, buf, w, b):
    seq = x.shape[1]
    xp = jnp.concatenate([buf.astype(x.dtype), x], axis=1)
    y = xp[:, 0:seq] * w[0]
    for j in range(1, CONV_K):
        y = y + xp[:, j:j + seq] * w[j]
    if b is not None:
        y = y + b
    return jax.nn.silu(y), xp[:, -(CONV_K - 1):]


def _ssd_scan(x, dt, a, bm, cm, h0):
    bsz, seq, nh, hp = x.shape
    ng, ns = bm.shape[2], bm.shape[3]
    hg = nh // ng
    q = math.gcd(seq, SSD_CHUNK)
    nc = seq // q

    def chunks(t):
        return t.reshape(bsz, nc, q, *t.shape[2:]).swapaxes(0, 1)

    xs = (chunks(x.reshape(bsz, seq, ng, hg, hp)), chunks(dt.reshape(bsz, seq, ng, hg)), chunks(bm), chunks(cm))
    ag = a.reshape(ng, hg)
    causal = jnp.tril(jnp.ones((q, q), bool))

    def body(h, inp):
        xc, dtc, bc, cc = inp
        cum = jnp.cumsum(dtc * ag, axis=1)
        seg = cum[:, :, None] - cum[:, None, :]
        lm = jnp.exp(jnp.where(causal[None, :, :, None, None], seg, -jnp.inf))
        cb = jnp.einsum('btgn,bsgn->btsg', cc, bc)
        xdt = xc * dtc[..., None]
        y = jnp.einsum('btsg,btsgh,bsghp->btghp', cb, lm, xdt)
        y = y + jnp.einsum('btgn,bghpn->btghp', cc, h) * jnp.exp(cum)[..., None]
        wlast = jnp.exp(cum[:, -1:] - cum)
        h = h * jnp.exp(cum[:, -1])[..., None, None] + jnp.einsum('bsgn,bsgh,bsghp->bghpn', bc, wlast, xdt)
        return h, y

    h, ys = lax.scan(body, h0.reshape(bsz, ng, hg, hp, ns), xs)
    return ys.swapaxes(0, 1).reshape(bsz, seq, nh, hp), h.reshape(bsz, nh, hp, ns)


def _gdn_scan(q, k, v, g, beta, s0):
    bsz, seq, nh, _ = q.shape
    dv = v.shape[-1]
    c = math.gcd(seq, GDN_CHUNK)
    nc = seq // c

    def chunks(t):
        return jnp.moveaxis(t.reshape(bsz, nc, c, nh, *t.shape[3:]), (1, 3), (0, 2))

    xs = (chunks(q), chunks(k), chunks(v), chunks(g), chunks(beta))
    tril = jnp.tril(jnp.ones((c, c), bool))
    strict = jnp.tril(jnp.ones((c, c), bool), -1)
    eye = jnp.eye(c, dtype=jnp.float32)

    def body(s, inp):
        qc, kc, vc, gc, bc = inp
        gam = jnp.cumsum(gc, axis=-1)
        dec = jnp.exp(jnp.where(tril, gam[..., :, None] - gam[..., None, :], -jnp.inf))
        kb = kc * bc[..., None]
        amat = jnp.where(strict, jnp.einsum('bhik,bhjk->bhij', kb, kc) * dec, 0.0)
        tinv = lax.linalg.triangular_solve(eye + amat, jnp.broadcast_to(eye, amat.shape),
                                           left_side=True, lower=True, unit_diagonal=True)
        u = jnp.einsum('bhij,bhjv->bhiv', tinv, vc * bc[..., None])
        w = jnp.einsum('bhij,bhjk->bhik', tinv, kb * jnp.exp(gam)[..., None])
        vn = u - jnp.einsum('bhik,bhkv->bhiv', w, s)
        qk = jnp.einsum('bhik,bhjk->bhij', qc, kc) * dec
        o = jnp.einsum('bhik,bhkv->bhiv', qc * jnp.exp(gam)[..., None], s) + jnp.einsum('bhij,bhjv->bhiv', qk, vn)
        glast = gam[..., -1]
        s = s * jnp.exp(glast)[..., None, None] + jnp.einsum('bhjk,bhjv->bhkv', kc * jnp.exp(glast[..., None] - gam)[..., None], vn)
        return s, o

    s, os_ = lax.scan(body, s0, xs)
    return jnp.moveaxis(os_, (0, 2), (1, 3)).reshape(bsz, seq, nh, dv), s


def _hybrid_mixer(hn, ssd_h0, ssd_buf, gdn_s0, gdn_buf, w_in, w_out, ssd_conv_w, ssd_conv_b,
                  ssd_dt_bias, ssd_a_log, ssd_d, ssd_norm, gdn_conv_w, gdn_dt_bias, gdn_a_log, gdn_norm):
    f32 = jnp.float32
    bsz, seq, _ = hn.shape
    z, xbc, dt, qkv, gate, b_raw, a_raw = _split(
        hn @ w_in, [SSD_INNER, SSD_CONV_DIM, SSD_HEADS, GDN_CONV_DIM, GDN_VW, GDN_HEADS, GDN_HEADS])
    xbc, ssd_buf_new = _causal_conv(xbc, ssd_buf, ssd_conv_w, ssd_conv_b)
    xs, bm, cm = _split(xbc.astype(f32), [SSD_INNER, SSD_GROUPS * SSD_STATE, SSD_GROUPS * SSD_STATE])
    xs = xs.reshape(bsz, seq, SSD_HEADS, SSD_HEAD_DIM)
    dt = jax.nn.softplus(dt.astype(f32) + ssd_dt_bias.astype(f32))
    a = -jnp.exp(ssd_a_log.astype(f32))
    y, ssd_h = _ssd_scan(xs, dt, a, bm.reshape(bsz, seq, SSD_GROUPS, SSD_STATE),
                         cm.reshape(bsz, seq, SSD_GROUPS, SSD_STATE), ssd_h0.astype(f32))
    y = (y + ssd_d.astype(f32)[:, None] * xs).reshape(bsz, seq, SSD_INNER) * jax.nn.silu(z.astype(f32))
    y = _rms(y.reshape(bsz, seq, SSD_GROUPS, SSD_INNER // SSD_GROUPS)).reshape(bsz, seq, SSD_INNER) * ssd_norm.astype(f32)
    qkv, gdn_buf_new = _causal_conv(qkv, gdn_buf, gdn_conv_w, None)
    q, k, v = _split(qkv.astype(f32), [GDN_QK, GDN_QK, GDN_VW])
    q = _l2norm(q.reshape(bsz, seq, GDN_HEADS, GDN_DK)) * GDN_DK ** -0.5
    k = _l2norm(k.reshape(bsz, seq, GDN_HEADS, GDN_DK))
    v = v.reshape(bsz, seq, GDN_HEADS, GDN_DV)
    beta = jax.nn.sigmoid(b_raw.astype(f32))
    g = -jnp.exp(gdn_a_log.astype(f32)) * jax.nn.softplus(a_raw.astype(f32) + gdn_dt_bias.astype(f32))
    o, gdn_s = _gdn_scan(q, k, v, g, beta, gdn_s0.astype(f32))
    o = _rms(o) * gdn_norm.astype(f32) * jax.nn.silu(gate.astype(f32).reshape(bsz, seq, GDN_HEADS, GDN_DV))
    mix = jnp.concatenate([y, o.reshape(bsz, seq, GDN_VW)], axis=-1).astype(hn.dtype)
    return (mix @ w_out, ssd_h.astype(hn.dtype), ssd_buf_new, gdn_s.astype(hn.dtype), gdn_buf_new)


def _fox_project(hn, w_in, b_f):
    bsz, seq, _ = hn.shape
    q, k, v, f = _split(hn @ w_in, [FOX_W, FOX_W, FOX_W, FOX_HEADS])
    shp = (bsz, seq, FOX_HEADS, FOX_HEAD_DIM)
    lf = jax.nn.log_sigmoid(f.astype(jnp.float32) + b_f.astype(jnp.float32))
    return q.reshape(shp), k.reshape(shp), v.reshape(shp), lf


def _fox_prompt_attn(q, k, v, lf):
    bsz, seq, nh, hd = q.shape
    scale = hd ** -0.5
    nb = seq // FOX_QBLOCK
    fcum = jnp.cumsum(lf, axis=1).transpose(0, 2, 1)
    qb = q.reshape(bsz, nb, FOX_QBLOCK, nh, hd).swapaxes(0, 1)
    fb = fcum.reshape(bsz, nh, nb, FOX_QBLOCK).transpose(2, 0, 1, 3)
    kpos = jnp.arange(seq)

    def block(args):
        qi, fi, i = args
        s = jnp.einsum('bqhd,bkhd->bhqk', qi, k).astype(jnp.float32) * scale + fi[..., None] - fcum[:, :, None, :]
        qpos = i * FOX_QBLOCK + jnp.arange(FOX_QBLOCK)
        s = jnp.where(kpos[None, :] <= qpos[:, None], s, -jnp.inf)
        p = jax.nn.softmax(s, axis=-1).astype(v.dtype)
        return jnp.einsum('bhqk,bkhd->bqhd', p, v)

    out = lax.map(block, (qb, fb, jnp.arange(nb)))
    return out.swapaxes(0, 1).reshape(bsz, seq, nh, hd)


def _fox_sample_attn(q, k, v, lf, kp, vp, lfp):
    hd = q.shape[-1]
    nt = q.shape[1]
    npast = kp.shape[1]
    scale = hd ** -0.5
    lfp = lfp.astype(jnp.float32)
    fn = jnp.cumsum(lf, axis=1).transpose(0, 2, 1)
    rev = (jnp.cumsum(lfp[:, ::-1], axis=1)[:, ::-1] - lfp).transpose(0, 2, 1)
    sp = jnp.einsum('bthd,bshd->bhts', q, kp).astype(jnp.float32) * scale + fn[..., None] + rev[:, :, None, :]
    sn = jnp.einsum('bthd,bshd->bhts', q, k).astype(jnp.float32) * scale + fn[..., None] - fn[:, :, None, :]
    sn = jnp.where(jnp.tril(jnp.ones((nt, nt), bool)), sn, -jnp.inf)
    p = jax.nn.softmax(jnp.concatenate([sp, sn], axis=-1), axis=-1)
    return (jnp.einsum('bhts,bshd->bthd', p[..., :npast].astype(vp.dtype), vp)
            + jnp.einsum('bhts,bshd->bthd', p[..., npast:].astype(v.dtype), v))


def _mem_kv(mem, g, wk, wv):
    bsz = mem.shape[0]
    m = rmsnorm(mem, g)
    shp = (bsz, MEM_LEN, X_HEADS, X_HEAD_DIM)
    return (m @ wk).reshape(shp), (m @ wv).reshape(shp)


def _cross_attn(hn, mk, mv, wq, wo):
    bsz, seq, _ = hn.shape
    q = (hn @ wq).reshape(bsz, seq, X_HEADS, X_HEAD_DIM)
    s = jnp.einsum('blhd,bmhd->bhlm', q, mk.astype(q.dtype)).astype(jnp.float32) * X_HEAD_DIM ** -0.5
    p = jax.nn.softmax(s, axis=-1).astype(q.dtype)
    o = jnp.einsum('bhlm,bmhd->blhd', p, mv.astype(q.dtype)).reshape(bsz, seq, X_W)
    return o @ wo


def _swiglu(hn, w1, w3, w2):
    return (jax.nn.silu(hn @ w1) * (hn @ w3)) @ w2


def _run_group(x, W, st, prompt):
    bsz, seq, _ = x.shape
    dty = x.dtype
    names = ('ssd', 'ssd_conv', 'gdn', 'gdn_conv', 'fox_k', 'fox_v', 'fox_lf', 'mem_k', 'mem_v')
    out = {n: [] for n in names}
    h = x
    for layer in range(DEPTH):
        hn = rmsnorm(h, W['norm_mix'][layer])
        if layer % 2 == 0:
            e = layer // 2
            if prompt:
                s0 = (jnp.zeros((bsz, SSD_HEADS, SSD_HEAD_DIM, SSD_STATE), jnp.float32),
                      jnp.zeros((bsz, CONV_K - 1, SSD_CONV_DIM), dty),
                      jnp.zeros((bsz, GDN_HEADS, GDN_DK, GDN_DV), jnp.float32),
                      jnp.zeros((bsz, CONV_K - 1, GDN_CONV_DIM), dty))
            else:
                s0 = (st['state_ssd'][e], st['state_ssd_conv'][e], st['state_gdn'][e], st['state_gdn_conv'][e])
            mix, s_ssd, c_ssd, s_gdn, c_gdn = _hybrid_mixer(
                hn, s0[0], s0[1], s0[2], s0[3], W['w_in_hyb'][e], W['w_out_hyb'][e], W['ssd_conv_w'][e],
                W['ssd_conv_b'][e], W['ssd_dt_bias'][e], W['ssd_A_log'][e], W['ssd_D'][e], W['ssd_norm'][e],
                W['gdn_conv_w'][e], W['gdn_dt_bias'][e], W['gdn_A_log'][e], W['gdn_norm'][e])
            out['ssd'].append(s_ssd)
            out['ssd_conv'].append(c_ssd)
            out['gdn'].append(s_gdn)
            out['gdn_conv'].append(c_gdn)
        else:
            o = layer // 2
            q, k, v, lf = _fox_project(hn, W['w_in_fox'][o], W['b_fox_f'][o])
            if prompt:
                att = _fox_prompt_attn(q, k, v, lf)
            else:
                pt = st['page_table']
                npast = pt.shape[1] * PAGE_SIZE
                kp = st['cache_fox_k'][o][pt].reshape(bsz, npast, FOX_HEADS, FOX_HEAD_DIM)
                vp = st['cache_fox_v'][o][pt].reshape(bsz, npast, FOX_HEADS, FOX_HEAD_DIM)
                lfp = st['cache_fox_lf'][o][pt].reshape(bsz, npast, FOX_HEADS)
                att = _fox_sample_attn(q, k, v, lf, kp, vp, lfp)
            mix = att.reshape(bsz, seq, FOX_W) @ W['w_out_fox'][o]
            out['fox_k'].append(k)
            out['fox_v'].append(v)
            out['fox_lf'].append(lf.astype(dty))
        h = h + mix
        if prompt:
            mk, mv = _mem_kv(st['mem'], W['norm_mem'][layer], W['wk_x'][layer], W['wv_x'][layer])
            out['mem_k'].append(mk)
            out['mem_v'].append(mv)
        else:
            mk, mv = st['cache_mem_k'][layer], st['cache_mem_v'][layer]
        h = h + _cross_attn(rmsnorm(h, W['norm_x'][layer]), mk, mv, W['wq_x'][layer], W['wo_x'][layer])
        h = h + _swiglu(rmsnorm(h, W['norm_ffn'][layer]), W['w1'][layer], W['w3'][layer], W['w2'][layer])
    y = rmsnorm(h, W['norm_final'])
    new = {n: jnp.stack(out[n]) for n in names if out[n]}
    return y, new


def setup_inputs(seed: int = 0) -> dict:
    key = jax.random.key(seed)
    ks = iter(jax.random.split(key, 64))
    f32 = jnp.float32

    def nrm(shape, scale=1.0):
        return jax.random.normal(next(ks), shape, f32) * scale

    def dense(n, fi, fo):
        return nrm((n, fi, fo), fi ** -0.5)

    def gain(shape):
        return 1.0 + nrm(shape, 0.01)

    def dt_bias(n, h):
        dt = jnp.exp(jax.random.uniform(next(ks), (n, h), f32, math.log(1e-3), math.log(1e-1)))
        return dt + jnp.log(-jnp.expm1(-dt))

    def a_log(n, h):
        return jnp.log(jax.random.uniform(next(ks), (n, h), f32, 1.0, 16.0))

    n_pages = PAST_LEN // PAGE_SIZE
    n_used = DEC_BATCH * n_pages
    n_pool = n_used + n_used // 4
    page_table = jax.random.permutation(next(ks), n_pool)[:n_used].reshape(DEC_BATCH, n_pages).astype(jnp.int32)
    fox_head_bias = jnp.linspace(FOX_BIAS_LO, FOX_BIAS_HI, FOX_HEADS, dtype=f32)
    return {
        'x_prompt': nrm((BATCH, SEQ, D_MODEL)),
        'x_sample': nrm((DEC_BATCH, DEC_SEQ, D_MODEL)),
        'mem_prompt': nrm((BATCH, MEM_LEN, D_MODEL)),
        'state_ssd': nrm((N_EVEN, DEC_BATCH, SSD_HEADS, SSD_HEAD_DIM, SSD_STATE), 0.1),
        'state_ssd_conv': nrm((N_EVEN, DEC_BATCH, CONV_K - 1, SSD_CONV_DIM)),
        'state_gdn': nrm((N_EVEN, DEC_BATCH, GDN_HEADS, GDN_DK, GDN_DV), 0.1),
        'state_gdn_conv': nrm((N_EVEN, DEC_BATCH, CONV_K - 1, GDN_CONV_DIM)),
        'cache_fox_k': nrm((N_ODD, n_pool, PAGE_SIZE, FOX_HEADS, FOX_HEAD_DIM)),
        'cache_fox_v': nrm((N_ODD, n_pool, PAGE_SIZE, FOX_HEADS, FOX_HEAD_DIM)),
        'cache_fox_lf': jax.nn.log_sigmoid(fox_head_bias + nrm((N_ODD, n_pool, PAGE_SIZE, FOX_HEADS), 0.5)),
        'page_table': page_table,
        'cache_mem_k': nrm((DEPTH, DEC_BATCH, MEM_LEN, X_HEADS, X_HEAD_DIM)),
        'cache_mem_v': nrm((DEPTH, DEC_BATCH, MEM_LEN, X_HEADS, X_HEAD_DIM)),
        'norm_mix': gain((DEPTH, D_MODEL)),
        'norm_x': gain((DEPTH, D_MODEL)),
        'norm_mem': gain((DEPTH, D_MODEL)),
        'norm_ffn': gain((DEPTH, D_MODEL)),
        'norm_final': gain((D_MODEL,)),
        'w_in_hyb': dense(N_EVEN, D_MODEL, IN_HYB),
        'w_out_hyb': dense(N_EVEN, MIX_HYB, D_MODEL),
        'ssd_conv_w': nrm((N_EVEN, CONV_K, SSD_CONV_DIM), CONV_K ** -0.5),
        'ssd_conv_b': nrm((N_EVEN, SSD_CONV_DIM), 0.02),
        'ssd_dt_bias': dt_bias(N_EVEN, SSD_HEADS),
        'ssd_A_log': a_log(N_EVEN, SSD_HEADS),
        'ssd_D': 1.0 + nrm((N_EVEN, SSD_HEADS), 0.1),
        'ssd_norm': gain((N_EVEN, SSD_INNER)),
        'gdn_conv_w': nrm((N_EVEN, CONV_K, GDN_CONV_DIM), CONV_K ** -0.5),
        'gdn_dt_bias': dt_bias(N_EVEN, GDN_HEADS),
        'gdn_A_log': a_log(N_EVEN, GDN_HEADS),
        'gdn_norm': gain((N_EVEN, GDN_DV)),
        'w_in_fox': dense(N_ODD, D_MODEL, IN_FOX),
        'b_fox_f': fox_head_bias + nrm((N_ODD, FOX_HEADS), 0.1),
        'w_out_fox': dense(N_ODD, FOX_W, D_MODEL),
        'wq_x': dense(DEPTH, D_MODEL, X_W),
        'wk_x': dense(DEPTH, D_MODEL, X_W),
        'wv_x': dense(DEPTH, D_MODEL, X_W),
        'wo_x': dense(DEPTH, X_W, D_MODEL),
        'w1': dense(DEPTH, D_MODEL, D_FF),
        'w3': dense(DEPTH, D_MODEL, D_FF),
        'w2': dense(DEPTH, D_FF, D_MODEL),
    }


def reference(x_prompt, x_sample, mem_prompt, state_ssd, state_ssd_conv, state_gdn, state_gdn_conv,
              cache_fox_k, cache_fox_v, cache_fox_lf, page_table, cache_mem_k, cache_mem_v,
              norm_mix, norm_x, norm_mem, norm_ffn, norm_final, w_in_hyb, w_out_hyb, ssd_conv_w, ssd_conv_b,
              ssd_dt_bias, ssd_A_log, ssd_D, ssd_norm, gdn_conv_w, gdn_dt_bias, gdn_A_log, gdn_norm,
              w_in_fox, b_fox_f, w_out_fox, wq_x, wk_x, wv_x, wo_x, w1, w3, w2):
    W = dict(norm_mix=norm_mix, norm_x=norm_x, norm_mem=norm_mem, norm_ffn=norm_ffn, norm_final=norm_final,
             w_in_hyb=w_in_hyb, w_out_hyb=w_out_hyb, ssd_conv_w=ssd_conv_w, ssd_conv_b=ssd_conv_b,
             ssd_dt_bias=ssd_dt_bias, ssd_A_log=ssd_A_log, ssd_D=ssd_D, ssd_norm=ssd_norm,
             gdn_conv_w=gdn_conv_w, gdn_dt_bias=gdn_dt_bias, gdn_A_log=gdn_A_log, gdn_norm=gdn_norm,
             w_in_fox=w_in_fox, b_fox_f=b_fox_f, w_out_fox=w_out_fox,
             wq_x=wq_x, wk_x=wk_x, wv_x=wv_x, wo_x=wo_x, w1=w1, w3=w3, w2=w2)
    y_prompt, pn = _run_group(x_prompt, W, dict(mem=mem_prompt), True)
    st = dict(state_ssd=state_ssd, state_ssd_conv=state_ssd_conv, state_gdn=state_gdn,
              state_gdn_conv=state_gdn_conv, cache_fox_k=cache_fox_k, cache_fox_v=cache_fox_v,
              cache_fox_lf=cache_fox_lf, page_table=page_table, cache_mem_k=cache_mem_k, cache_mem_v=cache_mem_v)
    y_sample, sn = _run_group(x_sample, W, st, False)
    return (y_prompt, y_sample,
            pn['ssd'], pn['ssd_conv'], pn['gdn'], pn['gdn_conv'],
            pn['fox_k'], pn['fox_v'], pn['fox_lf'], pn['mem_k'], pn['mem_v'],
            sn['ssd'], sn['ssd_conv'], sn['gdn'], sn['gdn_conv'],
            sn['fox_k'], sn['fox_v'], sn['fox_lf'])
```

```python
import functools
import math

import jax
import jax.numpy as jnp
from jax import lax
from jax.experimental import pallas as pl
from jax.experimental.pallas import tpu as pltpu

F32 = jnp.float32
BF16 = jnp.bfloat16
HI = lax.Precision.HIGHEST
NT_DIMS = (((1,), (1,)), ((), ()))
TN_DIMS = (((0,), (0,)), ((), ()))

D_MODEL = 1024
EPS = 1e-6
CONV_K = 4
SSD_HEADS = 16
SSD_HEAD_DIM = 64
SSD_INNER = SSD_HEADS * SSD_HEAD_DIM
SSD_GROUPS = 2
SSD_STATE = 64
SSD_CONV_DIM = SSD_INNER + 2 * SSD_GROUPS * SSD_STATE
GDN_HEADS = 8
GDN_DK = 128
GDN_DV = 128
GDN_QK = GDN_HEADS * GDN_DK
GDN_VW = GDN_HEADS * GDN_DV
GDN_CONV_DIM = 2 * GDN_QK + GDN_VW
FOX_HEADS = 16
FOX_HEAD_DIM = 64
FOX_W = FOX_HEADS * FOX_HEAD_DIM
PAGE_SIZE = 128
MEM_LEN = 256
X_HEADS = 4
X_HEAD_DIM = 128
X_W = X_HEADS * X_HEAD_DIM
LANES = 128
VMEM_LIMIT = 56 * 1024 * 1024


def _cparams(*sem):
    return pltpu.CompilerParams(dimension_semantics=sem, vmem_limit_bytes=VMEM_LIMIT)


def _pick(n, cands):
    for c in cands:
        if n % c == 0:
            return c
    return n


def _softplus(x):
    return jnp.maximum(x, 0.0) + jnp.log1p(jnp.exp(-jnp.abs(x)))


def _log_sigmoid(x):
    return jnp.minimum(x, 0.0) - jnp.log1p(jnp.exp(-jnp.abs(x)))


def _sigmoid(x):
    return 1.0 / (1.0 + jnp.exp(-x))


def _silu(x):
    return x * _sigmoid(x)


def _rms_rows(x):
    return x * lax.rsqrt(jnp.mean(x * x, axis=-1, keepdims=True) + EPS)


def _dot(a, b):
    return jnp.dot(a, b, preferred_element_type=F32)


def _dot_hi(a, b):
    return jnp.dot(a, b, preferred_element_type=F32, precision=HI)


def _dot_nt(a, b):
    return lax.dot_general(a, b, NT_DIMS, preferred_element_type=F32)


def _dot_tn(a, b):
    return lax.dot_general(a, b, TN_DIMS, preferred_element_type=F32)


def _iota(shape, dim):
    return lax.broadcasted_iota(jnp.int32, shape, dim)


def _linear_body(*refs, n_x, use_norm, act, has_res):
    it = iter(refs)
    x_refs = [next(it) for _ in range(n_x)]
    g_ref = next(it) if use_norm else None
    w_refs = [next(it) for _ in range(n_x)]
    b_ref = next(it) if act else None
    r_ref = next(it) if has_res else None
    o_ref = next(it)
    xs_refs = [next(it) for _ in range(n_x)]

    @pl.when(pl.program_id(1) == 0)
    def _():
        for i in range(n_x):
            x = x_refs[i][...]
            if use_norm and i == 0:
                x = _rms_rows(x) * g_ref[...]
            xs_refs[i][...] = x.astype(BF16)

    acc = _dot(xs_refs[0][...], w_refs[0][...])
    for i in range(1, n_x):
        acc = acc + _dot(xs_refs[i][...], w_refs[i][...])
    if act == "log_sigmoid":
        acc = _log_sigmoid(acc + b_ref[...])
    if has_res:
        acc = acc + r_ref[...]
    o_ref[...] = acc


def fused_linear(xs, ws, gain=None, bias=None, act=None, residual=None, name="linear"):
    m = xs[0].shape[0]
    n = ws[0].shape[1]
    tm = _pick(m, (1024, 512, 256, 128, 64, 32, 16, 8))
    tn = _pick(n, (512, 640, 384, 256, 128))
    n_x = len(xs)
    in_specs, args = [], []
    for x in xs:
        in_specs.append(pl.BlockSpec((tm, x.shape[1]), lambda i, j: (i, 0)))
        args.append(x)
    if gain is not None:
        in_specs.append(pl.BlockSpec((1, xs[0].shape[1]), lambda i, j: (0, 0)))
        args.append(gain.reshape(1, -1).astype(F32))
    for w in ws:
        in_specs.append(pl.BlockSpec((w.shape[0], tn), lambda i, j: (0, j)))
        args.append(w)
    if act:
        in_specs.append(pl.BlockSpec((1, tn), lambda i, j: (0, j)))
        args.append(bias.reshape(1, -1).astype(F32))
    if residual is not None:
        in_specs.append(pl.BlockSpec((tm, tn), lambda i, j: (i, j)))
        args.append(residual)
    body = functools.partial(_linear_body, n_x=n_x, use_norm=gain is not None, act=act,
                             has_res=residual is not None)
    return pl.pallas_call(
        body,
        grid=(m // tm, n // tn),
        in_specs=in_specs,
        out_specs=pl.BlockSpec((tm, tn), lambda i, j: (i, j)),
        out_shape=jax.ShapeDtypeStruct((m, n), F32),
        scratch_shapes=[pltpu.VMEM((tm, x.shape[1]), BF16) for x in xs],
        compiler_params=_cparams("parallel", "arbitrary"),
        name=name,
    )(*args)


def _rmsnorm_body(x_ref, g_ref, o_ref):
    o_ref[...] = _rms_rows(x_ref[...]) * g_ref[...]


def rmsnorm_rows(x, gain):
    m, d = x.shape
    tm = _pick(m, (1024, 512, 256, 128, 64, 32, 16, 8))
    return pl.pallas_call(
        _rmsnorm_body,
        grid=(m // tm,),
        in_specs=[pl.BlockSpec((tm, d), lambda i: (i, 0)), pl.BlockSpec((1, d), lambda i: (0, 0))],
        out_specs=pl.BlockSpec((tm, d), lambda i: (i, 0)),
        out_shape=jax.ShapeDtypeStruct((m, d), F32),
        compiler_params=_cparams("parallel"),
        name="final_norm",
    )(x, gain.reshape(1, d))


SUB = 8


def _conv_body(x_ref, prev_ref, buf_ref, w_ref, b_ref, o_ref, ext_ref, *, tt):
    first = pl.program_id(1) == 0
    ext_ref[0:SUB, :] = jnp.where(first, buf_ref[0], prev_ref[0])
    ext_ref[SUB:SUB + tt, :] = x_ref[0]
    w = w_ref[...]
    off = SUB - (CONV_K - 1)
    y = ext_ref[off:off + tt, :] * w[0:1, :]
    for j in range(1, CONV_K):
        y = y + ext_ref[off + j:off + j + tt, :] * w[j:j + 1, :]
    y = y + b_ref[...]
    o_ref[0] = _silu(y)


def conv_silu(x, buf, w, b):
    bsz, seq, ch = x.shape
    tt = _pick(seq, (512, 256, 128, 64, 32, 16, 8))
    tc = _pick(ch, (1024, 640, 512, 256, 128))
    bufp = jnp.concatenate([jnp.zeros((bsz, SUB - (CONV_K - 1), ch), F32), buf], axis=1)
    wp = jnp.concatenate([w, jnp.zeros((SUB - CONV_K, ch), F32)], axis=0)
    nsub = tt // SUB
    return pl.pallas_call(
        functools.partial(_conv_body, tt=tt),
        grid=(bsz, seq // tt, ch // tc),
        in_specs=[
            pl.BlockSpec((1, tt, tc), lambda bi, ti, ci: (bi, ti, ci)),
            pl.BlockSpec((1, SUB, tc), lambda bi, ti, ci: (bi, jnp.maximum(ti * nsub - 1, 0), ci)),
            pl.BlockSpec((1, SUB, tc), lambda bi, ti, ci: (bi, 0, ci)),
            pl.BlockSpec((SUB, tc), lambda bi, ti, ci: (0, ci)),
            pl.BlockSpec((1, tc), lambda bi, ti, ci: (0, ci)),
        ],
        out_specs=pl.BlockSpec((1, tt, tc), lambda bi, ti, ci: (bi, ti, ci)),
        out_shape=jax.ShapeDtypeStruct((bsz, seq, ch), F32),
        scratch_shapes=[pltpu.VMEM((SUB + tt, tc), F32)],
        compiler_params=_cparams("parallel", "parallel", "parallel"),
        name="conv_silu",
    )(x, x, bufp, wp, b.reshape(1, ch))


HG = SSD_HEADS // SSD_GROUPS
GW = HG * SSD_HEAD_DIM


def _expand_heads(v, e_bf16):
    hi = v.astype(BF16)
    r1 = v - hi.astype(F32)
    mid = r1.astype(BF16)
    lo = (r1 - mid.astype(F32)).astype(BF16)
    return _dot(hi, e_bf16) + _dot(mid, e_bf16) + _dot(lo, e_bf16)


def _ssd_body(xbc_ref, z_ref, small_ref, dtt_ref, h0_ref, prow_ref, pcol_ref, dexp_ref, norm_ref,
              y_ref, hout_ref, h_sc, y_sc, *, q):
    @pl.when(pl.program_id(1) == 0)
    def _():
        h_sc[...] = h0_ref[0]

    xbc = xbc_ref[0]
    xs = xbc[:, :SSD_INNER]
    bm = xbc[:, SSD_INNER:SSD_INNER + LANES]
    cm = xbc[:, SSD_INNER + LANES:SSD_INNER + 2 * LANES]
    dt_c = _softplus(small_ref[0][:, 0:SSD_HEADS] + prow_ref[0:1, 0:SSD_HEADS])
    a_c = dt_c * (-jnp.exp(prow_ref[1:2, 0:SSD_HEADS]))
    dt_t = _softplus(dtt_ref[0] + pcol_ref[:, 0:1])
    a_t = dt_t * (-jnp.exp(pcol_ref[:, 1:2]))
    ri = _iota((q, q), 0)
    ci = _iota((q, q), 1)
    causal = ci <= ri
    cum = _dot_hi(causal.astype(F32), a_c)
    cum_t = _dot_hi(a_t, (ri <= ci).astype(F32))
    e_heads = (_iota((SSD_HEADS, SSD_INNER), 1) // SSD_HEAD_DIM == _iota((SSD_HEADS, SSD_INNER), 0)).astype(BF16)
    dt_x = _expand_heads(dt_c, e_heads)
    ecum_x = _expand_heads(jnp.exp(cum), e_heads)
    wlast_x = _expand_heads(jnp.exp(cum[q - 1:q, :] - cum), e_heads)
    xdt = xs * dt_x
    xw = (xdt * wlast_x).astype(BF16)
    xdt_b = xdt.astype(BF16)

    lane = _iota((1, LANES), 1)
    low = lane < SSD_STATE
    bm_b = bm.astype(BF16)
    hs = h_sc[...]
    hs_b = hs.astype(BF16)
    y_inter = []
    cbs = []
    upd = []
    for g in range(SSD_GROUPS):
        cm_g = jnp.where(low if g == 0 else jnp.logical_not(low), cm, 0.0).astype(BF16)
        cbs.append(_dot_nt(cm_g, bm_b))
        y_inter.append(_dot(cm_g, hs_b))
        upd.append(_dot_tn(bm_b, xw[:, g * GW:(g + 1) * GW]))
    for j in range(SSD_HEADS // 2):
        g = (2 * j) // HG
        xp = xdt_b[:, j * LANES:(j + 1) * LANES]
        ys = []
        for hh in (2 * j, 2 * j + 1):
            seg = cum[:, hh:hh + 1] - cum_t[hh:hh + 1, :]
            lm = jnp.exp(jnp.where(causal, seg, -jnp.inf))
            ys.append(_dot((cbs[g] * lm).astype(BF16), xp))
        y_sc[:, j * LANES:(j + 1) * LANES] = jnp.where(low, ys[0], ys[1])
    y = y_sc[...] + jnp.concatenate(y_inter, axis=1) * ecum_x + dexp_ref[...] * xs
    y = y * _silu(z_ref[0])
    y = jnp.concatenate([_rms_rows(y[:, g * GW:(g + 1) * GW]) for g in range(SSD_GROUPS)], axis=1)
    y_ref[0] = y * norm_ref[...]

    row_low = _iota((2 * SSD_STATE, 1), 0) < SSD_STATE
    ecl = ecum_x[q - 1:q, :]
    decay = jnp.where(row_low, ecl[:, 0:GW], ecl[:, GW:2 * GW])
    h_new = hs * decay + jnp.where(row_low, upd[0], upd[1])
    h_sc[...] = h_new

    @pl.when(pl.program_id(1) == pl.num_programs(1) - 1)
    def _():
        hout_ref[0] = h_new


def ssd_scan(xbc_c, z, small, h0, dt_bias, a_log, d_skip, norm_w):
    bsz, seq, _ = xbc_c.shape
    q = _pick(seq, (128, 64, 32, 16, 8))
    dtt = jnp.swapaxes(small[:, :, 0:SSD_HEADS], 1, 2)
    hs0 = h0.reshape(bsz, SSD_GROUPS, HG, SSD_HEAD_DIM, SSD_STATE).transpose(0, 1, 4, 2, 3)
    hs0 = hs0.reshape(bsz, SSD_GROUPS * SSD_STATE, GW)
    prow = jnp.zeros((SUB, LANES), F32).at[0, :SSD_HEADS].set(dt_bias).at[1, :SSD_HEADS].set(a_log)
    pcol = jnp.zeros((SSD_HEADS, LANES), F32).at[:, 0].set(dt_bias).at[:, 1].set(a_log)
    dexp = jnp.repeat(d_skip, SSD_HEAD_DIM).reshape(1, SSD_INNER)
    y, hs = pl.pallas_call(
        functools.partial(_ssd_body, q=q),
        grid=(bsz, seq // q),
        in_specs=[
            pl.BlockSpec((1, q, SSD_CONV_DIM), lambda b, c: (b, c, 0)),
            pl.BlockSpec((1, q, SSD_INNER), lambda b, c: (b, c, 0)),
            pl.BlockSpec((1, q, LANES), lambda b, c: (b, c, 0)),
            pl.BlockSpec((1, SSD_HEADS, q), lambda b, c: (b, 0, c)),
            pl.BlockSpec((1, 2 * SSD_STATE, GW), lambda b, c: (b, 0, 0)),
            pl.BlockSpec((SUB, LANES), lambda b, c: (0, 0)),
            pl.BlockSpec((SSD_HEADS, LANES), lambda b, c: (0, 0)),
            pl.BlockSpec((1, SSD_INNER), lambda b, c: (0, 0)),
            pl.BlockSpec((1, SSD_INNER), lambda b, c: (0, 0)),
        ],
        out_specs=[
            pl.BlockSpec((1, q, SSD_INNER), lambda b, c: (b, c, 0)),
            pl.BlockSpec((1, 2 * SSD_STATE, GW), lambda b, c: (b, 0, 0)),
        ],
        out_shape=[
            jax.ShapeDtypeStruct((bsz, seq, SSD_INNER), F32),
            jax.ShapeDtypeStruct((bsz, 2 * SSD_STATE, GW), F32),
        ],
        scratch_shapes=[pltpu.VMEM((2 * SSD_STATE, GW), F32), pltpu.VMEM((q, SSD_INNER), F32)],
        compiler_params=_cparams("parallel", "arbitrary"),
        name="ssd_scan",
    )(xbc_c, z, small, dtt, hs0, prow, pcol, dexp, norm_w.reshape(1, SSD_INNER))
    h_new = hs.reshape(bsz, SSD_GROUPS, SSD_STATE, HG, SSD_HEAD_DIM).transpose(0, 1, 3, 4, 2)
    return y, h_new.reshape(bsz, SSD_HEADS, SSD_HEAD_DIM, SSD_STATE)


def _gdn_body(q_ref, k_ref, v_ref, gate_ref, small_ref, gbt_ref, prow_ref, s0_ref, norm_ref,
              o_ref, sout_ref, s_sc, *, c, nch):
    h = pl.program_id(1)

    @pl.when(pl.program_id(2) == 0)
    def _():
        s_sc[...] = s0_ref[0, 0]

    lane = _iota((1, LANES), 1)

    def pick_lane(x, idx):
        return jnp.sum(jnp.where(lane == idx, x, 0.0), axis=1, keepdims=True)

    neg_a = -jnp.exp(pick_lane(prow_ref[0:1, :], h))
    dtb = pick_lane(prow_ref[1:2, :], h)
    small = small_ref[0]
    beta_col = _sigmoid(pick_lane(small, SSD_HEADS + h))
    g_col = neg_a * _softplus(pick_lane(small, SSD_HEADS + GDN_HEADS + h) + dtb)
    g_row = neg_a * _softplus(gbt_ref[0, pl.ds(GDN_HEADS + h, 1), :] + dtb)

    ri = _iota((c, c), 0)
    ci = _iota((c, c), 1)
    tril = ri >= ci
    tril_f = tril.astype(F32)
    triu_f = (ri <= ci).astype(F32)
    eye = (ri == ci).astype(F32)
    for ch in range(nch):
        rows = slice(ch * c, (ch + 1) * c)
        qc = q_ref[0, rows, :]
        kc = k_ref[0, rows, :]
        vc = v_ref[0, rows, :]
        qn = qc * lax.rsqrt(jnp.sum(qc * qc, axis=-1, keepdims=True) + 1e-6) * (GDN_DK ** -0.5)
        kn = kc * lax.rsqrt(jnp.sum(kc * kc, axis=-1, keepdims=True) + 1e-6)
        bc = beta_col[rows, :]
        gam = _dot_hi(tril_f, jnp.broadcast_to(g_col[rows, :], (c, LANES)))
        gam_r = _dot_hi(jnp.broadcast_to(g_row[:, ch * c:(ch + 1) * c], (SUB, c)), triu_f)[0:1, :]
        dec = jnp.exp(jnp.where(tril, gam[:, 0:c] - gam_r, -jnp.inf))
        kb = kn * bc
        kn_b = kn.astype(BF16)
        nmat = jnp.where(ri > ci, _dot_nt(kb.astype(BF16), kn_b) * dec, 0.0) * -1.0
        tinv = eye + nmat
        npow = nmat
        for _ in range(int(math.log2(c)) - 1):
            npow = _dot_hi(npow, npow)
            tinv = tinv + _dot_hi(npow, tinv)
        tinv_b = tinv.astype(BF16)
        egam = jnp.exp(gam)
        u = _dot(tinv_b, (vc * bc).astype(BF16))
        w = _dot(tinv_b, (kb * egam).astype(BF16))
        s = s_sc[...]
        s_b = s.astype(BF16)
        vn = u - _dot(w.astype(BF16), s_b)
        vn_b = vn.astype(BF16)
        qk = _dot_nt(qn.astype(BF16), kn_b) * dec
        o = _dot((qn * egam).astype(BF16), s_b) + _dot(qk.astype(BF16), vn_b)
        glast = gam[c - 1:c, :]
        kd = kn * jnp.exp(glast - gam)
        s_sc[...] = s * jnp.exp(glast) + _dot_tn(kd.astype(BF16), vn_b)
        o_ref[0, rows, :] = _rms_rows(o) * norm_ref[...] * _silu(gate_ref[0, rows, :])

    @pl.when(pl.program_id(2) == pl.num_programs(2) - 1)
    def _():
        sout_ref[0, 0] = s_sc[...]


def gdn_scan(qkv_c, gate, small, s0, dt_bias, a_log, norm_w):
    bsz, seq, _ = qkv_c.shape
    c = _pick(seq, (64, 32, 16, 8))
    tb = _pick(seq, (256, 128, 64, 32, 16, 8))
    nch = tb // c
    gbt = jnp.swapaxes(small[:, :, SSD_HEADS:SSD_HEADS + 2 * GDN_HEADS], 1, 2)
    prow = jnp.zeros((SUB, LANES), F32).at[0, :GDN_HEADS].set(a_log).at[1, :GDN_HEADS].set(dt_bias)
    o, s_new = pl.pallas_call(
        functools.partial(_gdn_body, c=c, nch=nch),
        grid=(bsz, GDN_HEADS, seq // tb),
        in_specs=[
            pl.BlockSpec((1, tb, GDN_DK), lambda b, h, t: (b, t, h)),
            pl.BlockSpec((1, tb, GDN_DK), lambda b, h, t: (b, t, GDN_HEADS + h)),
            pl.BlockSpec((1, tb, GDN_DV), lambda b, h, t: (b, t, 2 * GDN_HEADS + h)),
            pl.BlockSpec((1, tb, GDN_DV), lambda b, h, t: (b, t, h)),
            pl.BlockSpec((1, tb, LANES), lambda b, h, t: (b, t, 0)),
            pl.BlockSpec((1, 2 * GDN_HEADS, tb), lambda b, h, t: (b, 0, t)),
            pl.BlockSpec((SUB, LANES), lambda b, h, t: (0, 0)),
            pl.BlockSpec((1, 1, GDN_DK, GDN_DV), lambda b, h, t: (b, h, 0, 0)),
            pl.BlockSpec((1, GDN_DV), lambda b, h, t: (0, 0)),
        ],
        out_specs=[
            pl.BlockSpec((1, tb, GDN_DV), lambda b, h, t: (b, t, h)),
            pl.BlockSpec((1, 1, GDN_DK, GDN_DV), lambda b, h, t: (b, h, 0, 0)),
        ],
        out_shape=[
            jax.ShapeDtypeStruct((bsz, seq, GDN_VW), F32),
            jax.ShapeDtypeStruct((bsz, GDN_HEADS, GDN_DK, GDN_DV), F32),
        ],
        scratch_shapes=[pltpu.VMEM((GDN_DK, GDN_DV), F32)],
        compiler_params=_cparams("parallel", "parallel", "arbitrary"),
        name="gdn_scan",
    )(qkv_c, qkv_c, qkv_c, gate, small, gbt, prow, s0, norm_w.reshape(1, GDN_DV))
    return o, s_new


def _xattn_body(h_ref, g_ref, wq_ref, mk_ref, mv_ref, wo_ref, o_ref):
    x = h_ref[0]
    xn = (_rms_rows(x) * g_ref[...]).astype(BF16)
    qf = _dot(xn, wq_ref[...])
    mk = mk_ref[0].astype(BF16)
    mv = mv_ref[0].astype(BF16)
    outs = []
    for hd in range(X_HEADS):
        sl = slice(hd * X_HEAD_DIM, (hd + 1) * X_HEAD_DIM)
        s = _dot_nt(qf[:, sl].astype(BF16), mk[:, sl]) * (X_HEAD_DIM ** -0.5)
        p = jnp.exp(s - jnp.max(s, axis=-1, keepdims=True))
        p = p / jnp.sum(p, axis=-1, keepdims=True)
        outs.append(_dot(p.astype(BF16), mv[:, sl]))
    o = jnp.concatenate(outs, axis=1).astype(BF16)
    o_ref[0] = x + _dot(o, wo_ref[...])


def cross_attn(h, gain, wq, mk, mv, wo):
    bsz, seq, d = h.shape
    tm = _pick(seq, (512, 256, 128, 64, 32, 16, 8))
    return pl.pallas_call(
        _xattn_body,
        grid=(bsz, seq // tm),
        in_specs=[
            pl.BlockSpec((1, tm, d), lambda b, i: (b, i, 0)),
            pl.BlockSpec((1, d), lambda b, i: (0, 0)),
            pl.BlockSpec((d, X_W), lambda b, i: (0, 0)),
            pl.BlockSpec((1, MEM_LEN, X_W), lambda b, i: (b, 0, 0)),
            pl.BlockSpec((1, MEM_LEN, X_W), lambda b, i: (b, 0, 0)),
            pl.BlockSpec((X_W, d), lambda b, i: (0, 0)),
        ],
        out_specs=pl.BlockSpec((1, tm, d), lambda b, i: (b, i, 0)),
        out_shape=jax.ShapeDtypeStruct((bsz, seq, d), F32),
        compiler_params=_cparams("parallel", "parallel"),
        name="cross_attn",
    )(h, gain.reshape(1, d), wq, mk, mv, wo)


def _swiglu_body(h_ref, g_ref, w1_ref, w3_ref, w2_ref, o_ref, xn_sc):
    j = pl.program_id(1)

    @pl.when(j == 0)
    def _():
        x = h_ref[...]
        xn_sc[...] = (_rms_rows(x) * g_ref[...]).astype(BF16)
        o_ref[...] = x

    xn = xn_sc[...]
    a = _dot(xn, w1_ref[...])
    b = _dot(xn, w3_ref[...])
    o_ref[...] += _dot((_silu(a) * b).astype(BF16), w2_ref[...])


def swiglu(h, gain, w1, w3, w2):
    m, d = h.shape
    ff = w1.shape[1]
    tm = _pick(m, (512, 256, 128, 64, 32, 16, 8))
    tf = _pick(ff, (1408, 1024, 512, 256, 128))
    return pl.pallas_call(
        _swiglu_body,
        grid=(m // tm, ff // tf),
        in_specs=[
            pl.BlockSpec((tm, d), lambda i, j: (i, 0)),
            pl.BlockSpec((1, d), lambda i, j: (0, 0)),
            pl.BlockSpec((d, tf), lambda i, j: (0, j)),
            pl.BlockSpec((d, tf), lambda i, j: (0, j)),
            pl.BlockSpec((tf, d), lambda i, j: (j, 0)),
        ],
        out_specs=pl.BlockSpec((tm, d), lambda i, j: (i, 0)),
        out_shape=jax.ShapeDtypeStruct((m, d), F32),
        scratch_shapes=[pltpu.VMEM((tm, d), BF16)],
        compiler_params=_cparams("parallel", "arbitrary"),
        name="swiglu",
    )(h, gain.reshape(1, d), w1, w3, w2)


def _cumsum_lanes_body(x_ref, o_ref, carry_sc, *, tc):
    @pl.when(pl.program_id(1) == 0)
    def _():
        carry_sc[...] = jnp.zeros_like(carry_sc)

    upper = (_iota((tc, tc), 0) <= _iota((tc, tc), 1)).astype(F32)
    f = _dot_hi(x_ref[0], upper) + carry_sc[:, 0:1]
    o_ref[0] = f
    carry_sc[...] = jnp.broadcast_to(f[:, tc - 1:tc], carry_sc.shape)


def cumsum_lanes(x):
    bsz, r, seq = x.shape
    tc = _pick(seq, (512, 256, 128))
    return pl.pallas_call(
        functools.partial(_cumsum_lanes_body, tc=tc),
        grid=(bsz, seq // tc),
        in_specs=[pl.BlockSpec((1, r, tc), lambda b, i: (b, 0, i))],
        out_specs=pl.BlockSpec((1, r, tc), lambda b, i: (b, 0, i)),
        out_shape=jax.ShapeDtypeStruct((bsz, r, seq), F32),
        scratch_shapes=[pltpu.VMEM((r, LANES), F32)],
        compiler_params=_cparams("parallel", "arbitrary"),
        name="cumsum_lanes",
    )(x)


def _fox_body(qi_ref, ki_ref, q_ref, k_ref, v_ref, f_ref, o_ref, m_sc, l_sc, acc_sc, *, t):
    step = pl.program_id(2)
    qi = qi_ref[step]
    ki = ki_ref[step]
    lane = _iota((1, LANES), 1)
    low = lane < FOX_HEAD_DIM

    @pl.when(ki == 0)
    def _():
        m_sc[...] = jnp.full(m_sc.shape, -jnp.inf, F32)
        l_sc[...] = jnp.zeros_like(l_sc)
        acc_sc[...] = jnp.zeros_like(acc_sc)

    def update(masked):
        qs = q_ref[0] * (FOX_HEAD_DIM ** -0.5)
        kb = k_ref[0].astype(BF16)
        vb = v_ref[0].astype(BF16)
        acc = acc_sc[...]
        new = []
        for hh in range(2):
            qm = jnp.where(low if hh == 0 else jnp.logical_not(low), qs, 0.0).astype(BF16)
            s = _dot_nt(qm, kb) - f_ref[0, 0, hh:hh + 1, :]
            if masked:
                s = jnp.where(_iota((t, t), 1) <= _iota((t, t), 0), s, -jnp.inf)
            m_prev = m_sc[hh]
            m_new = jnp.maximum(m_prev, jnp.max(s, axis=-1, keepdims=True))
            alpha = jnp.exp(m_prev - m_new)
            p = jnp.exp(s - m_new)
            l_sc[hh] = alpha * l_sc[hh] + jnp.sum(p, axis=-1, keepdims=True)
            m_sc[hh] = m_new
            new.append(alpha * acc + _dot(p.astype(BF16), vb))
        acc_sc[...] = jnp.where(low, new[0], new[1])

    @pl.when(ki < qi)
    def _():
        update(False)

    @pl.when(ki == qi)
    def _():
        update(True)
        o_ref[0] = acc_sc[...] / jnp.where(low, l_sc[0], l_sc[1])


def fox_prompt_attn(q, k, v, ft):
    bsz, seq, _ = q.shape
    t = _pick(seq, (512, 256, 128))
    nb = seq // t
    pairs = [(i, j) for i in range(nb) for j in range(i + 1)]
    qi = jnp.asarray([p[0] for p in pairs], jnp.int32)
    ki = jnp.asarray([p[1] for p in pairs], jnp.int32)
    npair = FOX_HEADS // 2
    ft4 = ft.reshape(bsz, npair, 2, seq)
    grid_spec = pltpu.PrefetchScalarGridSpec(
        num_scalar_prefetch=2,
        grid=(bsz, npair, len(pairs)),
        in_specs=[
            pl.BlockSpec((1, t, LANES), lambda b, j, s, qi, ki: (b, qi[s], j)),
            pl.BlockSpec((1, t, LANES), lambda b, j, s, qi, ki: (b, ki[s], j)),
            pl.BlockSpec((1, t, LANES), lambda b, j, s, qi, ki: (b, ki[s], j)),
            pl.BlockSpec((1, 1, 2, t), lambda b, j, s, qi, ki: (b, j, 0, ki[s])),
        ],
        out_specs=pl.BlockSpec((1, t, LANES), lambda b, j, s, qi, ki: (b, qi[s], j)),
        scratch_shapes=[pltpu.VMEM((2, t, 1), F32), pltpu.VMEM((2, t, 1), F32), pltpu.VMEM((t, LANES), F32)],
    )
    return pl.pallas_call(
        functools.partial(_fox_body, t=t),
        grid_spec=grid_spec,
        out_shape=jax.ShapeDtypeStruct((bsz, seq, FOX_W), F32),
        compiler_params=_cparams("parallel", "parallel", "arbitrary"),
        name="fox_prompt_attn",
    )(qi, ki, q, k, v, ft4)


def _page_suffix_body(x_ref, rin_ref, tot_ref):
    x = x_ref[...]
    after = (_iota((PAGE_SIZE, PAGE_SIZE), 0) > _iota((PAGE_SIZE, PAGE_SIZE), 1)).astype(F32)
    rin_ref[...] = _dot_hi(x, after)
    tot_ref[...] = jnp.broadcast_to(jnp.sum(x, axis=1, keepdims=True), x.shape)


def page_suffix(lft):
    r = lft.shape[0]
    tr = _pick(r, (2048, 1024, 512, 256, 128, 64, 32, 16))
    spec = pl.BlockSpec((tr, PAGE_SIZE), lambda i: (i, 0))
    return pl.pallas_call(
        _page_suffix_body,
        grid=(r // tr,),
        in_specs=[spec],
        out_specs=[spec, spec],
        out_shape=[jax.ShapeDtypeStruct(lft.shape, F32)] * 2,
        compiler_params=_cparams("parallel"),
        name="page_suffix",
    )(lft)


def _rep_rows(x, n):
    r, c = x.shape
    return jnp.broadcast_to(x[:, None, :], (r, n, c)).reshape(r * n, c)


def _fox_decode_body(pt_ref, q_ref, kc_ref, vc_ref, rin_ref, tot_ref, kn_ref, vn_ref, lfn_ref, o_ref,
                     qbd_sc, m_sc, l_sc, acc_sc, run_sc, *, nt, npages):
    j = pl.program_id(1)
    rows = FOX_HEADS * nt

    @pl.when(j == 0)
    def _():
        qt = jnp.concatenate([q_ref[0] * (FOX_HEAD_DIM ** -0.5)] * FOX_HEADS, axis=0)
        own = _iota((rows, FOX_W), 1) // FOX_HEAD_DIM == _iota((rows, FOX_W), 0) // nt
        qbd_sc[...] = jnp.where(own, qt, 0.0).astype(BF16)
        m_sc[...] = jnp.full(m_sc.shape, -jnp.inf, F32)
        l_sc[...] = jnp.zeros_like(l_sc)
        acc_sc[...] = jnp.zeros_like(acc_sc)
        run_sc[...] = jnp.zeros_like(run_sc)

    def update(s, vb):
        m_prev = m_sc[...]
        m_new = jnp.maximum(m_prev, jnp.max(s, axis=-1, keepdims=True))
        alpha = jnp.exp(m_prev - m_new)
        p = jnp.exp(s - m_new)
        l_sc[...] = alpha * l_sc[...] + jnp.sum(p, axis=-1, keepdims=True)
        m_sc[...] = m_new
        acc_sc[...] = alpha * acc_sc[...] + _dot(p.astype(BF16), vb)

    @pl.when(j < npages)
    def _():
        s = _dot_nt(qbd_sc[...], kc_ref[0].astype(BF16))
        s = s + _rep_rows(rin_ref[0] + run_sc[...], nt)
        update(s, vc_ref[0].astype(BF16))
        run_sc[...] = run_sc[...] + tot_ref[0]

    @pl.when(j == npages)
    def _():
        pad = jnp.zeros((PAGE_SIZE - nt, FOX_W), F32)
        kb = jnp.concatenate([kn_ref[0], pad], axis=0).astype(BF16)
        vb = jnp.concatenate([vn_ref[0], pad], axis=0).astype(BF16)
        s = _dot_nt(qbd_sc[...], kb)
        incl = (_iota((PAGE_SIZE, PAGE_SIZE), 0) <= _iota((PAGE_SIZE, PAGE_SIZE), 1)).astype(F32)
        fn = _dot_hi(lfn_ref[0], incl)
        tok = _iota((rows, PAGE_SIZE), 0) % nt
        s = jnp.where(_iota((rows, PAGE_SIZE), 1) <= tok, s - _rep_rows(fn, nt), -jnp.inf)
        update(s, vb)
        o = acc_sc[...] / l_sc[...]
        lane_head = _iota((nt, FOX_W), 1) // FOX_HEAD_DIM
        out = jnp.zeros((nt, FOX_W), F32)
        for hd in range(FOX_HEADS):
            out = out + jnp.where(lane_head == hd, o[hd * nt:(hd + 1) * nt, :], 0.0)
        o_ref[0] = out


def fox_decode_attn(q, k_new, v_new, lfn_t, k_cache, v_cache, rin, tot, page_table):
    bsz, nt, _ = q.shape
    npages = page_table.shape[1]
    rows = FOX_HEADS * nt

    def page(b, j, pt):
        return pt[b, npages - 1 - jnp.minimum(j, npages - 1)]

    grid_spec = pltpu.PrefetchScalarGridSpec(
        num_scalar_prefetch=1,
        grid=(bsz, npages + 1),
        in_specs=[
            pl.BlockSpec((1, nt, FOX_W), lambda b, j, pt: (b, 0, 0)),
            pl.BlockSpec((1, PAGE_SIZE, FOX_W), lambda b, j, pt: (page(b, j, pt), 0, 0)),
            pl.BlockSpec((1, PAGE_SIZE, FOX_W), lambda b, j, pt: (page(b, j, pt), 0, 0)),
            pl.BlockSpec((1, FOX_HEADS, PAGE_SIZE), lambda b, j, pt: (page(b, j, pt), 0, 0)),
            pl.BlockSpec((1, FOX_HEADS, PAGE_SIZE), lambda b, j, pt: (page(b, j, pt), 0, 0)),
            pl.BlockSpec((1, nt, FOX_W), lambda b, j, pt: (b, 0, 0)),
            pl.BlockSpec((1, nt, FOX_W), lambda b, j, pt: (b, 0, 0)),
            pl.BlockSpec((1, FOX_HEADS, PAGE_SIZE), lambda b, j, pt: (b, 0, 0)),
        ],
        out_specs=pl.BlockSpec((1, nt, FOX_W), lambda b, j, pt: (b, 0, 0)),
        scratch_shapes=[
            pltpu.VMEM((rows, FOX_W), BF16),
            pltpu.VMEM((rows, 1), F32),
            pltpu.VMEM((rows, 1), F32),
            pltpu.VMEM((rows, FOX_W), F32),
            pltpu.VMEM((FOX_HEADS, PAGE_SIZE), F32),
        ],
    )
    return pl.pallas_call(
        functools.partial(_fox_decode_body, nt=nt, npages=npages),
        grid_spec=grid_spec,
        out_shape=jax.ShapeDtypeStruct((bsz, nt, FOX_W), F32),
        compiler_params=_cparams("parallel", "arbitrary"),
        name="fox_decode_attn",
    )(page_table, q, k_cache, v_cache, rin, tot, k_new, v_new, lfn_t)


def _split_hyb_weights(w_in):
    o = 0
    parts = {}
    for name, width in (("z", SSD_INNER), ("xbc", SSD_CONV_DIM), ("dt", SSD_HEADS), ("qkv", GDN_CONV_DIM),
                        ("gate", GDN_VW), ("b", GDN_HEADS), ("a", GDN_HEADS)):
        parts[name] = w_in[:, o:o + width]
        o += width
    small = jnp.concatenate([parts["dt"], parts["b"], parts["a"]], axis=1)
    small = jnp.pad(small, ((0, 0), (0, LANES - small.shape[1])))
    return {k: parts[k].astype(BF16) for k in ("z", "xbc", "qkv", "gate")} | {"small": small.astype(BF16)}


def _hybrid_layer(h, e, W, st):
    bsz, seq, d = h.shape
    m = bsz * seq
    h2 = h.reshape(m, d)
    wp = _split_hyb_weights(W["w_in_hyb"][e])
    gain = W["norm_mix"][2 * e]
    proj = {k: fused_linear([h2], [wp[k]], gain=gain, name="hyb_in_" + k) for k in ("z", "xbc", "qkv", "gate", "small")}
    xbc = proj["xbc"].reshape(bsz, seq, SSD_CONV_DIM)
    qkv = proj["qkv"].reshape(bsz, seq, GDN_CONV_DIM)
    small = proj["small"].reshape(bsz, seq, LANES)
    xbc_c = conv_silu(xbc, st["ssd_conv"], W["ssd_conv_w"][e], W["ssd_conv_b"][e])
    qkv_c = conv_silu(qkv, st["gdn_conv"], W["gdn_conv_w"][e], jnp.zeros((GDN_CONV_DIM,), F32))
    y, ssd_h = ssd_scan(xbc_c, proj["z"].reshape(bsz, seq, SSD_INNER), small, st["ssd"],
                        W["ssd_dt_bias"][e], W["ssd_A_log"][e], W["ssd_D"][e], W["ssd_norm"][e])
    o, gdn_s = gdn_scan(qkv_c, proj["gate"].reshape(bsz, seq, GDN_VW), small, st["gdn"],
                        W["gdn_dt_bias"][e], W["gdn_A_log"][e], W["gdn_norm"][e])
    w_out = W["w_out_hyb"][e].astype(BF16)
    h_new = fused_linear([y.reshape(m, SSD_INNER), o.reshape(m, GDN_VW)], [w_out[:SSD_INNER], w_out[SSD_INNER:]],
                         residual=h2, name="hyb_out")
    new = dict(ssd=ssd_h, ssd_conv=xbc[:, seq - (CONV_K - 1):], gdn=gdn_s, gdn_conv=qkv[:, seq - (CONV_K - 1):])
    return h_new.reshape(bsz, seq, d), new


def _fox_layer(h, o_idx, layer, W, st, prompt):
    bsz, seq, d = h.shape
    m = bsz * seq
    h2 = h.reshape(m, d)
    w_in = W["w_in_fox"][o_idx]
    gain = W["norm_mix"][layer]
    wq, wk, wv = (w_in[:, i * FOX_W:(i + 1) * FOX_W].astype(BF16) for i in range(3))
    wf = jnp.pad(w_in[:, 3 * FOX_W:], ((0, 0), (0, LANES - FOX_HEADS))).astype(BF16)
    bf = jnp.pad(W["b_fox_f"][o_idx], (0, LANES - FOX_HEADS))
    q = fused_linear([h2], [wq], gain=gain, name="fox_q").reshape(bsz, seq, FOX_W)
    k = fused_linear([h2], [wk], gain=gain, name="fox_k").reshape(bsz, seq, FOX_W)
    v = fused_linear([h2], [wv], gain=gain, name="fox_v").reshape(bsz, seq, FOX_W)
    lf = fused_linear([h2], [wf], gain=gain, bias=bf, act="log_sigmoid", name="fox_f")[:, :FOX_HEADS]
    lf = lf.reshape(bsz, seq, FOX_HEADS)
    lf_t = jnp.swapaxes(lf, 1, 2)
    if prompt:
        att = fox_prompt_attn(q, k, v, cumsum_lanes(lf_t))
    else:
        n_pool = st["cache_fox_k"].shape[1]
        kc = st["cache_fox_k"][o_idx].reshape(n_pool, PAGE_SIZE, FOX_W)
        vc = st["cache_fox_v"][o_idx].reshape(n_pool, PAGE_SIZE, FOX_W)
        lfc_t = jnp.swapaxes(st["cache_fox_lf"][o_idx].astype(F32), 1, 2).reshape(n_pool * FOX_HEADS, PAGE_SIZE)
        rin, tot = page_suffix(lfc_t)
        rin = rin.reshape(n_pool, FOX_HEADS, PAGE_SIZE)
        tot = tot.reshape(n_pool, FOX_HEADS, PAGE_SIZE)
        lfn_t = jnp.pad(lf_t, ((0, 0), (0, 0), (0, PAGE_SIZE - seq)))
        att = fox_decode_attn(q, k, v, lfn_t, kc, vc, rin, tot, st["page_table"])
    h_new = fused_linear([att.reshape(m, FOX_W)], [W["w_out_fox"][o_idx].astype(BF16)], residual=h2, name="fox_out")
    shp = (bsz, seq, FOX_HEADS, FOX_HEAD_DIM)
    return h_new.reshape(bsz, seq, d), dict(fox_k=k.reshape(shp), fox_v=v.reshape(shp), fox_lf=lf)


def _run_group(x, W, st, prompt):
    bsz, seq, d = x.shape
    depth = W["norm_mix"].shape[0]
    names = ("ssd", "ssd_conv", "gdn", "gdn_conv", "fox_k", "fox_v", "fox_lf", "mem_k", "mem_v")
    out = {n: [] for n in names}
    h = x
    for layer in range(depth):
        if layer % 2 == 0:
            e = layer // 2
            if prompt:
                s0 = dict(ssd=jnp.zeros((bsz, SSD_HEADS, SSD_HEAD_DIM, SSD_STATE), F32),
                          ssd_conv=jnp.zeros((bsz, CONV_K - 1, SSD_CONV_DIM), F32),
                          gdn=jnp.zeros((bsz, GDN_HEADS, GDN_DK, GDN_DV), F32),
                          gdn_conv=jnp.zeros((bsz, CONV_K - 1, GDN_CONV_DIM), F32))
            else:
                s0 = dict(ssd=st["state_ssd"][e], ssd_conv=st["state_ssd_conv"][e],
                          gdn=st["state_gdn"][e], gdn_conv=st["state_gdn_conv"][e])
            h, new = _hybrid_layer(h, e, W, s0)
        else:
            h, new = _fox_layer(h, layer // 2, layer, W, st, prompt)
        for n, val in new.items():
            out[n].append(val)
        if prompt:
            mem = st["mem"]
            mem2 = mem.reshape(bsz * MEM_LEN, d)
            mk = fused_linear([mem2], [W["wk_x"][layer].astype(BF16)], gain=W["norm_mem"][layer], name="mem_k")
            mv = fused_linear([mem2], [W["wv_x"][layer].astype(BF16)], gain=W["norm_mem"][layer], name="mem_v")
            mk = mk.reshape(bsz, MEM_LEN, X_W)
            mv = mv.reshape(bsz, MEM_LEN, X_W)
            out["mem_k"].append(mk.reshape(bsz, MEM_LEN, X_HEADS, X_HEAD_DIM))
            out["mem_v"].append(mv.reshape(bsz, MEM_LEN, X_HEADS, X_HEAD_DIM))
        else:
            mk = st["cache_mem_k"][layer].reshape(bsz, MEM_LEN, X_W)
            mv = st["cache_mem_v"][layer].reshape(bsz, MEM_LEN, X_W)
        h = cross_attn(h, W["norm_x"][layer], W["wq_x"][layer].astype(BF16), mk, mv, W["wo_x"][layer].astype(BF16))
        h = swiglu(h.reshape(bsz * seq, d), W["norm_ffn"][layer], W["w1"][layer].astype(BF16),
                   W["w3"][layer].astype(BF16), W["w2"][layer].astype(BF16)).reshape(bsz, seq, d)
    y = rmsnorm_rows(h.reshape(bsz * seq, d), W["norm_final"]).reshape(bsz, seq, d)
    new = {n: jnp.stack(out[n]) for n in names if out[n]}
    return y, new


def kernel(x_prompt, x_sample, mem_prompt, state_ssd, state_ssd_conv, state_gdn, state_gdn_conv, cache_fox_k, cache_fox_v, cache_fox_lf, page_table, cache_mem_k, cache_mem_v, norm_mix, norm_x, norm_mem, norm_ffn, norm_final, w_in_hyb, w_out_hyb, ssd_conv_w, ssd_conv_b, ssd_dt_bias, ssd_A_log, ssd_D, ssd_norm, gdn_conv_w, gdn_dt_bias, gdn_A_log, gdn_norm, w_in_fox, b_fox_f, w_out_fox, wq_x, wk_x, wv_x, wo_x, w1, w3, w2):
    W = dict(norm_mix=norm_mix, norm_x=norm_x, norm_mem=norm_mem, norm_ffn=norm_ffn, norm_final=norm_final,
             w_in_hyb=w_in_hyb, w_out_hyb=w_out_hyb, ssd_conv_w=ssd_conv_w, ssd_conv_b=ssd_conv_b,
             ssd_dt_bias=ssd_dt_bias, ssd_A_log=ssd_A_log, ssd_D=ssd_D, ssd_norm=ssd_norm,
             gdn_conv_w=gdn_conv_w, gdn_dt_bias=gdn_dt_bias, gdn_A_log=gdn_A_log, gdn_norm=gdn_norm,
             w_in_fox=w_in_fox, b_fox_f=b_fox_f, w_out_fox=w_out_fox,
             wq_x=wq_x, wk_x=wk_x, wv_x=wv_x, wo_x=wo_x, w1=w1, w3=w3, w2=w2)
    y_prompt, pn = _run_group(x_prompt, W, dict(mem=mem_prompt), True)
    st = dict(state_ssd=state_ssd, state_ssd_conv=state_ssd_conv, state_gdn=state_gdn,
              state_gdn_conv=state_gdn_conv, cache_fox_k=cache_fox_k, cache_fox_v=cache_fox_v,
              cache_fox_lf=cache_fox_lf, page_table=page_table, cache_mem_k=cache_mem_k, cache_mem_v=cache_mem_v)
    y_sample, sn = _run_group(x_sample, W, st, False)
    return (y_prompt, y_sample,
            pn["ssd"], pn["ssd_conv"], pn["gdn"], pn["gdn_conv"],
            pn["fox_k"], pn["fox_v"], pn["fox_lf"], pn["mem_k"], pn["mem_v"],
            sn["ssd"], sn["ssd_conv"], sn["gdn"], sn["gdn_conv"],
            sn["fox_k"], sn["fox_v"], sn["fox_lf"])
```

```python
import functools
import math

import jax
import jax.numpy as jnp
from jax import lax
from jax.experimental import pallas as pl
from jax.experimental.pallas import tpu as pltpu

F32 = jnp.float32
BF16 = jnp.bfloat16
HI = lax.Precision.HIGHEST
NT_DIMS = (((1,), (1,)), ((), ()))
TN_DIMS = (((0,), (0,)), ((), ()))

D_MODEL = 1024
EPS = 1e-6
CONV_K = 4
SSD_HEADS = 16
SSD_HEAD_DIM = 64
SSD_INNER = SSD_HEADS * SSD_HEAD_DIM
SSD_GROUPS = 2
SSD_STATE = 64
SSD_CONV_DIM = SSD_INNER + 2 * SSD_GROUPS * SSD_STATE
GDN_HEADS = 8
GDN_DK = 128
GDN_DV = 128
GDN_QK = GDN_HEADS * GDN_DK
GDN_VW = GDN_HEADS * GDN_DV
GDN_CONV_DIM = 2 * GDN_QK + GDN_VW
FOX_HEADS = 16
FOX_HEAD_DIM = 64
FOX_W = FOX_HEADS * FOX_HEAD_DIM
PAGE_SIZE = 128
MEM_LEN = 256
X_HEADS = 4
X_HEAD_DIM = 128
X_W = X_HEADS * X_HEAD_DIM
LANES = 128
VMEM_LIMIT = 56 * 1024 * 1024


def _cparams(*sem):
    return pltpu.CompilerParams(dimension_semantics=sem, vmem_limit_bytes=VMEM_LIMIT)


def _pick(n, cands):
    for c in cands:
        if n % c == 0:
            return c
    return n


def _softplus(x):
    return jnp.maximum(x, 0.0) + jnp.log1p(jnp.exp(-jnp.abs(x)))


def _log_sigmoid(x):
    return jnp.minimum(x, 0.0) - jnp.log1p(jnp.exp(-jnp.abs(x)))


def _sigmoid(x):
    return 1.0 / (1.0 + jnp.exp(-x))


def _silu(x):
    return x * _sigmoid(x)


def _rms_rows(x):
    return x * lax.rsqrt(jnp.mean(x * x, axis=-1, keepdims=True) + EPS)


def _dot(a, b):
    return jnp.dot(a, b, preferred_element_type=F32)


def _dot_hi(a, b):
    return jnp.dot(a, b, preferred_element_type=F32, precision=HI)


def _dot_nt(a, b):
    return lax.dot_general(a, b, NT_DIMS, preferred_element_type=F32)


def _dot_tn(a, b):
    return lax.dot_general(a, b, TN_DIMS, preferred_element_type=F32)


def _iota(shape, dim):
    return lax.broadcasted_iota(jnp.int32, shape, dim)


def _linear_body(*refs, n_x, use_norm, act, has_res):
    it = iter(refs)
    x_refs = [next(it) for _ in range(n_x)]
    g_ref = next(it) if use_norm else None
    w_refs = [next(it) for _ in range(n_x)]
    b_ref = next(it) if act else None
    r_ref = next(it) if has_res else None
    o_ref = next(it)
    xs_refs = [next(it) for _ in range(n_x)]

    @pl.when(pl.program_id(1) == 0)
    def _():
        for i in range(n_x):
            x = x_refs[i][...]
            if use_norm and i == 0:
                x = _rms_rows(x) * g_ref[...]
            xs_refs[i][...] = x.astype(BF16)

    acc = _dot(xs_refs[0][...], w_refs[0][...])
    for i in range(1, n_x):
        acc = acc + _dot(xs_refs[i][...], w_refs[i][...])
    if act == "log_sigmoid":
        acc = _log_sigmoid(acc + b_ref[...])
    if has_res:
        acc = acc + r_ref[...]
    o_ref[...] = acc


def fused_linear(xs, ws, gain=None, bias=None, act=None, residual=None, name="linear"):
    m = xs[0].shape[0]
    n = ws[0].shape[1]
    tm = _pick(m, (1024, 512, 256, 128, 64, 32, 16, 8))
    tn = _pick(n, (512, 640, 384, 256, 128))
    n_x = len(xs)
    in_specs, args = [], []
    for x in xs:
        in_specs.append(pl.BlockSpec((tm, x.shape[1]), lambda i, j: (i, 0)))
        args.append(x)
    if gain is not None:
        in_specs.append(pl.BlockSpec((1, xs[0].shape[1]), lambda i, j: (0, 0)))
        args.append(gain.reshape(1, -1).astype(F32))
    for w in ws:
        in_specs.append(pl.BlockSpec((w.shape[0], tn), lambda i, j: (0, j)))
        args.append(w)
    if act:
        in_specs.append(pl.BlockSpec((1, tn), lambda i, j: (0, j)))
        args.append(bias.reshape(1, -1).astype(F32))
    if residual is not None:
        in_specs.append(pl.BlockSpec((tm, tn), lambda i, j: (i, j)))
        args.append(residual)
    body = functools.partial(_linear_body, n_x=n_x, use_norm=gain is not None, act=act,
                             has_res=residual is not None)
    return pl.pallas_call(
        body,
        grid=(m // tm, n // tn),
        in_specs=in_specs,
        out_specs=pl.BlockSpec((tm, tn), lambda i, j: (i, j)),
        out_shape=jax.ShapeDtypeStruct((m, n), F32),
        scratch_shapes=[pltpu.VMEM((tm, x.shape[1]), BF16) for x in xs],
        compiler_params=_cparams("parallel", "arbitrary"),
        name=name,
    )(*args)


def _rmsnorm_body(x_ref, g_ref, o_ref):
    o_ref[...] = _rms_rows(x_ref[...]) * g_ref[...]


def rmsnorm_rows(x, gain):
    m, d = x.shape
    tm = _pick(m, (1024, 512, 256, 128, 64, 32, 16, 8))
    return pl.pallas_call(
        _rmsnorm_body,
        grid=(m // tm,),
        in_specs=[pl.BlockSpec((tm, d), lambda i: (i, 0)), pl.BlockSpec((1, d), lambda i: (0, 0))],
        out_specs=pl.BlockSpec((tm, d), lambda i: (i, 0)),
        out_shape=jax.ShapeDtypeStruct((m, d), F32),
        compiler_params=_cparams("parallel"),
        name="final_norm",
    )(x, gain.reshape(1, d))


SUB = 8


def _conv_body(x_ref, prev_ref, buf_ref, w_ref, b_ref, o_ref, ext_ref, *, tt):
    first = pl.program_id(1) == 0
    ext_ref[0:SUB, :] = jnp.where(first, buf_ref[0], prev_ref[0])
    ext_ref[SUB:SUB + tt, :] = x_ref[0]
    w = w_ref[...]
    off = SUB - (CONV_K - 1)
    y = ext_ref[off:off + tt, :] * w[0:1, :]
    for j in range(1, CONV_K):
        y = y + ext_ref[off + j:off + j + tt, :] * w[j:j + 1, :]
    y = y + b_ref[...]
    o_ref[0] = _silu(y)


def conv_silu(x, buf, w, b):
    bsz, seq, ch = x.shape
    tt = _pick(seq, (512, 256, 128, 64, 32, 16, 8))
    tc = _pick(ch, (1024, 640, 512, 256, 128))
    bufp = jnp.concatenate([jnp.zeros((bsz, SUB - (CONV_K - 1), ch), F32), buf], axis=1)
    wp = jnp.concatenate([w, jnp.zeros((SUB - CONV_K, ch), F32)], axis=0)
    nsub = tt // SUB
    return pl.pallas_call(
        functools.partial(_conv_body, tt=tt),
        grid=(bsz, seq // tt, ch // tc),
        in_specs=[
            pl.BlockSpec((1, tt, tc), lambda bi, ti, ci: (bi, ti, ci)),
            pl.BlockSpec((1, SUB, tc), lambda bi, ti, ci: (bi, jnp.maximum(ti * nsub - 1, 0), ci)),
            pl.BlockSpec((1, SUB, tc), lambda bi, ti, ci: (bi, 0, ci)),
            pl.BlockSpec((SUB, tc), lambda bi, ti, ci: (0, ci)),
            pl.BlockSpec((1, tc), lambda bi, ti, ci: (0, ci)),
        ],
        out_specs=pl.BlockSpec((1, tt, tc), lambda bi, ti, ci: (bi, ti, ci)),
        out_shape=jax.ShapeDtypeStruct((bsz, seq, ch), F32),
        scratch_shapes=[pltpu.VMEM((SUB + tt, tc), F32)],
        compiler_params=_cparams("parallel", "parallel", "parallel"),
        name="conv_silu",
    )(x, x, bufp, wp, b.reshape(1, ch))


HG = SSD_HEADS // SSD_GROUPS
GW = HG * SSD_HEAD_DIM


def _expand_heads(v, e_bf16):
    hi = v.astype(BF16)
    r1 = v - hi.astype(F32)
    mid = r1.astype(BF16)
    lo = (r1 - mid.astype(F32)).astype(BF16)
    return _dot(hi, e_bf16) + _dot(mid, e_bf16) + _dot(lo, e_bf16)


def _ssd_body(xbc_ref, z_ref, small_ref, dtt_ref, h0_ref, prow_ref, pcol_ref, dexp_ref, norm_ref,
              y_ref, hout_ref, h_sc, y_sc, *, q):
    @pl.when(pl.program_id(1) == 0)
    def _():
        h_sc[...] = h0_ref[0]

    xbc = xbc_ref[0]
    xs = xbc[:, :SSD_INNER]
    bm = xbc[:, SSD_INNER:SSD_INNER + LANES]
    cm = xbc[:, SSD_INNER + LANES:SSD_INNER + 2 * LANES]
    dt_c = _softplus(small_ref[0][:, 0:SSD_HEADS] + prow_ref[0:1, 0:SSD_HEADS])
    a_c = dt_c * (-jnp.exp(prow_ref[1:2, 0:SSD_HEADS]))
    dt_t = _softplus(dtt_ref[0] + pcol_ref[:, 0:1])
    a_t = dt_t * (-jnp.exp(pcol_ref[:, 1:2]))
    ri = _iota((q, q), 0)
    ci = _iota((q, q), 1)
    causal = ci <= ri
    cum = _dot_hi(causal.astype(F32), a_c)
    cum_t = _dot_hi(a_t, (ri <= ci).astype(F32))
    e_heads = (_iota((SSD_HEADS, SSD_INNER), 1) // SSD_HEAD_DIM == _iota((SSD_HEADS, SSD_INNER), 0)).astype(BF16)
    dt_x = _expand_heads(dt_c, e_heads)
    ecum_x = _expand_heads(jnp.exp(cum), e_heads)
    wlast_x = _expand_heads(jnp.exp(cum[q - 1:q, :] - cum), e_heads)
    xdt = xs * dt_x
    xw = (xdt * wlast_x).astype(BF16)
    xdt_b = xdt.astype(BF16)

    lane = _iota((1, LANES), 1)
    low = lane < SSD_STATE
    bm_b = bm.astype(BF16)
    hs = h_sc[...]
    hs_b = hs.astype(BF16)
    y_inter = []
    cbs = []
    upd = []
    for g in range(SSD_GROUPS):
        cm_g = jnp.where(low if g == 0 else jnp.logical_not(low), cm, 0.0).astype(BF16)
        cbs.append(_dot_nt(cm_g, bm_b))
        y_inter.append(_dot(cm_g, hs_b))
        upd.append(_dot_tn(bm_b, xw[:, g * GW:(g + 1) * GW]))
    for j in range(SSD_HEADS // 2):
        g = (2 * j) // HG
        xp = xdt_b[:, j * LANES:(j + 1) * LANES]
        ys = []
        for hh in (2 * j, 2 * j + 1):
            seg = cum[:, hh:hh + 1] - cum_t[hh:hh + 1, :]
            lm = jnp.exp(jnp.where(causal, seg, -jnp.inf))
            ys.append(_dot((cbs[g] * lm).astype(BF16), xp))
        y_sc[:, j * LANES:(j + 1) * LANES] = jnp.where(low, ys[0], ys[1])
    y = y_sc[...] + jnp.concatenate(y_inter, axis=1) * ecum_x + dexp_ref[...] * xs
    y = y * _silu(z_ref[0])
    y = jnp.concatenate([_rms_rows(y[:, g * GW:(g + 1) * GW]) for g in range(SSD_GROUPS)], axis=1)
    y_ref[0] = y * norm_ref[...]

    row_low = _iota((2 * SSD_STATE, 1), 0) < SSD_STATE
    ecl = ecum_x[q - 1:q, :]
    decay = jnp.where(row_low, ecl[:, 0:GW], ecl[:, GW:2 * GW])
    h_new = hs * decay + jnp.where(row_low, upd[0], upd[1])
    h_sc[...] = h_new

    @pl.when(pl.program_id(1) == pl.num_programs(1) - 1)
    def _():
        hout_ref[0] = h_new


def ssd_scan(xbc_c, z, small, h0, dt_bias, a_log, d_skip, norm_w):
    bsz, seq, _ = xbc_c.shape
    q = _pick(seq, (128, 64, 32, 16, 8))
    dtt = jnp.swapaxes(small[:, :, 0:SSD_HEADS], 1, 2)
    hs0 = h0.reshape(bsz, SSD_GROUPS, HG, SSD_HEAD_DIM, SSD_STATE).transpose(0, 1, 4, 2, 3)
    hs0 = hs0.reshape(bsz, SSD_GROUPS * SSD_STATE, GW)
    prow = jnp.zeros((SUB, LANES), F32).at[0, :SSD_HEADS].set(dt_bias).at[1, :SSD_HEADS].set(a_log)
    pcol = jnp.zeros((SSD_HEADS, LANES), F32).at[:, 0].set(dt_bias).at[:, 1].set(a_log)
    dexp = jnp.repeat(d_skip, SSD_HEAD_DIM).reshape(1, SSD_INNER)
    y, hs = pl.pallas_call(
        functools.partial(_ssd_body, q=q),
        grid=(bsz, seq // q),
        in_specs=[
            pl.BlockSpec((1, q, SSD_CONV_DIM), lambda b, c: (b, c, 0)),
            pl.BlockSpec((1, q, SSD_INNER), lambda b, c: (b, c, 0)),
            pl.BlockSpec((1, q, LANES), lambda b, c: (b, c, 0)),
            pl.BlockSpec((1, SSD_HEADS, q), lambda b, c: (b, 0, c)),
            pl.BlockSpec((1, 2 * SSD_STATE, GW), lambda b, c: (b, 0, 0)),
            pl.BlockSpec((SUB, LANES), lambda b, c: (0, 0)),
            pl.BlockSpec((SSD_HEADS, LANES), lambda b, c: (0, 0)),
            pl.BlockSpec((1, SSD_INNER), lambda b, c: (0, 0)),
            pl.BlockSpec((1, SSD_INNER), lambda b, c: (0, 0)),
        ],
        out_specs=[
            pl.BlockSpec((1, q, SSD_INNER), lambda b, c: (b, c, 0)),
            pl.BlockSpec((1, 2 * SSD_STATE, GW), lambda b, c: (b, 0, 0)),
        ],
        out_shape=[
            jax.ShapeDtypeStruct((bsz, seq, SSD_INNER), F32),
            jax.ShapeDtypeStruct((bsz, 2 * SSD_STATE, GW), F32),
        ],
        scratch_shapes=[pltpu.VMEM((2 * SSD_STATE, GW), F32), pltpu.VMEM((q, SSD_INNER), F32)],
        compiler_params=_cparams("parallel", "arbitrary"),
        name="ssd_scan",
    )(xbc_c, z, small, dtt, hs0, prow, pcol, dexp, norm_w.reshape(1, SSD_INNER))
    h_new = hs.reshape(bsz, SSD_GROUPS, SSD_STATE, HG, SSD_HEAD_DIM).transpose(0, 1, 3, 4, 2)
    return y, h_new.reshape(bsz, SSD_HEADS, SSD_HEAD_DIM, SSD_STATE)


def _split2(x):
    hi = x.astype(BF16)
    return hi, (x - hi.astype(F32)).astype(BF16)


def _dot3(a, b):
    ah, al = _split2(a)
    bh, bl = _split2(b)
    return _dot(ah, bh) + (_dot(ah, bl) + _dot(al, bh))


def _gdn_prep_body(q_ref, k_ref, v_ref, small_ref, gbt_ref, prow_ref,
                   u_ref, w_ref, qg_ref, kd_ref, qk_ref, eg_ref, *, c, nch):
    h = pl.program_id(1)
    lane = _iota((1, LANES), 1)

    def pick_lane(x, idx):
        return jnp.sum(jnp.where(lane == idx, x, 0.0), axis=1, keepdims=True)

    neg_a = -jnp.exp(pick_lane(prow_ref[0:1, :], h))
    dtb = pick_lane(prow_ref[1:2, :], h)
    small = small_ref[0]
    beta_col = _sigmoid(pick_lane(small, SSD_HEADS + h))
    g_col = neg_a * _softplus(pick_lane(small, SSD_HEADS + GDN_HEADS + h) + dtb)
    g_row = neg_a * _softplus(gbt_ref[0, pl.ds(GDN_HEADS + h, 1), :] + dtb)

    ri = _iota((c, c), 0)
    ci = _iota((c, c), 1)
    tril = ri >= ci
    tril_f = tril.astype(F32)
    triu_f = (ri <= ci).astype(F32)
    eye = (ri == ci).astype(F32)
    chunks = range(nch)
    rows = [slice(ch * c, (ch + 1) * c) for ch in chunks]
    qn, kn, kb, gam, dec, nmat = [], [], [], [], [], []
    for ch in chunks:
        qc = q_ref[0, rows[ch], :]
        kc = k_ref[0, rows[ch], :]
        qn.append(qc * lax.rsqrt(jnp.sum(qc * qc, axis=-1, keepdims=True) + 1e-6) * (GDN_DK ** -0.5))
        kn.append(kc * lax.rsqrt(jnp.sum(kc * kc, axis=-1, keepdims=True) + 1e-6))
        kb.append(kn[ch] * beta_col[rows[ch], :])
        gam.append(_dot_hi(tril_f, jnp.broadcast_to(g_col[rows[ch], :], (c, LANES))))
    for ch in chunks:
        gam_r = _dot_hi(jnp.broadcast_to(g_row[:, ch * c:(ch + 1) * c], (SUB, c)), triu_f)[0:1, :]
        dec.append(jnp.exp(jnp.where(tril, gam[ch][:, 0:c] - gam_r, -jnp.inf)))
        nmat.append(jnp.where(ri > ci, _dot_nt(kb[ch].astype(BF16), kn[ch].astype(BF16)) * dec[ch], 0.0) * -1.0)
    tinv = [eye + n for n in nmat]
    npow = [_dot3(n, n) for n in nmat]
    levels = int(math.log2(c))
    for lvl in range(1, levels):
        for ch in chunks:
            if lvl == levels - 1:
                tinv[ch] = tinv[ch] + _dot3(npow[ch], tinv[ch])
            else:
                both = _dot3(npow[ch], jnp.concatenate([tinv[ch], npow[ch]], axis=1))
                tinv[ch] = tinv[ch] + both[:, :c]
                npow[ch] = both[:, c:]
    for ch in chunks:
        egam = jnp.exp(gam[ch])
        vb = (v_ref[0, rows[ch], :] * beta_col[rows[ch], :]).astype(BF16)
        uw = _dot(tinv[ch].astype(BF16), jnp.concatenate([vb, (kb[ch] * egam).astype(BF16)], axis=1))
        glast = gam[ch][c - 1:c, :]
        u_ref[0, rows[ch], :] = uw[:, :GDN_DV]
        w_ref[0, rows[ch], :] = uw[:, GDN_DV:].astype(w_ref.dtype)
        qg_ref[0, rows[ch], :] = (qn[ch] * egam).astype(qg_ref.dtype)
        kd_ref[0, rows[ch], :] = (kn[ch] * jnp.exp(glast - gam[ch])).astype(kd_ref.dtype)
        qk_ref[0, 0, rows[ch], :] = (_dot_nt(qn[ch].astype(BF16), kn[ch].astype(BF16)) * dec[ch]).astype(qk_ref.dtype)
        eg_ref[0, 0, ch:ch + 1, :] = jnp.exp(glast)


GDN_GROUP = 4


def _gdn_scan_body(u_ref, w_ref, qg_ref, kd_ref, qk_ref, eg_ref, gate_ref, s0_ref, norm_ref,
                   o_ref, sout_ref, s_sc, *, c, nch):
    @pl.when(pl.program_id(2) == 0)
    def _():
        s_sc[...] = s0_ref[0]

    s = [s_sc[i] for i in range(GDN_GROUP)]
    for ch in range(nch):
        rows = slice(ch * c, (ch + 1) * c)
        for i in range(GDN_GROUP):
            lanes = slice(i * GDN_DV, (i + 1) * GDN_DV)
            s_b = s[i].astype(BF16)
            wq = jnp.concatenate([w_ref[0, rows, lanes].astype(BF16), qg_ref[0, rows, lanes].astype(BF16)], axis=0)
            ws = _dot(wq, s_b)
            vn_b = (u_ref[0, rows, lanes] - ws[:c]).astype(BF16)
            o = ws[c:] + _dot(qk_ref[0, i, rows, :].astype(BF16), vn_b)
            s[i] = s[i] * eg_ref[0, i, ch:ch + 1, :] + _dot_tn(kd_ref[0, rows, lanes].astype(BF16), vn_b)
            o_ref[0, rows, lanes] = _rms_rows(o) * norm_ref[...] * _silu(gate_ref[0, rows, lanes])
    for i in range(GDN_GROUP):
        s_sc[i] = s[i]

    @pl.when(pl.program_id(2) == pl.num_programs(2) - 1)
    def _():
        for i in range(GDN_GROUP):
            sout_ref[0, i] = s[i]


def gdn_scan(qkv_c, gate, small, s0, dt_bias, a_log, norm_w):
    bsz, seq, _ = qkv_c.shape
    c = _pick(seq, (64, 32, 16, 8))
    tb = _pick(seq, (512, 256, 128, 64, 32, 16, 8))
    nch = tb // c
    wdt = BF16 if c % 16 == 0 else F32
    gbt = jnp.swapaxes(small[:, :, SSD_HEADS:SSD_HEADS + 2 * GDN_HEADS], 1, 2)
    prow = jnp.zeros((SUB, LANES), F32).at[0, :GDN_HEADS].set(a_log).at[1, :GDN_HEADS].set(dt_bias)
    head_blk = pl.BlockSpec((1, tb, GDN_DV), lambda b, h, t: (b, t, h))
    u, w, qg, kd, qk, eg = pl.pallas_call(
        functools.partial(_gdn_prep_body, c=c, nch=nch),
        grid=(bsz, GDN_HEADS, seq // tb),
        in_specs=[
            pl.BlockSpec((1, tb, GDN_DK), lambda b, h, t: (b, t, h)),
            pl.BlockSpec((1, tb, GDN_DK), lambda b, h, t: (b, t, GDN_HEADS + h)),
            pl.BlockSpec((1, tb, GDN_DV), lambda b, h, t: (b, t, 2 * GDN_HEADS + h)),
            pl.BlockSpec((1, tb, LANES), lambda b, h, t: (b, t, 0)),
            pl.BlockSpec((1, 2 * GDN_HEADS, tb), lambda b, h, t: (b, 0, t)),
            pl.BlockSpec((SUB, LANES), lambda b, h, t: (0, 0)),
        ],
        out_specs=[head_blk, head_blk, head_blk, head_blk,
                   pl.BlockSpec((1, 1, tb, c), lambda b, h, t: (b, h, t, 0)),
                   pl.BlockSpec((1, 1, nch, LANES), lambda b, h, t: (b, h, t, 0))],
        out_shape=[
            jax.ShapeDtypeStruct((bsz, seq, GDN_VW), F32),
            jax.ShapeDtypeStruct((bsz, seq, GDN_VW), wdt),
            jax.ShapeDtypeStruct((bsz, seq, GDN_QK), wdt),
            jax.ShapeDtypeStruct((bsz, seq, GDN_QK), wdt),
            jax.ShapeDtypeStruct((bsz, GDN_HEADS, seq, c), wdt),
            jax.ShapeDtypeStruct((bsz, GDN_HEADS, seq // c, LANES), F32),
        ],
        compiler_params=_cparams("parallel", "parallel", "parallel"),
        name="gdn_prep",
    )(qkv_c, qkv_c, qkv_c, small, gbt, prow)
    gw = GDN_GROUP * GDN_DV
    grp_blk = pl.BlockSpec((1, tb, gw), lambda b, g, t: (b, t, g))
    state_blk = pl.BlockSpec((1, GDN_GROUP, GDN_DK, GDN_DV), lambda b, g, t: (b, g, 0, 0))
    o, s_new = pl.pallas_call(
        functools.partial(_gdn_scan_body, c=c, nch=nch),
        grid=(bsz, GDN_HEADS // GDN_GROUP, seq // tb),
        in_specs=[
            grp_blk, grp_blk, grp_blk, grp_blk,
            pl.BlockSpec((1, GDN_GROUP, tb, c), lambda b, g, t: (b, g, t, 0)),
            pl.BlockSpec((1, GDN_GROUP, nch, LANES), lambda b, g, t: (b, g, t, 0)),
            grp_blk,
            state_blk,
            pl.BlockSpec((1, GDN_DV), lambda b, g, t: (0, 0)),
        ],
        out_specs=[grp_blk, state_blk],
        out_shape=[
            jax.ShapeDtypeStruct((bsz, seq, GDN_VW), F32),
            jax.ShapeDtypeStruct((bsz, GDN_HEADS, GDN_DK, GDN_DV), F32),
        ],
        scratch_shapes=[pltpu.VMEM((GDN_GROUP, GDN_DK, GDN_DV), F32)],
        compiler_params=_cparams("parallel", "parallel", "arbitrary"),
        name="gdn_scan",
    )(u, w, qg, kd, qk, eg, gate, s0, norm_w.reshape(1, GDN_DV))
    return o, s_new


def _xattn_body(h_ref, g_ref, wq_ref, mk_ref, mv_ref, wo_ref, o_ref):
    x = h_ref[0]
    xn = (_rms_rows(x) * g_ref[...]).astype(BF16)
    qf = _dot(xn, wq_ref[...])
    mk = mk_ref[0].astype(BF16)
    mv = mv_ref[0].astype(BF16)
    outs = []
    for hd in range(X_HEADS):
        sl = slice(hd * X_HEAD_DIM, (hd + 1) * X_HEAD_DIM)
        s = _dot_nt(qf[:, sl].astype(BF16), mk[:, sl]) * (X_HEAD_DIM ** -0.5)
        p = jnp.exp(s - jnp.max(s, axis=-1, keepdims=True))
        p = p / jnp.sum(p, axis=-1, keepdims=True)
        outs.append(_dot(p.astype(BF16), mv[:, sl]))
    o = jnp.concatenate(outs, axis=1).astype(BF16)
    o_ref[0] = x + _dot(o, wo_ref[...])


def cross_attn(h, gain, wq, mk, mv, wo):
    bsz, seq, d = h.shape
    tm = _pick(seq, (512, 256, 128, 64, 32, 16, 8))
    return pl.pallas_call(
        _xattn_body,
        grid=(bsz, seq // tm),
        in_specs=[
            pl.BlockSpec((1, tm, d), lambda b, i: (b, i, 0)),
            pl.BlockSpec((1, d), lambda b, i: (0, 0)),
            pl.BlockSpec((d, X_W), lambda b, i: (0, 0)),
            pl.BlockSpec((1, MEM_LEN, X_W), lambda b, i: (b, 0, 0)),
            pl.BlockSpec((1, MEM_LEN, X_W), lambda b, i: (b, 0, 0)),
            pl.BlockSpec((X_W, d), lambda b, i: (0, 0)),
        ],
        out_specs=pl.BlockSpec((1, tm, d), lambda b, i: (b, i, 0)),
        out_shape=jax.ShapeDtypeStruct((bsz, seq, d), F32),
        compiler_params=_cparams("parallel", "parallel"),
        name="cross_attn",
    )(h, gain.reshape(1, d), wq, mk, mv, wo)


def _swiglu_body(h_ref, g_ref, w1_ref, w3_ref, w2_ref, o_ref, xn_sc):
    j = pl.program_id(1)

    @pl.when(j == 0)
    def _():
        x = h_ref[...]
        xn_sc[...] = (_rms_rows(x) * g_ref[...]).astype(BF16)
        o_ref[...] = x

    xn = xn_sc[...]
    a = _dot(xn, w1_ref[...])
    b = _dot(xn, w3_ref[...])
    o_ref[...] += _dot((_silu(a) * b).astype(BF16), w2_ref[...])


def swiglu(h, gain, w1, w3, w2):
    m, d = h.shape
    ff = w1.shape[1]
    tm = _pick(m, (512, 256, 128, 64, 32, 16, 8))
    tf = _pick(ff, (1408, 1024, 512, 256, 128))
    return pl.pallas_call(
        _swiglu_body,
        grid=(m // tm, ff // tf),
        in_specs=[
            pl.BlockSpec((tm, d), lambda i, j: (i, 0)),
            pl.BlockSpec((1, d), lambda i, j: (0, 0)),
            pl.BlockSpec((d, tf), lambda i, j: (0, j)),
            pl.BlockSpec((d, tf), lambda i, j: (0, j)),
            pl.BlockSpec((tf, d), lambda i, j: (j, 0)),
        ],
        out_specs=pl.BlockSpec((tm, d), lambda i, j: (i, 0)),
        out_shape=jax.ShapeDtypeStruct((m, d), F32),
        scratch_shapes=[pltpu.VMEM((tm, d), BF16)],
        compiler_params=_cparams("parallel", "arbitrary"),
        name="swiglu",
    )(h, gain.reshape(1, d), w1, w3, w2)


def _cumsum_lanes_body(x_ref, o_ref, carry_sc, *, tc):
    @pl.when(pl.program_id(1) == 0)
    def _():
        carry_sc[...] = jnp.zeros_like(carry_sc)

    upper = (_iota((tc, tc), 0) <= _iota((tc, tc), 1)).astype(F32)
    f = _dot_hi(x_ref[0], upper) + carry_sc[:, 0:1]
    o_ref[0] = f
    carry_sc[...] = jnp.broadcast_to(f[:, tc - 1:tc], carry_sc.shape)


def cumsum_lanes(x):
    bsz, r, seq = x.shape
    tc = _pick(seq, (512, 256, 128))
    return pl.pallas_call(
        functools.partial(_cumsum_lanes_body, tc=tc),
        grid=(bsz, seq // tc),
        in_specs=[pl.BlockSpec((1, r, tc), lambda b, i: (b, 0, i))],
        out_specs=pl.BlockSpec((1, r, tc), lambda b, i: (b, 0, i)),
        out_shape=jax.ShapeDtypeStruct((bsz, r, seq), F32),
        scratch_shapes=[pltpu.VMEM((r, LANES), F32)],
        compiler_params=_cparams("parallel", "arbitrary"),
        name="cumsum_lanes",
    )(x)


FOX_ROWS = 512
FOX_PAIRS = 2


def _fox_body(qi_ref, ki_ref, q_ref, k_ref, v_ref, f_ref, o_ref, m_sc, l_sc, acc_sc, *, t):
    step = pl.program_id(2)
    qi = qi_ref[step]
    ki = ki_ref[step]
    low = _iota((1, LANES), 1) < FOX_HEAD_DIM
    rb = min(FOX_ROWS, t)

    @pl.when(ki == 0)
    def _():
        m_sc[...] = jnp.full(m_sc.shape, -jnp.inf, F32)
        l_sc[...] = jnp.zeros_like(l_sc)
        acc_sc[...] = jnp.zeros_like(acc_sc)

    def update(masked):
        units = [(pr, r, hh) for pr in range(FOX_PAIRS) for r in range(t // rb) for hh in range(2)]
        kbs, vbs = {}, {}

        def nkeys(r):
            return (r + 1) * rb if masked else t

        def qk(u):
            pr, r, hh = u
            lanes = slice(pr * LANES, (pr + 1) * LANES)
            if pr not in kbs:
                kbs[pr] = k_ref[0, :, lanes].astype(BF16)
                vbs[pr] = v_ref[0, :, lanes].astype(BF16)
            qs = q_ref[0, r * rb:(r + 1) * rb, lanes] * (FOX_HEAD_DIM ** -0.5)
            qm = jnp.where(low if hh == 0 else jnp.logical_not(low), qs, 0.0).astype(BF16)
            return _dot_nt(qm, kbs[pr][:nkeys(r)])

        s_next = qk(units[0])
        for i, (pr, r, hh) in enumerate(units):
            rows = slice(r * rb, (r + 1) * rb)
            nk = nkeys(r)
            s = s_next - f_ref[0, pr, hh:hh + 1, :nk]
            if i + 1 < len(units):
                s_next = qk(units[i + 1])
            if masked:
                s = jnp.where(_iota((rb, nk), 1) <= _iota((rb, nk), 0) + r * rb, s, -jnp.inf)
            m_prev = m_sc[pr, hh, rows, :]
            m_new = jnp.maximum(m_prev, jnp.max(s, axis=-1, keepdims=True))
            alpha = jnp.exp(m_prev - m_new)
            p = jnp.exp(s - jnp.concatenate([m_new] * (nk // LANES), axis=1))
            l_sc[pr, hh, rows, :] = alpha * l_sc[pr, hh, rows, :] + jnp.sum(p, axis=-1, keepdims=True)
            m_sc[pr, hh, rows, :] = m_new
            acc_sc[pr, hh, rows, :] = alpha * acc_sc[pr, hh, rows, :] + _dot(p.astype(BF16), vbs[pr][:nk])

    @pl.when(ki < qi)
    def _():
        update(False)

    @pl.when(ki == qi)
    def _():
        update(True)
        for pr in range(FOX_PAIRS):
            o_ref[0, :, pr * LANES:(pr + 1) * LANES] = jnp.where(
                low, acc_sc[pr, 0] / l_sc[pr, 0], acc_sc[pr, 1] / l_sc[pr, 1])


def fox_prompt_attn(q, k, v, ft):
    bsz, seq, _ = q.shape
    t = _pick(seq, (1024, 512, 256, 128))
    nb = seq // t
    pairs =[(i, j) for i in range(nb) for j in range(i + 1)]
    qi = jnp.asarray([p[0] for p in pairs], jnp.int32)
    ki = jnp.asarray([p[1] for p in pairs], jnp.int32)
    npair = FOX_HEADS // 2
    w = FOX_PAIRS * LANES
    ft4 = ft.reshape(bsz, npair, 2, seq)
    grid_spec = pltpu.PrefetchScalarGridSpec(
        num_scalar_prefetch=2,
        grid=(bsz, npair // FOX_PAIRS, len(pairs)),
        in_specs=[
            pl.BlockSpec((1, t, w), lambda b, j, s, qi, ki: (b, qi[s], j)),
            pl.BlockSpec((1, t, w), lambda b, j, s, qi, ki: (b, ki[s], j)),
            pl.BlockSpec((1, t, w), lambda b, j, s, qi, ki: (b, ki[s], j)),
            pl.BlockSpec((1, FOX_PAIRS, 2, t), lambda b, j, s, qi, ki: (b, j, 0, ki[s])),
        ],
        out_specs=pl.BlockSpec((1, t, w), lambda b, j, s, qi, ki: (b, qi[s], j)),
        scratch_shapes=[pltpu.VMEM((FOX_PAIRS, 2, t, LANES), F32)] * 3,
    )
    return pl.pallas_call(
        functools.partial(_fox_body, t=t),
        grid_spec=grid_spec,
        out_shape=jax.ShapeDtypeStruct((bsz, seq, FOX_W), F32),
        compiler_params=_cparams("parallel", "parallel", "arbitrary"),
        name="fox_prompt_attn",
    )(qi, ki, q, k, v, ft4)


def _page_suffix_body(x_ref, rin_ref, tot_ref):
    x = x_ref[...]
    after = (_iota((PAGE_SIZE, PAGE_SIZE), 0) > _iota((PAGE_SIZE, PAGE_SIZE), 1)).astype(F32)
    rin_ref[...] = _dot_hi(x, after)
    tot_ref[...] = jnp.broadcast_to(jnp.sum(x, axis=1, keepdims=True), x.shape)


def page_suffix(lft):
    r = lft.shape[0]
    tr = _pick(r, (2048, 1024, 512, 256, 128, 64, 32, 16))
    spec = pl.BlockSpec((tr, PAGE_SIZE), lambda i: (i, 0))
    return pl.pallas_call(
        _page_suffix_body,
        grid=(r // tr,),
        in_specs=[spec],
        out_specs=[spec, spec],
        out_shape=[jax.ShapeDtypeStruct(lft.shape, F32)] * 2,
        compiler_params=_cparams("parallel"),
        name="page_suffix",
    )(lft)


DEC_PAGES = 2
PAGE_COLS = PAGE_SIZE * FOX_HEADS


def _fox_decode_body(pt_ref, q_ref, *refs, nt, nsteps):
    k_refs = refs[0:DEC_PAGES]
    v_refs = refs[DEC_PAGES:2 * DEC_PAGES]
    rin_refs = refs[2 * DEC_PAGES:3 * DEC_PAGES]
    tot_refs = refs[3 * DEC_PAGES:4 * DEC_PAGES]
    kn_ref, vn_ref, lfn_ref, o_ref, qall_sc, mask_sc, m_sc, l_sc, acc_sc, run_sc = refs[4 * DEC_PAGES:]
    j = pl.program_id(1)
    rows = FOX_HEADS * nt

    @pl.when(j == 0)
    def _():
        qs = q_ref[0] * (FOX_HEAD_DIM ** -0.5)
        qall_sc[...] = jnp.concatenate(
            [qs[:, hd * FOX_HEAD_DIM:(hd + 1) * FOX_HEAD_DIM] for hd in range(FOX_HEADS)], axis=0).astype(BF16)
        own = _iota((rows, PAGE_COLS), 1) % FOX_HEADS == _iota((rows, PAGE_COLS), 0) // nt
        mask_sc[...] = jnp.where(own, 0.0, -jnp.inf)
        m_sc[...] = jnp.full(m_sc.shape, -jnp.inf, F32)
        l_sc[...] = jnp.zeros_like(l_sc)
        acc_sc[...] = jnp.zeros_like(acc_sc)
        run_sc[...] = jnp.zeros_like(run_sc)

    def update(ss, vbs):
        m_prev = m_sc[...]
        m_new = m_prev
        for s in ss:
            m_new = jnp.maximum(m_new, jnp.max(s, axis=-1, keepdims=True))
        alpha = jnp.exp(m_prev - m_new)
        l_new = alpha * l_sc[...]
        acc = alpha[:, :FOX_HEAD_DIM] * acc_sc[...]
        for s, vb in zip(ss, vbs):
            p = jnp.exp(s - jnp.concatenate([m_new] * (s.shape[1] // LANES), axis=1))
            l_new = l_new + jnp.sum(p, axis=-1, keepdims=True)
            acc = acc + _dot(p.astype(BF16), vb)
        m_sc[...] = m_new
        l_sc[...] = l_new
        acc_sc[...] = acc

    @pl.when(j < nsteps)
    def _():
        qall = qall_sc[...]
        raw = [_dot_nt(qall, k_refs[i][0, 0].reshape(PAGE_COLS, FOX_HEAD_DIM).astype(BF16)) for i in range(DEC_PAGES)]
        run = run_sc[...]
        ss = []
        for i in range(DEC_PAGES):
            ss.append(raw[i] + (mask_sc[...] + (rin_refs[i][0] + run)))
            run = run + tot_refs[i][0]
        run_sc[...] = run
        update(ss, [v_refs[i][0, 0].reshape(PAGE_COLS, FOX_HEAD_DIM).astype(BF16) for i in range(DEC_PAGES)])

    @pl.when(j == nsteps)
    def _():
        s = _dot_nt(qall_sc[...], kn_ref[0].astype(BF16))
        ri = _iota((LANES, LANES), 0)
        ci = _iota((LANES, LANES), 1)
        same_head_before = jnp.logical_and(ri % FOX_HEADS == ci % FOX_HEADS, ri <= ci).astype(F32)
        fn = _dot_hi(jnp.broadcast_to(lfn_ref[0], (SUB, LANES)), same_head_before)[0:1, :]
        col = _iota((rows, LANES), 1)
        row = _iota((rows, LANES), 0)
        keep = jnp.logical_and(col % FOX_HEADS == row // nt, col // FOX_HEADS <= row % nt)
        update([jnp.where(keep, s - fn, -jnp.inf)], [vn_ref[0].astype(BF16)])
        o = acc_sc[...] / l_sc[:, :FOX_HEAD_DIM]
        o_ref[0] = jnp.concatenate([o[hd * nt:(hd + 1) * nt, :] for hd in range(FOX_HEADS)], axis=1)


def fox_decode_attn(q, k_new, v_new, lfn, k_cache, v_cache, layer, rin, tot, page_table):
    bsz, nt, _ = q.shape
    npages = page_table.shape[1]
    assert npages % DEC_PAGES == 0 and nt * FOX_HEADS == LANES
    nsteps = npages // DEC_PAGES
    rows = FOX_HEADS * nt

    def page(i):
        return lambda b, j, pt: pt[b, npages - 1 - (jnp.minimum(j, nsteps - 1) * DEC_PAGES + i)]

    cache_specs = [pl.BlockSpec((1, 1, PAGE_SIZE, FOX_HEADS, FOX_HEAD_DIM),
                                functools.partial(lambda b, j, pt, pg: (layer, pg(b, j, pt), 0, 0, 0), pg=page(i)))
                   for i in range(DEC_PAGES)]
    bias_specs = [pl.BlockSpec((1, 1, PAGE_COLS), functools.partial(lambda b, j, pt, pg: (pg(b, j, pt), 0, 0), pg=page(i)))
                  for i in range(DEC_PAGES)]
    per_seq = lambda shape: pl.BlockSpec(shape, lambda b, j, pt: (b, 0, 0))
    grid_spec = pltpu.PrefetchScalarGridSpec(
        num_scalar_prefetch=1,
        grid=(bsz, nsteps + 1),
        in_specs=[per_seq((1, nt, FOX_W))] + cache_specs + cache_specs + bias_specs + bias_specs
        + [per_seq((1, LANES, FOX_HEAD_DIM)), per_seq((1, LANES, FOX_HEAD_DIM)), per_seq((1, 1, LANES))],
        out_specs=per_seq((1, nt, FOX_W)),
        scratch_shapes=[
            pltpu.VMEM((rows, FOX_HEAD_DIM), BF16),
            pltpu.VMEM((rows, PAGE_COLS), F32),
            pltpu.VMEM((rows, LANES), F32),
            pltpu.VMEM((rows, LANES), F32),
            pltpu.VMEM((rows, FOX_HEAD_DIM), F32),
            pltpu.VMEM((1, PAGE_COLS), F32),
        ],
    )
    return pl.pallas_call(
        functools.partial(_fox_decode_body, nt=nt, nsteps=nsteps),
        grid_spec=grid_spec,
        out_shape=jax.ShapeDtypeStruct((bsz, nt, FOX_W), F32),
        compiler_params=_cparams("parallel", "arbitrary"),
        name="fox_decode_attn",
    )(page_table, q, *([k_cache] * DEC_PAGES), *([v_cache] * DEC_PAGES), *([rin] * DEC_PAGES), *([tot] * DEC_PAGES),
      k_new, v_new, lfn)


def _split_hyb_weights(w_in):
    o = 0
    parts = {}
    for name, width in (("z", SSD_INNER), ("xbc", SSD_CONV_DIM), ("dt", SSD_HEADS), ("qkv", GDN_CONV_DIM),
                        ("gate", GDN_VW), ("b", GDN_HEADS), ("a", GDN_HEADS)):
        parts[name] = w_in[:, o:o + width]
        o += width
    small = jnp.concatenate([parts["dt"], parts["b"], parts["a"]], axis=1)
    small = jnp.pad(small, ((0, 0), (0, LANES - small.shape[1])))
    return {k: parts[k].astype(BF16) for k in ("z", "xbc", "qkv", "gate")} | {"small": small.astype(BF16)}


def _hybrid_layer(h, e, W, st):
    bsz, seq, d = h.shape
    m = bsz * seq
    h2 = h.reshape(m, d)
    wp = _split_hyb_weights(W["w_in_hyb"][e])
    gain = W["norm_mix"][2 * e]
    proj = {k: fused_linear([h2], [wp[k]], gain=gain, name="hyb_in_" + k) for k in ("z", "xbc", "qkv", "gate", "small")}
    xbc = proj["xbc"].reshape(bsz, seq, SSD_CONV_DIM)
    qkv = proj["qkv"].reshape(bsz, seq, GDN_CONV_DIM)
    small = proj["small"].reshape(bsz, seq, LANES)
    xbc_c = conv_silu(xbc, st["ssd_conv"], W["ssd_conv_w"][e], W["ssd_conv_b"][e])
    qkv_c = conv_silu(qkv, st["gdn_conv"], W["gdn_conv_w"][e], jnp.zeros((GDN_CONV_DIM,), F32))
    y, ssd_h = ssd_scan(xbc_c, proj["z"].reshape(bsz, seq, SSD_INNER), small, st["ssd"],
                        W["ssd_dt_bias"][e], W["ssd_A_log"][e], W["ssd_D"][e], W["ssd_norm"][e])
    o, gdn_s = gdn_scan(qkv_c, proj["gate"].reshape(bsz, seq, GDN_VW), small, st["gdn"],
                        W["gdn_dt_bias"][e], W["gdn_A_log"][e], W["gdn_norm"][e])
    w_out = W["w_out_hyb"][e].astype(BF16)
    h_new = fused_linear([y.reshape(m, SSD_INNER), o.reshape(m, GDN_VW)], [w_out[:SSD_INNER], w_out[SSD_INNER:]],
                         residual=h2, name="hyb_out")
    new = dict(ssd=ssd_h, ssd_conv=xbc[:, seq - (CONV_K - 1):], gdn=gdn_s, gdn_conv=qkv[:, seq - (CONV_K - 1):])
    return h_new.reshape(bsz, seq, d), new


def _fox_layer(h, o_idx, layer, W, st, prompt):
    bsz, seq, d = h.shape
    m = bsz * seq
    h2 = h.reshape(m, d)
    w_in = W["w_in_fox"][o_idx]
    gain = W["norm_mix"][layer]
    wq, wk, wv = (w_in[:, i * FOX_W:(i + 1) * FOX_W].astype(BF16) for i in range(3))
    wf = jnp.pad(w_in[:, 3 * FOX_W:], ((0, 0), (0, LANES - FOX_HEADS))).astype(BF16)
    bf = jnp.pad(W["b_fox_f"][o_idx], (0, LANES - FOX_HEADS))
    q = fused_linear([h2], [wq], gain=gain, name="fox_q").reshape(bsz, seq, FOX_W)
    k = fused_linear([h2], [wk], gain=gain, name="fox_k").reshape(bsz, seq, FOX_W)
    v = fused_linear([h2], [wv], gain=gain, name="fox_v").reshape(bsz, seq, FOX_W)
    lf = fused_linear([h2], [wf], gain=gain, bias=bf, act="log_sigmoid", name="fox_f")[:, :FOX_HEADS]
    lf = lf.reshape(bsz, seq, FOX_HEADS)
    lf_t = jnp.swapaxes(lf, 1, 2)
    if prompt:
        att = fox_prompt_attn(q, k, v, cumsum_lanes(lf_t))
    else:
        n_pool = st["cache_fox_k"].shape[1]
        lfc_t = jnp.swapaxes(st["cache_fox_lf"][o_idx].astype(F32), 1, 2).reshape(n_pool * FOX_HEADS, PAGE_SIZE)
        rin, tot = page_suffix(lfc_t)
        rin = jnp.swapaxes(rin.reshape(n_pool, FOX_HEADS, PAGE_SIZE), 1, 2).reshape(n_pool, 1, PAGE_COLS)
        tot = jnp.tile(tot.reshape(n_pool, FOX_HEADS, PAGE_SIZE)[:, :, 0], (1, PAGE_SIZE)).reshape(n_pool, 1, PAGE_COLS)
        rows_new = seq * FOX_HEADS
        att = fox_decode_attn(q, k.reshape(bsz, rows_new, FOX_HEAD_DIM), v.reshape(bsz, rows_new, FOX_HEAD_DIM),
                              lf.reshape(bsz, 1, rows_new), st["cache_fox_k"], st["cache_fox_v"], o_idx,
                              rin, tot, st["page_table"])
    h_new = fused_linear([att.reshape(m, FOX_W)], [W["w_out_fox"][o_idx].astype(BF16)], residual=h2, name="fox_out")
    shp = (bsz, seq, FOX_HEADS, FOX_HEAD_DIM)
    return h_new.reshape(bsz, seq, d), dict(fox_k=k.reshape(shp), fox_v=v.reshape(shp), fox_lf=lf)


def _run_group(x, W, st, prompt):
    bsz, seq, d = x.shape
    depth = W["norm_mix"].shape[0]
    names = ("ssd", "ssd_conv", "gdn", "gdn_conv", "fox_k", "fox_v", "fox_lf", "mem_k", "mem_v")
    out = {n: [] for n in names}
    h = x
    for layer in range(depth):
        if layer % 2 == 0:
            e = layer // 2
            if prompt:
                s0 = dict(ssd=jnp.zeros((bsz, SSD_HEADS, SSD_HEAD_DIM, SSD_STATE), F32),
                          ssd_conv=jnp.zeros((bsz, CONV_K - 1, SSD_CONV_DIM), F32),
                          gdn=jnp.zeros((bsz, GDN_HEADS, GDN_DK, GDN_DV), F32),
                          gdn_conv=jnp.zeros((bsz, CONV_K - 1, GDN_CONV_DIM), F32))
            else:
                s0 = dict(ssd=st["state_ssd"][e], ssd_conv=st["state_ssd_conv"][e],
                          gdn=st["state_gdn"][e], gdn_conv=st["state_gdn_conv"][e])
            h, new = _hybrid_layer(h, e, W, s0)
        else:
            h, new = _fox_layer(h, layer // 2, layer, W, st, prompt)
        for n, val in new.items():
            out[n].append(val)
        if prompt:
            mem = st["mem"]
            mem2 = mem.reshape(bsz * MEM_LEN, d)
            mk = fused_linear([mem2], [W["wk_x"][layer].astype(BF16)], gain=W["norm_mem"][layer], name="mem_k")
            mv = fused_linear([mem2], [W["wv_x"][layer].astype(BF16)], gain=W["norm_mem"][layer], name="mem_v")
            mk = mk.reshape(bsz, MEM_LEN, X_W)
            mv = mv.reshape(bsz, MEM_LEN, X_W)
            out["mem_k"].append(mk.reshape(bsz, MEM_LEN, X_HEADS, X_HEAD_DIM))
            out["mem_v"].append(mv.reshape(bsz, MEM_LEN, X_HEADS, X_HEAD_DIM))
        else:
            mk = st["cache_mem_k"][layer].reshape(bsz, MEM_LEN, X_W)
            mv = st["cache_mem_v"][layer].reshape(bsz, MEM_LEN, X_W)
        h = cross_attn(h, W["norm_x"][layer], W["wq_x"][layer].astype(BF16), mk, mv, W["wo_x"][layer].astype(BF16))
        h = swiglu(h.reshape(bsz * seq, d), W["norm_ffn"][layer], W["w1"][layer].astype(BF16),
                   W["w3"][layer].astype(BF16), W["w2"][layer].astype(BF16)).reshape(bsz, seq, d)
    y = rmsnorm_rows(h.reshape(bsz * seq, d), W["norm_final"]).reshape(bsz, seq, d)
    new = {n: jnp.stack(out[n]) for n in names if out[n]}
    return y, new


def kernel(x_prompt, x_sample, mem_prompt, state_ssd, state_ssd_conv, state_gdn, state_gdn_conv, cache_fox_k, cache_fox_v, cache_fox_lf, page_table, cache_mem_k, cache_mem_v, norm_mix, norm_x, norm_mem, norm_ffn, norm_final, w_in_hyb, w_out_hyb, ssd_conv_w, ssd_conv_b, ssd_dt_bias, ssd_A_log, ssd_D, ssd_norm, gdn_conv_w, gdn_dt_bias, gdn_A_log, gdn_norm, w_in_fox, b_fox_f, w_out_fox, wq_x, wk_x, wv_x, wo_x, w1, w3, w2):
    W = dict(norm_mix=norm_mix, norm_x=norm_x, norm_mem=norm_mem, norm_ffn=norm_ffn, norm_final=norm_final,
             w_in_hyb=w_in_hyb, w_out_hyb=w_out_hyb, ssd_conv_w=ssd_conv_w, ssd_conv_b=ssd_conv_b,
             ssd_dt_bias=ssd_dt_bias, ssd_A_log=ssd_A_log, ssd_D=ssd_D, ssd_norm=ssd_norm,
             gdn_conv_w=gdn_conv_w, gdn_dt_bias=gdn_dt_bias, gdn_A_log=gdn_A_log, gdn_norm=gdn_norm,
             w_in_fox=w_in_fox, b_fox_f=b_fox_f, w_out_fox=w_out_fox,
             wq_x=wq_x, wk_x=wk_x, wv_x=wv_x, wo_x=wo_x, w1=w1, w3=w3, w2=w2)
    y_prompt, pn = _run_group(x_prompt, W, dict(mem=mem_prompt), True)
    st = dict(state_ssd=state_ssd, state_ssd_conv=state_ssd_conv, state_gdn=state_gdn,
              state_gdn_conv=state_gdn_conv, cache_fox_k=cache_fox_k, cache_fox_v=cache_fox_v,
              cache_fox_lf=cache_fox_lf, page_table=page_table, cache_mem_k=cache_mem_k, cache_mem_v=cache_mem_v)
    y_sample, sn = _run_group(x_sample, W, st, False)
    return (y_prompt, y_sample,
            pn["ssd"], pn["ssd_conv"], pn["gdn"], pn["gdn_conv"],
            pn["fox_k"], pn["fox_v"], pn["fox_lf"], pn["mem_k"], pn["mem_v"],
            sn["ssd"], sn["ssd_conv"], sn["gdn"], sn["gdn_conv"],
            sn["fox_k"], sn["fox_v"], sn["fox_lf"])
```

```python
import functools
import math

import jax
import jax.numpy as jnp
from jax import lax
from jax.experimental import pallas as pl
from jax.experimental.pallas import tpu as pltpu

F32 = jnp.float32
BF16 = jnp.bfloat16
HI = lax.Precision.HIGHEST
NT_DIMS = (((1,), (1,)), ((), ()))
TN_DIMS = (((0,), (0,)), ((), ()))

D_MODEL = 1024
EPS = 1e-6
CONV_K = 4
SSD_HEADS = 16
SSD_HEAD_DIM = 64
SSD_INNER = SSD_HEADS * SSD_HEAD_DIM
SSD_GROUPS = 2
SSD_STATE = 64
SSD_CONV_DIM = SSD_INNER + 2 * SSD_GROUPS * SSD_STATE
GDN_HEADS = 8
GDN_DK = 128
GDN_DV = 128
GDN_QK = GDN_HEADS * GDN_DK
GDN_VW = GDN_HEADS * GDN_DV
GDN_CONV_DIM = 2 * GDN_QK + GDN_VW
FOX_HEADS = 16
FOX_HEAD_DIM = 64
FOX_W = FOX_HEADS * FOX_HEAD_DIM
PAGE_SIZE = 128
MEM_LEN = 256
X_HEADS = 4
X_HEAD_DIM = 128
X_W = X_HEADS * X_HEAD_DIM
LANES = 128
VMEM_LIMIT = 56 * 1024 * 1024


def _cparams(*sem):
    return pltpu.CompilerParams(dimension_semantics=sem, vmem_limit_bytes=VMEM_LIMIT)


def _pick(n, cands):
    for c in cands:
        if n % c == 0:
            return c
    return n


def _softplus(x):
    return jnp.maximum(x, 0.0) + jnp.log1p(jnp.exp(-jnp.abs(x)))


def _log_sigmoid(x):
    return jnp.minimum(x, 0.0) - jnp.log1p(jnp.exp(-jnp.abs(x)))


def _sigmoid(x):
    return 1.0 / (1.0 + jnp.exp(-x))


def _silu(x):
    return x * _sigmoid(x)


def _rms_rows(x):
    return x * lax.rsqrt(jnp.mean(x * x, axis=-1, keepdims=True) + EPS)


def _dot(a, b):
    return jnp.dot(a, b, preferred_element_type=F32)


def _dot_hi(a, b):
    return jnp.dot(a, b, preferred_element_type=F32, precision=HI)


def _dot_nt(a, b):
    return lax.dot_general(a, b, NT_DIMS, preferred_element_type=F32)


def _dot_tn(a, b):
    return lax.dot_general(a, b, TN_DIMS, preferred_element_type=F32)


def _iota(shape, dim):
    return lax.broadcasted_iota(jnp.int32, shape, dim)


def _linear_body(*refs, n_x, use_norm, act, has_res):
    it = iter(refs)
    x_refs = [next(it) for _ in range(n_x)]
    g_ref = next(it) if use_norm else None
    w_refs = [next(it) for _ in range(n_x)]
    b_ref = next(it) if act else None
    r_ref = next(it) if has_res else None
    o_ref = next(it)
    xs_refs = [next(it) for _ in range(n_x)]

    @pl.when(pl.program_id(1) == 0)
    def _():
        for i in range(n_x):
            x = x_refs[i][...]
            if use_norm and i == 0:
                x = _rms_rows(x) * g_ref[...]
            xs_refs[i][...] = x.astype(BF16)

    acc = _dot(xs_refs[0][...], w_refs[0][...])
    for i in range(1, n_x):
        acc = acc + _dot(xs_refs[i][...], w_refs[i][...])
    if act == "log_sigmoid":
        acc = _log_sigmoid(acc + b_ref[...])
    if has_res:
        acc = acc + r_ref[...]
    o_ref[...] = acc


def fused_linear(xs, ws, gain=None, bias=None, act=None, residual=None, name="linear"):
    m = xs[0].shape[0]
    n = ws[0].shape[1]
    tm = _pick(m, (1024, 512, 256, 128, 64, 32, 16, 8))
    tn = _pick(n, (1536, 1280, 1024, 768, 640, 512, 384, 256, 128))
    n_x = len(xs)
    in_specs, args = [], []
    for x in xs:
        in_specs.append(pl.BlockSpec((tm, x.shape[1]), lambda i, j: (i, 0)))
        args.append(x)
    if gain is not None:
        in_specs.append(pl.BlockSpec((1, xs[0].shape[1]), lambda i, j: (0, 0)))
        args.append(gain.reshape(1, -1).astype(F32))
    for w in ws:
        in_specs.append(pl.BlockSpec((w.shape[0], tn), lambda i, j: (0, j)))
        args.append(w)
    if act:
        in_specs.append(pl.BlockSpec((1, tn), lambda i, j: (0, j)))
        args.append(bias.reshape(1, -1).astype(F32))
    if residual is not None:
        in_specs.append(pl.BlockSpec((tm, tn), lambda i, j: (i, j)))
        args.append(residual)
    body = functools.partial(_linear_body, n_x=n_x, use_norm=gain is not None, act=act,
                             has_res=residual is not None)
    return pl.pallas_call(
        body,
        grid=(m // tm, n // tn),
        in_specs=in_specs,
        out_specs=pl.BlockSpec((tm, tn), lambda i, j: (i, j)),
        out_shape=jax.ShapeDtypeStruct((m, n), F32),
        scratch_shapes=[pltpu.VMEM((tm, x.shape[1]), BF16) for x in xs],
        compiler_params=_cparams("parallel", "arbitrary"),
        name=name,
    )(*args)


def _rmsnorm_body(x_ref, g_ref, o_ref):
    o_ref[...] = _rms_rows(x_ref[...]) * g_ref[...]


def rmsnorm_rows(x, gain):
    m, d = x.shape
    tm = _pick(m, (1024, 512, 256, 128, 64, 32, 16, 8))
    return pl.pallas_call(
        _rmsnorm_body,
        grid=(m // tm,),
        in_specs=[pl.BlockSpec((tm, d), lambda i: (i, 0)), pl.BlockSpec((1, d), lambda i: (0, 0))],
        out_specs=pl.BlockSpec((tm, d), lambda i: (i, 0)),
        out_shape=jax.ShapeDtypeStruct((m, d), F32),
        compiler_params=_cparams("parallel"),
        name="final_norm",
    )(x, gain.reshape(1, d))


SUB = 8


def _conv_body(x_ref, prev_ref, buf_ref, w_ref, b_ref, o_ref, ext_ref, *, tt):
    first = pl.program_id(1) == 0
    ext_ref[0:SUB, :] = jnp.where(first, buf_ref[0], prev_ref[0])
    ext_ref[SUB:SUB + tt, :] = x_ref[0]
    w = w_ref[...]
    off = SUB - (CONV_K - 1)
    y = ext_ref[off:off + tt, :] * w[0:1, :]
    for j in range(1, CONV_K):
        y = y + ext_ref[off + j:off + j + tt, :] * w[j:j + 1, :]
    y = y + b_ref[...]
    o_ref[0] = _silu(y)


def conv_silu(x, buf, w, b):
    bsz, seq, ch = x.shape
    tt = _pick(seq, (512, 256, 128, 64, 32, 16, 8))
    tc = ch if tt <= 64 else _pick(ch, (1024, 640, 512, 256, 128))
    bufp = jnp.concatenate([jnp.zeros((bsz, SUB - (CONV_K - 1), ch), F32), buf], axis=1)
    wp = jnp.concatenate([w, jnp.zeros((SUB - CONV_K, ch), F32)], axis=0)
    nsub = tt // SUB
    return pl.pallas_call(
        functools.partial(_conv_body, tt=tt),
        grid=(bsz, seq // tt, ch // tc),
        in_specs=[
            pl.BlockSpec((1, tt, tc), lambda bi, ti, ci: (bi, ti, ci)),
            pl.BlockSpec((1, SUB, tc), lambda bi, ti, ci: (bi, jnp.maximum(ti * nsub - 1, 0), ci)),
            pl.BlockSpec((1, SUB, tc), lambda bi, ti, ci: (bi, 0, ci)),
            pl.BlockSpec((SUB, tc), lambda bi, ti, ci: (0, ci)),
            pl.BlockSpec((1, tc), lambda bi, ti, ci: (0, ci)),
        ],
        out_specs=pl.BlockSpec((1, tt, tc), lambda bi, ti, ci: (bi, ti, ci)),
        out_shape=jax.ShapeDtypeStruct((bsz, seq, ch), F32),
        scratch_shapes=[pltpu.VMEM((SUB + tt, tc), F32)],
        compiler_params=_cparams("parallel", "parallel", "parallel"),
        name="conv_silu",
    )(x, x, bufp, wp, b.reshape(1, ch))


HG = SSD_HEADS // SSD_GROUPS
GW = HG * SSD_HEAD_DIM


def _expand_heads(v, e_bf16):
    hi = v.astype(BF16)
    r1 = v - hi.astype(F32)
    mid = r1.astype(BF16)
    lo = (r1 - mid.astype(F32)).astype(BF16)
    return _dot(hi, e_bf16) + _dot(mid, e_bf16) + _dot(lo, e_bf16)


def _ssd_body(xbc_ref, z_ref, small_ref, dtt_ref, h0_ref, prow_ref, pcol_ref, dexp_ref, norm_ref,
              y_ref, hout_ref, h_sc, y_sc, *, q):
    @pl.when(pl.program_id(1) == 0)
    def _():
        h_sc[...] = h0_ref[0]

    xbc = xbc_ref[0]
    xs = xbc[:, :SSD_INNER]
    bm = xbc[:, SSD_INNER:SSD_INNER + LANES]
    cm = xbc[:, SSD_INNER + LANES:SSD_INNER + 2 * LANES]
    dt_c = _softplus(small_ref[0][:, 0:SSD_HEADS] + prow_ref[0:1, 0:SSD_HEADS])
    a_c = dt_c * (-jnp.exp(prow_ref[1:2, 0:SSD_HEADS]))
    dt_t = _softplus(dtt_ref[0] + pcol_ref[:, 0:1])
    a_t = dt_t * (-jnp.exp(pcol_ref[:, 1:2]))
    ri = _iota((q, q), 0)
    ci = _iota((q, q), 1)
    causal = ci <= ri
    cum = _dot_hi(causal.astype(F32), a_c)
    cum_t = _dot_hi(a_t, (ri <= ci).astype(F32))
    e_heads = (_iota((SSD_HEADS, SSD_INNER), 1) // SSD_HEAD_DIM == _iota((SSD_HEADS, SSD_INNER), 0)).astype(BF16)
    dt_x = _expand_heads(dt_c, e_heads)
    ecum_x = _expand_heads(jnp.exp(cum), e_heads)
    wlast_x = _expand_heads(jnp.exp(cum[q - 1:q, :] - cum), e_heads)
    xdt = xs * dt_x
    xw = (xdt * wlast_x).astype(BF16)
    xdt_b = xdt.astype(BF16)

    lane = _iota((1, LANES), 1)
    low = lane < SSD_STATE
    bm_b = bm.astype(BF16)
    hs = h_sc[...]
    hs_b = hs.astype(BF16)
    y_inter = []
    cbs = []
    upd = []
    for g in range(SSD_GROUPS):
        cm_g = jnp.where(low if g == 0 else jnp.logical_not(low), cm, 0.0).astype(BF16)
        cbs.append(_dot_nt(cm_g, bm_b))
        y_inter.append(_dot(cm_g, hs_b))
        upd.append(_dot_tn(bm_b, xw[:, g * GW:(g + 1) * GW]))
    for j in range(SSD_HEADS // 2):
        g = (2 * j) // HG
        xp = xdt_b[:, j * LANES:(j + 1) * LANES]
        ys = []
        for hh in (2 * j, 2 * j + 1):
            seg = cum[:, hh:hh + 1] - cum_t[hh:hh + 1, :]
            lm = jnp.exp(jnp.where(causal, seg, -jnp.inf))
            ys.append(_dot((cbs[g] * lm).astype(BF16), xp))
        y_sc[:, j * LANES:(j + 1) * LANES] = jnp.where(low, ys[0], ys[1])
    y = y_sc[...] + jnp.concatenate(y_inter, axis=1) * ecum_x + dexp_ref[...] * xs
    y = y * _silu(z_ref[0])
    y = jnp.concatenate([_rms_rows(y[:, g * GW:(g + 1) * GW]) for g in range(SSD_GROUPS)], axis=1)
    y_ref[0] = y * norm_ref[...]

    row_low = _iota((2 * SSD_STATE, 1), 0) < SSD_STATE
    ecl = ecum_x[q - 1:q, :]
    decay = jnp.where(row_low, ecl[:, 0:GW], ecl[:, GW:2 * GW])
    h_new = hs * decay + jnp.where(row_low, upd[0], upd[1])
    h_sc[...] = h_new

    @pl.when(pl.program_id(1) == pl.num_programs(1) - 1)
    def _():
        hout_ref[0] = h_new


def ssd_scan(xbc_c, z, small, h0, dt_bias, a_log, d_skip, norm_w):
    bsz, seq, _ = xbc_c.shape
    q = _pick(seq, (128, 64, 32, 16, 8))
    dtt = jnp.swapaxes(small[:, :, 0:SSD_HEADS], 1, 2)
    hs0 = h0.reshape(bsz, SSD_GROUPS, HG, SSD_HEAD_DIM, SSD_STATE).transpose(0, 1, 4, 2, 3)
    hs0 = hs0.reshape(bsz, SSD_GROUPS * SSD_STATE, GW)
    prow = jnp.zeros((SUB, LANES), F32).at[0, :SSD_HEADS].set(dt_bias).at[1, :SSD_HEADS].set(a_log)
    pcol = jnp.zeros((SSD_HEADS, LANES), F32).at[:, 0].set(dt_bias).at[:, 1].set(a_log)
    dexp = jnp.repeat(d_skip, SSD_HEAD_DIM).reshape(1, SSD_INNER)
    y, hs = pl.pallas_call(
        functools.partial(_ssd_body, q=q),
        grid=(bsz, seq // q),
        in_specs=[
            pl.BlockSpec((1, q, SSD_CONV_DIM), lambda b, c: (b, c, 0)),
            pl.BlockSpec((1, q, SSD_INNER), lambda b, c: (b, c, 0)),
            pl.BlockSpec((1, q, LANES), lambda b, c: (b, c, 0)),
            pl.BlockSpec((1, SSD_HEADS, q), lambda b, c: (b, 0, c)),
            pl.BlockSpec((1, 2 * SSD_STATE, GW), lambda b, c: (b, 0, 0)),
            pl.BlockSpec((SUB, LANES), lambda b, c: (0, 0)),
            pl.BlockSpec((SSD_HEADS, LANES), lambda b, c: (0, 0)),
            pl.BlockSpec((1, SSD_INNER), lambda b, c: (0, 0)),
            pl.BlockSpec((1, SSD_INNER), lambda b, c: (0, 0)),
        ],
        out_specs=[
            pl.BlockSpec((1, q, SSD_INNER), lambda b, c: (b, c, 0)),
            pl.BlockSpec((1, 2 * SSD_STATE, GW), lambda b, c: (b, 0, 0)),
        ],
        out_shape=[
            jax.ShapeDtypeStruct((bsz, seq, SSD_INNER), F32),
            jax.ShapeDtypeStruct((bsz, 2 * SSD_STATE, GW), F32),
        ],
        scratch_shapes=[pltpu.VMEM((2 * SSD_STATE, GW), F32), pltpu.VMEM((q, SSD_INNER), F32)],
        compiler_params=_cparams("parallel", "arbitrary"),
        name="ssd_scan",
    )(xbc_c, z, small, dtt, hs0, prow, pcol, dexp, norm_w.reshape(1, SSD_INNER))
    h_new = hs.reshape(bsz, SSD_GROUPS, SSD_STATE, HG, SSD_HEAD_DIM).transpose(0, 1, 3, 4, 2)
    return y, h_new.reshape(bsz, SSD_HEADS, SSD_HEAD_DIM, SSD_STATE)


def _split2(x):
    hi = x.astype(BF16)
    return hi, (x - hi.astype(F32)).astype(BF16)


def _dot3(a, b):
    ah, al = _split2(a)
    bh, bl = _split2(b)
    return _dot(ah, bh) + (_dot(ah, bl) + _dot(al, bh))


def _gdn_prep_body(q_ref, k_ref, v_ref, small_ref, gbt_ref, prow_ref,
                   u_ref, w_ref, qg_ref, kd_ref, qk_ref, eg_ref, *, c, nch, hp):
    lane = _iota((1, LANES), 1)

    def pick_lane(x, idx):
        return jnp.sum(jnp.where(lane == idx, x, 0.0), axis=1, keepdims=True)

    small = small_ref[0]
    beta_col, g_col, g_row = [], [], []
    for i in range(hp):
        h = pl.program_id(1) * hp + i
        neg_a = -jnp.exp(pick_lane(prow_ref[0:1, :], h))
        dtb = pick_lane(prow_ref[1:2, :], h)
        beta_col.append(_sigmoid(pick_lane(small, SSD_HEADS + h)))
        g_col.append(neg_a * _softplus(pick_lane(small, SSD_HEADS + GDN_HEADS + h) + dtb))
        g_row.append(neg_a * _softplus(gbt_ref[0, pl.ds(GDN_HEADS + h, 1), :] + dtb))

    ri = _iota((c, c), 0)
    ci = _iota((c, c), 1)
    tril = ri >= ci
    tril_f = tril.astype(F32)
    triu_f = (ri <= ci).astype(F32)
    eye = (ri == ci).astype(F32)
    units = [(i, ch) for i in range(hp) for ch in range(nch)]
    rows = [slice(ch * c, (ch + 1) * c) for _, ch in units]
    lanes = [slice(i * GDN_DK, (i + 1) * GDN_DK) for i, _ in units]
    nu = range(len(units))
    qn, kn, kb, gam, dec, nmat = [], [], [], [], [], []
    for n in nu:
        i = units[n][0]
        qc = q_ref[0, rows[n], lanes[n]]
        kc = k_ref[0, rows[n], lanes[n]]
        qn.append(qc * lax.rsqrt(jnp.sum(qc * qc, axis=-1, keepdims=True) + 1e-6) * (GDN_DK ** -0.5))
        kn.append(kc * lax.rsqrt(jnp.sum(kc * kc, axis=-1, keepdims=True) + 1e-6))
        kb.append(kn[n] * beta_col[i][rows[n], :])
        gam.append(_dot_hi(tril_f, jnp.broadcast_to(g_col[i][rows[n], :], (c, LANES))))
    for n in nu:
        gam_r = _dot_hi(jnp.broadcast_to(g_row[units[n][0]][:, rows[n]], (SUB, c)), triu_f)[0:1, :]
        dec.append(jnp.exp(jnp.where(tril, gam[n][:, 0:c] - gam_r, -jnp.inf)))
        nmat.append(jnp.where(ri > ci, _dot_nt(kb[n].astype(BF16), kn[n].astype(BF16)) * dec[n], 0.0) * -1.0)
    tinv = [eye + m for m in nmat]
    npow = [_dot3(m, m) for m in nmat]
    levels = int(math.log2(c))
    for lvl in range(1, levels):
        for n in nu:
            if lvl == levels - 1:
                tinv[n] = tinv[n] + _dot3(npow[n], tinv[n])
            else:
                both = _dot3(npow[n], jnp.concatenate([tinv[n], npow[n]], axis=1))
                tinv[n] = tinv[n] + both[:, :c]
                npow[n] = both[:, c:]
    for n in nu:
        i, ch = units[n]
        egam = jnp.exp(gam[n])
        vb = (v_ref[0, rows[n], lanes[n]] * beta_col[i][rows[n], :]).astype(BF16)
        uw = _dot(tinv[n].astype(BF16), jnp.concatenate([vb, (kb[n] * egam).astype(BF16)], axis=1))
        glast = gam[n][c - 1:c, :]
        u_ref[0, rows[n], lanes[n]] = uw[:, :GDN_DV]
        w_ref[0, rows[n], lanes[n]] = uw[:, GDN_DV:].astype(w_ref.dtype)
        qg_ref[0, rows[n], lanes[n]] = (qn[n] * egam).astype(qg_ref.dtype)
        kd_ref[0, rows[n], lanes[n]] = (kn[n] * jnp.exp(glast - gam[n])).astype(kd_ref.dtype)
        qk_ref[0, i, rows[n], :] = (_dot_nt(qn[n].astype(BF16), kn[n].astype(BF16)) * dec[n]).astype(qk_ref.dtype)
        eg_ref[0, i, ch:ch + 1, :] = jnp.exp(glast)


GDN_GROUP = 4


def _gdn_scan_body(u_ref, w_ref, qg_ref, kd_ref, qk_ref, eg_ref, gate_ref, s0_ref, norm_ref,
                   o_ref, sout_ref, s_sc, *, c, nch):
    @pl.when(pl.program_id(2) == 0)
    def _():
        s_sc[...] = s0_ref[0]

    s = [s_sc[i] for i in range(GDN_GROUP)]
    for ch in range(nch):
        rows = slice(ch * c, (ch + 1) * c)
        for i in range(GDN_GROUP):
            lanes = slice(i * GDN_DV, (i + 1) * GDN_DV)
            s_b = s[i].astype(BF16)
            wq = jnp.concatenate([w_ref[0, rows, lanes].astype(BF16), qg_ref[0, rows, lanes].astype(BF16)], axis=0)
            ws = _dot(wq, s_b)
            vn_b = (u_ref[0, rows, lanes] - ws[:c]).astype(BF16)
            o = ws[c:] + _dot(qk_ref[0, i, rows, :].astype(BF16), vn_b)
            s[i] = s[i] * eg_ref[0, i, ch:ch + 1, :] + _dot_tn(kd_ref[0, rows, lanes].astype(BF16), vn_b)
            o_ref[0, rows, lanes] = _rms_rows(o) * norm_ref[...] * _silu(gate_ref[0, rows, lanes])
    for i in range(GDN_GROUP):
        s_sc[i] = s[i]

    @pl.when(pl.program_id(2) == pl.num_programs(2) - 1)
    def _():
        for i in range(GDN_GROUP):
            sout_ref[0, i] = s[i]


def gdn_scan(qkv_c, gate, small, s0, dt_bias, a_log, norm_w):
    bsz, seq, _ = qkv_c.shape
    c = _pick(seq, (64, 32, 16, 8))
    tb = _pick(seq, (512, 256, 128, 64, 32, 16, 8))
    nch = tb // c
    wdt = BF16 if c % 16 == 0 else F32
    gbt = jnp.swapaxes(small[:, :, SSD_HEADS:SSD_HEADS + 2 * GDN_HEADS], 1, 2)
    prow = jnp.zeros((SUB, LANES), F32).at[0, :GDN_HEADS].set(a_log).at[1, :GDN_HEADS].set(dt_bias)
    hp = 1 if nch >= 4 else GDN_HEADS
    ng = GDN_HEADS // hp
    head_blk = pl.BlockSpec((1, tb, hp * GDN_DV), lambda b, h, t: (b, t, h))
    u, w, qg, kd, qk, eg = pl.pallas_call(
        functools.partial(_gdn_prep_body, c=c, nch=nch, hp=hp),
        grid=(bsz, ng, seq // tb),
        in_specs=[
            pl.BlockSpec((1, tb, hp * GDN_DK), lambda b, h, t: (b, t, h)),
            pl.BlockSpec((1, tb, hp * GDN_DK), lambda b, h, t: (b, t, ng + h)),
            pl.BlockSpec((1, tb, hp * GDN_DV), lambda b, h, t: (b, t, 2 * ng + h)),
            pl.BlockSpec((1, tb, LANES), lambda b, h, t: (b, t, 0)),
            pl.BlockSpec((1, 2 * GDN_HEADS, tb), lambda b, h, t: (b, 0, t)),
            pl.BlockSpec((SUB, LANES), lambda b, h, t: (0, 0)),
        ],
        out_specs=[head_blk, head_blk, head_blk, head_blk,
                   pl.BlockSpec((1, hp, tb, c), lambda b, h, t: (b, h, t, 0)),
                   pl.BlockSpec((1, hp, nch, LANES), lambda b, h, t: (b, h, t, 0))],
        out_shape=[
            jax.ShapeDtypeStruct((bsz, seq, GDN_VW), F32),
            jax.ShapeDtypeStruct((bsz, seq, GDN_VW), wdt),
            jax.ShapeDtypeStruct((bsz, seq, GDN_QK), wdt),
            jax.ShapeDtypeStruct((bsz, seq, GDN_QK), wdt),
            jax.ShapeDtypeStruct((bsz, GDN_HEADS, seq, c), wdt),
            jax.ShapeDtypeStruct((bsz, GDN_HEADS, seq // c, LANES), F32),
        ],
        compiler_params=_cparams("parallel", "parallel", "parallel"),
        name="gdn_prep",
    )(qkv_c, qkv_c, qkv_c, small, gbt, prow)
    gw = GDN_GROUP * GDN_DV
    grp_blk = pl.BlockSpec((1, tb, gw), lambda b, g, t: (b, t, g))
    state_blk = pl.BlockSpec((1, GDN_GROUP, GDN_DK, GDN_DV), lambda b, g, t: (b, g, 0, 0))
    o, s_new = pl.pallas_call(
        functools.partial(_gdn_scan_body, c=c, nch=nch),
        grid=(bsz, GDN_HEADS // GDN_GROUP, seq // tb),
        in_specs=[
            grp_blk, grp_blk, grp_blk, grp_blk,
            pl.BlockSpec((1, GDN_GROUP, tb, c), lambda b, g, t: (b, g, t, 0)),
            pl.BlockSpec((1, GDN_GROUP, nch, LANES), lambda b, g, t: (b, g, t, 0)),
            grp_blk,
            state_blk,
            pl.BlockSpec((1, GDN_DV), lambda b, g, t: (0, 0)),
        ],
        out_specs=[grp_blk, state_blk],
        out_shape=[
            jax.ShapeDtypeStruct((bsz, seq, GDN_VW), F32),
            jax.ShapeDtypeStruct((bsz, GDN_HEADS, GDN_DK, GDN_DV), F32),
        ],
        scratch_shapes=[pltpu.VMEM((GDN_GROUP, GDN_DK, GDN_DV), F32)],
        compiler_params=_cparams("parallel", "parallel", "arbitrary"),
        name="gdn_scan",
    )(u, w, qg, kd, qk, eg, gate, s0, norm_w.reshape(1, GDN_DV))
    return o, s_new


def _xattn_body(h_ref, g_ref, wq_ref, mk_ref, mv_ref, wo_ref, o_ref):
    x = h_ref[0]
    xn = (_rms_rows(x) * g_ref[...]).astype(BF16)
    qf = _dot(xn, wq_ref[...])
    mk = mk_ref[0].astype(BF16)
    mv = mv_ref[0].astype(BF16)
    outs = []
    for hd in range(X_HEADS):
        sl = slice(hd * X_HEAD_DIM, (hd + 1) * X_HEAD_DIM)
        s = _dot_nt(qf[:, sl].astype(BF16), mk[:, sl]) * (X_HEAD_DIM ** -0.5)
        p = jnp.exp(s - jnp.max(s, axis=-1, keepdims=True))
        p = p / jnp.sum(p, axis=-1, keepdims=True)
        outs.append(_dot(p.astype(BF16), mv[:, sl]))
    o = jnp.concatenate(outs, axis=1).astype(BF16)
    o_ref[0] = x + _dot(o, wo_ref[...])


def cross_attn(h, gain, wq, mk, mv, wo):
    bsz, seq, d = h.shape
    tm = _pick(seq, (512, 256, 128, 64, 32, 16, 8))
    return pl.pallas_call(
        _xattn_body,
        grid=(bsz, seq // tm),
        in_specs=[
            pl.BlockSpec((1, tm, d), lambda b, i: (b, i, 0)),
            pl.BlockSpec((1, d), lambda b, i: (0, 0)),
            pl.BlockSpec((d, X_W), lambda b, i: (0, 0)),
            pl.BlockSpec((1, MEM_LEN, X_W), lambda b, i: (b, 0, 0)),
            pl.BlockSpec((1, MEM_LEN, X_W), lambda b, i: (b, 0, 0)),
            pl.BlockSpec((X_W, d), lambda b, i: (0, 0)),
        ],
        out_specs=pl.BlockSpec((1, tm, d), lambda b, i: (b, i, 0)),
        out_shape=jax.ShapeDtypeStruct((bsz, seq, d), F32),
        compiler_params=_cparams("parallel", "parallel"),
        name="cross_attn",
    )(h, gain.reshape(1, d), wq, mk, mv, wo)


def _swiglu_body(h_ref, g_ref, w1_ref, w3_ref, w2_ref, o_ref, xn_sc):
    j = pl.program_id(1)

    @pl.when(j == 0)
    def _():
        x = h_ref[...]
        xn_sc[...] = (_rms_rows(x) * g_ref[...]).astype(BF16)
        o_ref[...] = x

    xn = xn_sc[...]
    a = _dot(xn, w1_ref[...])
    b = _dot(xn, w3_ref[...])
    o_ref[...] += _dot((_silu(a) * b).astype(BF16), w2_ref[...])


def swiglu(h, gain, w1, w3, w2):
    m, d = h.shape
    ff = w1.shape[1]
    tm = _pick(m, (512, 256, 128, 64, 32, 16, 8))
    tf = _pick(ff, (1408, 1024, 512, 256, 128))
    return pl.pallas_call(
        _swiglu_body,
        grid=(m // tm, ff // tf),
        in_specs=[
            pl.BlockSpec((tm, d), lambda i, j: (i, 0)),
            pl.BlockSpec((1, d), lambda i, j: (0, 0)),
            pl.BlockSpec((d, tf), lambda i, j: (0, j)),
            pl.BlockSpec((d, tf), lambda i, j: (0, j)),
            pl.BlockSpec((tf, d), lambda i, j: (j, 0)),
        ],
        out_specs=pl.BlockSpec((tm, d), lambda i, j: (i, 0)),
        out_shape=jax.ShapeDtypeStruct((m, d), F32),
        scratch_shapes=[pltpu.VMEM((tm, d), BF16)],
        compiler_params=_cparams("parallel", "arbitrary"),
        name="swiglu",
    )(h, gain.reshape(1, d), w1, w3, w2)


def _cumsum_lanes_body(x_ref, o_ref, carry_sc, *, tc):
    @pl.when(pl.program_id(1) == 0)
    def _():
        carry_sc[...] = jnp.zeros_like(carry_sc)

    upper = (_iota((tc, tc), 0) <= _iota((tc, tc), 1)).astype(F32)
    f = _dot_hi(x_ref[0], upper) + carry_sc[:, 0:1]
    o_ref[0] = f
    carry_sc[...] = jnp.broadcast_to(f[:, tc - 1:tc], carry_sc.shape)


def cumsum_lanes(x):
    bsz, r, seq = x.shape
    tc = _pick(seq, (512, 256, 128))
    return pl.pallas_call(
        functools.partial(_cumsum_lanes_body, tc=tc),
        grid=(bsz, seq // tc),
        in_specs=[pl.BlockSpec((1, r, tc), lambda b, i: (b, 0, i))],
        out_specs=pl.BlockSpec((1, r, tc), lambda b, i: (b, 0, i)),
        out_shape=jax.ShapeDtypeStruct((bsz, r, seq), F32),
        scratch_shapes=[pltpu.VMEM((r, LANES), F32)],
        compiler_params=_cparams("parallel", "arbitrary"),
        name="cumsum_lanes",
    )(x)


FOX_ROWS = 512
FOX_PAIRS = 2


def _fox_body(qi_ref, ki_ref, q_ref, k_ref, v_ref, f_ref, o_ref, m_sc, l_sc, acc_sc, *, t):
    step = pl.program_id(2)
    qi = qi_ref[step]
    ki = ki_ref[step]
    low = _iota((1, LANES), 1) < FOX_HEAD_DIM
    rb = min(FOX_ROWS, t)

    @pl.when(ki == 0)
    def _():
        m_sc[...] = jnp.full(m_sc.shape, -jnp.inf, F32)
        l_sc[...] = jnp.zeros_like(l_sc)
        acc_sc[...] = jnp.zeros_like(acc_sc)

    def update(masked):
        units = [(pr, r, hh) for pr in range(FOX_PAIRS) for r in range(t // rb) for hh in range(2)]
        kbs, vbs = {}, {}

        def nkeys(r):
            return (r + 1) * rb if masked else t

        def qk(u):
            pr, r, hh = u
            lanes = slice(pr * LANES, (pr + 1) * LANES)
            if pr not in kbs:
                kbs[pr] = k_ref[0, :, lanes].astype(BF16)
                vbs[pr] = v_ref[0, :, lanes].astype(BF16)
            qs = q_ref[0, r * rb:(r + 1) * rb, lanes] * (FOX_HEAD_DIM ** -0.5)
            qm = jnp.where(low if hh == 0 else jnp.logical_not(low), qs, 0.0).astype(BF16)
            return _dot_nt(qm, kbs[pr][:nkeys(r)])

        s_next = qk(units[0])
        for i, (pr, r, hh) in enumerate(units):
            rows = slice(r * rb, (r + 1) * rb)
            nk = nkeys(r)
            s = s_next - f_ref[0, pr, hh:hh + 1, :nk]
            if i + 1 < len(units):
                s_next = qk(units[i + 1])
            if masked:
                s = jnp.where(_iota((rb, nk), 1) <= _iota((rb, nk), 0) + r * rb, s, -jnp.inf)
            m_prev = m_sc[pr, hh, rows, :]
            m_new = jnp.maximum(m_prev, jnp.max(s, axis=-1, keepdims=True))
            alpha = jnp.exp(m_prev - m_new)
            p = jnp.exp(s - jnp.concatenate([m_new] * (nk // LANES), axis=1))
            l_sc[pr, hh, rows, :] = alpha * l_sc[pr, hh, rows, :] + jnp.sum(p, axis=-1, keepdims=True)
            m_sc[pr, hh, rows, :] = m_new
            acc_sc[pr, hh, rows, :] = alpha * acc_sc[pr, hh, rows, :] + _dot(p.astype(BF16), vbs[pr][:nk])

    @pl.when(ki < qi)
    def _():
        update(False)

    @pl.when(ki == qi)
    def _():
        update(True)
        for pr in range(FOX_PAIRS):
            o_ref[0, :, pr * LANES:(pr + 1) * LANES] = jnp.where(
                low, acc_sc[pr, 0] / l_sc[pr, 0], acc_sc[pr, 1] / l_sc[pr, 1])


def fox_prompt_attn(q, k, v, ft):
    bsz, seq, _ = q.shape
    t = _pick(seq, (1024, 512, 256, 128))
    nb = seq // t
    pairs =[(i, j) for i in range(nb) for j in range(i + 1)]
    qi = jnp.asarray([p[0] for p in pairs], jnp.int32)
    ki = jnp.asarray([p[1] for p in pairs], jnp.int32)
    npair = FOX_HEADS // 2
    w = FOX_PAIRS * LANES
    ft4 = ft.reshape(bsz, npair, 2, seq)
    grid_spec = pltpu.PrefetchScalarGridSpec(
        num_scalar_prefetch=2,
        grid=(bsz, npair // FOX_PAIRS, len(pairs)),
        in_specs=[
            pl.BlockSpec((1, t, w), lambda b, j, s, qi, ki: (b, qi[s], j)),
            pl.BlockSpec((1, t, w), lambda b, j, s, qi, ki: (b, ki[s], j)),
            pl.BlockSpec((1, t, w), lambda b, j, s, qi, ki: (b, ki[s], j)),
            pl.BlockSpec((1, FOX_PAIRS, 2, t), lambda b, j, s, qi, ki: (b, j, 0, ki[s])),
        ],
        out_specs=pl.BlockSpec((1, t, w), lambda b, j, s, qi, ki: (b, qi[s], j)),
        scratch_shapes=[pltpu.VMEM((FOX_PAIRS, 2, t, LANES), F32)] * 3,
    )
    return pl.pallas_call(
        functools.partial(_fox_body, t=t),
        grid_spec=grid_spec,
        out_shape=jax.ShapeDtypeStruct((bsz, seq, FOX_W), F32),
        compiler_params=_cparams("parallel", "parallel", "arbitrary"),
        name="fox_prompt_attn",
    )(qi, ki, q, k, v, ft4)


def _page_suffix_body(x_ref, rin_ref, tot_ref):
    x = x_ref[...]
    after = (_iota((PAGE_SIZE, PAGE_SIZE), 0) > _iota((PAGE_SIZE, PAGE_SIZE), 1)).astype(F32)
    rin_ref[...] = _dot_hi(x, after)
    tot_ref[...] = jnp.broadcast_to(jnp.sum(x, axis=1, keepdims=True), x.shape)


def page_suffix(lft):
    r = lft.shape[0]
    tr = _pick(r, (2048, 1024, 512, 256, 128, 64, 32, 16))
    spec = pl.BlockSpec((tr, PAGE_SIZE), lambda i: (i, 0))
    return pl.pallas_call(
        _page_suffix_body,
        grid=(r // tr,),
        in_specs=[spec],
        out_specs=[spec, spec],
        out_shape=[jax.ShapeDtypeStruct(lft.shape, F32)] * 2,
        compiler_params=_cparams("parallel"),
        name="page_suffix",
    )(lft)


DEC_PAGES = 4


def _rep_rows(x, n):
    r, c = x.shape
    return jnp.broadcast_to(x[:, None, :], (r, n, c)).reshape(r * n, c)


def _fox_decode_body(pt_ref, q_ref, *refs, nt, nsteps):
    k_refs = refs[0:DEC_PAGES]
    v_refs = refs[DEC_PAGES:2 * DEC_PAGES]
    rin_refs = refs[2 * DEC_PAGES:3 * DEC_PAGES]
    tot_refs = refs[3 * DEC_PAGES:4 * DEC_PAGES]
    kn_ref, vn_ref, lfn_ref, o_ref, qbd_sc, m_sc, l_sc, acc_sc, run_sc = refs[4 * DEC_PAGES:]
    j = pl.program_id(1)
    rows = FOX_HEADS * nt

    @pl.when(j == 0)
    def _():
        qt = jnp.concatenate([q_ref[0] * (FOX_HEAD_DIM ** -0.5)] * FOX_HEADS, axis=0)
        own = _iota((rows, FOX_W), 1) // FOX_HEAD_DIM == _iota((rows, FOX_W), 0) // nt
        qbd_sc[...] = jnp.where(own, qt, 0.0).astype(BF16)
        m_sc[...] = jnp.full(m_sc.shape, -jnp.inf, F32)
        l_sc[...] = jnp.zeros_like(l_sc)
        acc_sc[...] = jnp.zeros_like(acc_sc)
        run_sc[...] = jnp.zeros_like(run_sc)

    def update(ss, vals, transposed):
        m_prev = m_sc[...]
        m_new = m_prev
        for s in ss:
            m_new = jnp.maximum(m_new, jnp.max(s, axis=-1, keepdims=True))
        alpha = jnp.exp(m_prev - m_new)
        l_new = alpha * l_sc[...]
        acc = jnp.concatenate([alpha] * (FOX_W // LANES), axis=1) * acc_sc[...]
        for s, val in zip(ss, vals):
            p = jnp.exp(s - m_new)
            l_new = l_new + jnp.sum(p, axis=-1, keepdims=True)
            acc = acc + (_dot_nt(p.astype(BF16), val) if transposed else _dot(p.astype(BF16), val))
        m_sc[...] = m_new
        l_sc[...] = l_new
        acc_sc[...] = acc

    @pl.when(j < nsteps)
    def _():
        qbd = qbd_sc[...]
        raw = [_dot(qbd, k_refs[i][0, 0].reshape(FOX_W, PAGE_SIZE).astype(BF16)) for i in range(DEC_PAGES)]
        run = run_sc[...]
        ss = []
        for i in range(DEC_PAGES):
            ss.append(raw[i] + _rep_rows(rin_refs[i][0] + run, nt))
            run = run + tot_refs[i][0]
        run_sc[...] = run
        update(ss, [v_refs[i][0, 0].reshape(FOX_W, PAGE_SIZE).astype(BF16) for i in range(DEC_PAGES)], True)

    @pl.when(j == nsteps)
    def _():
        pad = jnp.zeros((PAGE_SIZE - nt, FOX_W), F32)
        kb = jnp.concatenate([kn_ref[0], pad], axis=0).astype(BF16)
        vb = jnp.concatenate([vn_ref[0], pad], axis=0).astype(BF16)
        s = _dot_nt(qbd_sc[...], kb)
        incl = (_iota((PAGE_SIZE, PAGE_SIZE), 0) <= _iota((PAGE_SIZE, PAGE_SIZE), 1)).astype(F32)
        fn = _dot_hi(lfn_ref[0], incl)
        tok = _iota((rows, PAGE_SIZE), 0) % nt
        update([jnp.where(_iota((rows, PAGE_SIZE), 1) <= tok, s - _rep_rows(fn, nt), -jnp.inf)], [vb], False)
        o = acc_sc[...] / jnp.concatenate([l_sc[...]] * (FOX_W // LANES), axis=1)
        lane_head = _iota((nt, FOX_W), 1) // FOX_HEAD_DIM
        out = jnp.zeros((nt, FOX_W), F32)
        for hd in range(FOX_HEADS):
            out = out + jnp.where(lane_head == hd, o[hd * nt:(hd + 1) * nt, :], 0.0)
        o_ref[0] = out


def fox_decode_attn(q, k_new, v_new, lfn_t, k_cache_t, v_cache_t, layer, rin, tot, page_table):
    bsz, nt, _ = q.shape
    npages = page_table.shape[1]
    assert npages % DEC_PAGES == 0
    nsteps = npages // DEC_PAGES
    rows = FOX_HEADS * nt

    def page(i):
        return lambda b, j, pt: pt[b, npages - 1 - (jnp.minimum(j, nsteps - 1) * DEC_PAGES + i)]

    cache_specs = [pl.BlockSpec((1, 1, FOX_HEADS, FOX_HEAD_DIM, PAGE_SIZE),
                                functools.partial(lambda b, j, pt, pg: (layer, pg(b, j, pt), 0, 0, 0), pg=page(i)))
                   for i in range(DEC_PAGES)]
    bias_specs = [pl.BlockSpec((1, FOX_HEADS, PAGE_SIZE), functools.partial(lambda b, j, pt, pg: (pg(b, j, pt), 0, 0), pg=page(i)))
                  for i in range(DEC_PAGES)]
    per_seq = lambda shape: pl.BlockSpec(shape, lambda b, j, pt: (b, 0, 0))
    grid_spec = pltpu.PrefetchScalarGridSpec(
        num_scalar_prefetch=1,
        grid=(bsz, nsteps + 1),
        in_specs=[per_seq((1, nt, FOX_W))] + cache_specs + cache_specs + bias_specs + bias_specs
        + [per_seq((1, nt, FOX_W)), per_seq((1, nt, FOX_W)), per_seq((1, FOX_HEADS, PAGE_SIZE))],
        out_specs=per_seq((1, nt, FOX_W)),
        scratch_shapes=[
            pltpu.VMEM((rows, FOX_W), BF16),
            pltpu.VMEM((rows, LANES), F32),
            pltpu.VMEM((rows, LANES), F32),
            pltpu.VMEM((rows, FOX_W), F32),
            pltpu.VMEM((FOX_HEADS, PAGE_SIZE), F32),
        ],
    )
    return pl.pallas_call(
        functools.partial(_fox_decode_body, nt=nt, nsteps=nsteps),
        grid_spec=grid_spec,
        out_shape=jax.ShapeDtypeStruct((bsz, nt, FOX_W), F32),
        compiler_params=_cparams("parallel", "arbitrary"),
        name="fox_decode_attn",
    )(page_table, q, *([k_cache_t] * DEC_PAGES), *([v_cache_t] * DEC_PAGES), *([rin] * DEC_PAGES),
      *([tot] * DEC_PAGES), k_new, v_new, lfn_t)


def _split_hyb_weights(w_in):
    o = 0
    parts = {}
    for name, width in (("z", SSD_INNER), ("xbc", SSD_CONV_DIM), ("dt", SSD_HEADS), ("qkv", GDN_CONV_DIM),
                        ("gate", GDN_VW), ("b", GDN_HEADS), ("a", GDN_HEADS)):
        parts[name] = w_in[:, o:o + width]
        o += width
    small = jnp.concatenate([parts["dt"], parts["b"], parts["a"]], axis=1)
    small = jnp.pad(small, ((0, 0), (0, LANES - small.shape[1])))
    return {k: parts[k].astype(BF16) for k in ("z", "xbc", "qkv", "gate")} | {"small": small.astype(BF16)}


def _hybrid_layer(h, e, W, st):
    bsz, seq, d = h.shape
    m = bsz * seq
    h2 = h.reshape(m, d)
    wp = _split_hyb_weights(W["w_in_hyb"][e])
    gain = W["norm_mix"][2 * e]
    proj = {k: fused_linear([h2], [wp[k]], gain=gain, name="hyb_in_" + k) for k in ("z", "xbc", "qkv", "gate", "small")}
    xbc = proj["xbc"].reshape(bsz, seq, SSD_CONV_DIM)
    qkv = proj["qkv"].reshape(bsz, seq, GDN_CONV_DIM)
    small = proj["small"].reshape(bsz, seq, LANES)
    xbc_c = conv_silu(xbc, st["ssd_conv"], W["ssd_conv_w"][e], W["ssd_conv_b"][e])
    qkv_c = conv_silu(qkv, st["gdn_conv"], W["gdn_conv_w"][e], jnp.zeros((GDN_CONV_DIM,), F32))
    y, ssd_h = ssd_scan(xbc_c, proj["z"].reshape(bsz, seq, SSD_INNER), small, st["ssd"],
                        W["ssd_dt_bias"][e], W["ssd_A_log"][e], W["ssd_D"][e], W["ssd_norm"][e])
    o, gdn_s = gdn_scan(qkv_c, proj["gate"].reshape(bsz, seq, GDN_VW), small, st["gdn"],
                        W["gdn_dt_bias"][e], W["gdn_A_log"][e], W["gdn_norm"][e])
    w_out = W["w_out_hyb"][e].astype(BF16)
    h_new = fused_linear([y.reshape(m, SSD_INNER), o.reshape(m, GDN_VW)], [w_out[:SSD_INNER], w_out[SSD_INNER:]],
                         residual=h2, name="hyb_out")
    new = dict(ssd=ssd_h, ssd_conv=xbc[:, seq - (CONV_K - 1):], gdn=gdn_s, gdn_conv=qkv[:, seq - (CONV_K - 1):])
    return h_new.reshape(bsz, seq, d), new


def _fox_layer(h, o_idx, layer, W, st, prompt):
    bsz, seq, d = h.shape
    m = bsz * seq
    h2 = h.reshape(m, d)
    w_in = W["w_in_fox"][o_idx]
    gain = W["norm_mix"][layer]
    wq, wk, wv = (w_in[:, i * FOX_W:(i + 1) * FOX_W].astype(BF16) for i in range(3))
    wf = jnp.pad(w_in[:, 3 * FOX_W:], ((0, 0), (0, LANES - FOX_HEADS))).astype(BF16)
    bf = jnp.pad(W["b_fox_f"][o_idx], (0, LANES - FOX_HEADS))
    q = fused_linear([h2], [wq], gain=gain, name="fox_q").reshape(bsz, seq, FOX_W)
    k = fused_linear([h2], [wk], gain=gain, name="fox_k").reshape(bsz, seq, FOX_W)
    v = fused_linear([h2], [wv], gain=gain, name="fox_v").reshape(bsz, seq, FOX_W)
    lf = fused_linear([h2], [wf], gain=gain, bias=bf, act="log_sigmoid", name="fox_f")[:, :FOX_HEADS]
    lf = lf.reshape(bsz, seq, FOX_HEADS)
    lf_t = jnp.swapaxes(lf, 1, 2)
    if prompt:
        att = fox_prompt_attn(q, k, v, cumsum_lanes(lf_t))
    else:
        n_pool = st["cache_fox_k"].shape[1]
        lfc_t = jnp.swapaxes(st["cache_fox_lf"][o_idx].astype(F32), 1, 2).reshape(n_pool * FOX_HEADS, PAGE_SIZE)
        rin, tot = page_suffix(lfc_t)
        rin = rin.reshape(n_pool, FOX_HEADS, PAGE_SIZE)
        tot = tot.reshape(n_pool, FOX_HEADS, PAGE_SIZE)
        lfn_t = jnp.pad(lf_t, ((0, 0), (0, 0), (0, PAGE_SIZE - seq)))
        kc_t = jnp.transpose(st["cache_fox_k"], (0, 1, 3, 4, 2))
        vc_t = jnp.transpose(st["cache_fox_v"], (0, 1, 3, 4, 2))
        att = fox_decode_attn(q, k, v, lfn_t, kc_t, vc_t, o_idx, rin, tot, st["page_table"])
    h_new = fused_linear([att.reshape(m, FOX_W)], [W["w_out_fox"][o_idx].astype(BF16)], residual=h2, name="fox_out")
    shp = (bsz, seq, FOX_HEADS, FOX_HEAD_DIM)
    return h_new.reshape(bsz, seq, d), dict(fox_k=k.reshape(shp), fox_v=v.reshape(shp), fox_lf=lf)


def _run_group(x, W, st, prompt):
    bsz, seq, d = x.shape
    depth = W["norm_mix"].shape[0]
    names = ("ssd", "ssd_conv", "gdn", "gdn_conv", "fox_k", "fox_v", "fox_lf", "mem_k", "mem_v")
    out = {n: [] for n in names}
    h = x
    for layer in range(depth):
        if layer % 2 == 0:
            e = layer // 2
            if prompt:
                s0 = dict(ssd=jnp.zeros((bsz, SSD_HEADS, SSD_HEAD_DIM, SSD_STATE), F32),
                          ssd_conv=jnp.zeros((bsz, CONV_K - 1, SSD_CONV_DIM), F32),
                          gdn=jnp.zeros((bsz, GDN_HEADS, GDN_DK, GDN_DV), F32),
                          gdn_conv=jnp.zeros((bsz, CONV_K - 1, GDN_CONV_DIM), F32))
            else:
                s0 = dict(ssd=st["state_ssd"][e], ssd_conv=st["state_ssd_conv"][e],
                          gdn=st["state_gdn"][e], gdn_conv=st["state_gdn_conv"][e])
            h, new = _hybrid_layer(h, e, W, s0)
        else:
            h, new = _fox_layer(h, layer // 2, layer, W, st, prompt)
        for n, val in new.items():
            out[n].append(val)
        if prompt:
            mem = st["mem"]
            mem2 = mem.reshape(bsz * MEM_LEN, d)
            mk = fused_linear([mem2], [W["wk_x"][layer].astype(BF16)], gain=W["norm_mem"][layer], name="mem_k")
            mv = fused_linear([mem2], [W["wv_x"][layer].astype(BF16)], gain=W["norm_mem"][layer], name="mem_v")
            mk = mk.reshape(bsz, MEM_LEN, X_W)
            mv = mv.reshape(bsz, MEM_LEN, X_W)
            out["mem_k"].append(mk.reshape(bsz, MEM_LEN, X_HEADS, X_HEAD_DIM))
            out["mem_v"].append(mv.reshape(bsz, MEM_LEN, X_HEADS, X_HEAD_DIM))
        else:
            mk = st["cache_mem_k"][layer].reshape(bsz, MEM_LEN, X_W)
            mv = st["cache_mem_v"][layer].reshape(bsz, MEM_LEN, X_W)
        h = cross_attn(h, W["norm_x"][layer], W["wq_x"][layer].astype(BF16), mk, mv, W["wo_x"][layer].astype(BF16))
        h = swiglu(h.reshape(bsz * seq, d), W["norm_ffn"][layer], W["w1"][layer].astype(BF16),
                   W["w3"][layer].astype(BF16), W["w2"][layer].astype(BF16)).reshape(bsz, seq, d)
    y = rmsnorm_rows(h.reshape(bsz * seq, d), W["norm_final"]).reshape(bsz, seq, d)
    new = {n: jnp.stack(out[n]) for n in names if out[n]}
    return y, new


def kernel(x_prompt, x_sample, mem_prompt, state_ssd, state_ssd_conv, state_gdn, state_gdn_conv, cache_fox_k, cache_fox_v, cache_fox_lf, page_table, cache_mem_k, cache_mem_v, norm_mix, norm_x, norm_mem, norm_ffn, norm_final, w_in_hyb, w_out_hyb, ssd_conv_w, ssd_conv_b, ssd_dt_bias, ssd_A_log, ssd_D, ssd_norm, gdn_conv_w, gdn_dt_bias, gdn_A_log, gdn_norm, w_in_fox, b_fox_f, w_out_fox, wq_x, wk_x, wv_x, wo_x, w1, w3, w2):
    W = dict(norm_mix=norm_mix, norm_x=norm_x, norm_mem=norm_mem, norm_ffn=norm_ffn, norm_final=norm_final,
             w_in_hyb=w_in_hyb, w_out_hyb=w_out_hyb, ssd_conv_w=ssd_conv_w, ssd_conv_b=ssd_conv_b,
             ssd_dt_bias=ssd_dt_bias, ssd_A_log=ssd_A_log, ssd_D=ssd_D, ssd_norm=ssd_norm,
             gdn_conv_w=gdn_conv_w, gdn_dt_bias=gdn_dt_bias, gdn_A_log=gdn_A_log, gdn_norm=gdn_norm,
             w_in_fox=w_in_fox, b_fox_f=b_fox_f, w_out_fox=w_out_fox,
             wq_x=wq_x, wk_x=wk_x, wv_x=wv_x, wo_x=wo_x, w1=w1, w3=w3, w2=w2)
    y_prompt, pn = _run_group(x_prompt, W, dict(mem=mem_prompt), True)
    st = dict(state_ssd=state_ssd, state_ssd_conv=state_ssd_conv, state_gdn=state_gdn,
              state_gdn_conv=state_gdn_conv, cache_fox_k=cache_fox_k, cache_fox_v=cache_fox_v,
              cache_fox_lf=cache_fox_lf, page_table=page_table, cache_mem_k=cache_mem_k, cache_mem_v=cache_mem_v)
    y_sample, sn = _run_group(x_sample, W, st, False)
    return (y_prompt, y_sample,
            pn["ssd"], pn["ssd_conv"], pn["gdn"], pn["gdn_conv"],
            pn["fox_k"], pn["fox_v"], pn["fox_lf"], pn["mem_k"], pn["mem_v"],
            sn["ssd"], sn["ssd_conv"], sn["gdn"], sn["gdn_conv"],
            sn["fox_k"], sn["fox_v"], sn["fox_lf"])
```

```python
import functools
import math

import jax
import jax.numpy as jnp
from jax import lax
from jax.experimental import pallas as pl
from jax.experimental.pallas import tpu as pltpu

F32 = jnp.float32
BF16 = jnp.bfloat16
HI = lax.Precision.HIGHEST
NT_DIMS = (((1,), (1,)), ((), ()))
TN_DIMS = (((0,), (0,)), ((), ()))

D_MODEL = 1024
EPS = 1e-6
CONV_K = 4
SSD_HEADS = 16
SSD_HEAD_DIM = 64
SSD_INNER = SSD_HEADS * SSD_HEAD_DIM
SSD_GROUPS = 2
SSD_STATE = 64
SSD_CONV_DIM = SSD_INNER + 2 * SSD_GROUPS * SSD_STATE
GDN_HEADS = 8
GDN_DK = 128
GDN_DV = 128
GDN_QK = GDN_HEADS * GDN_DK
GDN_VW = GDN_HEADS * GDN_DV
GDN_CONV_DIM = 2 * GDN_QK + GDN_VW
FOX_HEADS = 16
FOX_HEAD_DIM = 64
FOX_W = FOX_HEADS * FOX_HEAD_DIM
PAGE_SIZE = 128
MEM_LEN = 256
X_HEADS = 4
X_HEAD_DIM = 128
X_W = X_HEADS * X_HEAD_DIM
LANES = 128
VMEM_LIMIT = 56 * 1024 * 1024


def _cparams(*sem):
    return pltpu.CompilerParams(dimension_semantics=sem, vmem_limit_bytes=VMEM_LIMIT)


def _pick(n, cands):
    for c in cands:
        if n % c == 0:
            return c
    return n


def _softplus(x):
    return jnp.maximum(x, 0.0) + jnp.log1p(jnp.exp(-jnp.abs(x)))


def _log_sigmoid(x):
    return jnp.minimum(x, 0.0) - jnp.log1p(jnp.exp(-jnp.abs(x)))


def _sigmoid(x):
    return 1.0 / (1.0 + jnp.exp(-x))


def _silu(x):
    return x * _sigmoid(x)


def _rms_rows(x):
    return x * lax.rsqrt(jnp.mean(x * x, axis=-1, keepdims=True) + EPS)


def _dot(a, b):
    return jnp.dot(a, b, preferred_element_type=F32)


def _dot_hi(a, b):
    return jnp.dot(a, b, preferred_element_type=F32, precision=HI)


def _dot_nt(a, b):
    return lax.dot_general(a, b, NT_DIMS, preferred_element_type=F32)


def _dot_tn(a, b):
    return lax.dot_general(a, b, TN_DIMS, preferred_element_type=F32)


def _iota(shape, dim):
    return lax.broadcasted_iota(jnp.int32, shape, dim)


def _linear_body(*refs, n_x, use_norm, act, has_res):
    it = iter(refs)
    x_refs = [next(it) for _ in range(n_x)]
    g_ref = next(it) if use_norm else None
    w_refs = [next(it) for _ in range(n_x)]
    b_ref = next(it) if act else None
    r_ref = next(it) if has_res else None
    o_ref = next(it)
    xs_refs = [next(it) for _ in range(n_x)]

    @pl.when(pl.program_id(1) == 0)
    def _():
        for i in range(n_x):
            x = x_refs[i][...]
            if use_norm and i == 0:
                x = _rms_rows(x) * g_ref[...]
            xs_refs[i][...] = x.astype(BF16)

    acc = _dot(xs_refs[0][...], w_refs[0][...])
    for i in range(1, n_x):
        acc = acc + _dot(xs_refs[i][...], w_refs[i][...])
    if act == "log_sigmoid":
        acc = _log_sigmoid(acc + b_ref[...])
    if has_res:
        acc = acc + r_ref[...]
    o_ref[...] = acc


def fused_linear(xs, ws, gain=None, bias=None, act=None, residual=None, name="linear"):
    m = xs[0].shape[0]
    n = ws[0].shape[1]
    tm = _pick(m, (1024, 512, 256, 128, 64, 32, 16, 8))
    tn = _pick(n, (1536, 1280, 1024, 768, 640, 512, 384, 256, 128))
    n_x = len(xs)
    in_specs, args = [], []
    for x in xs:
        in_specs.append(pl.BlockSpec((tm, x.shape[1]), lambda i, j: (i, 0)))
        args.append(x)
    if gain is not None:
        in_specs.append(pl.BlockSpec((1, xs[0].shape[1]), lambda i, j: (0, 0)))
        args.append(gain.reshape(1, -1).astype(F32))
    for w in ws:
        in_specs.append(pl.BlockSpec((w.shape[0], tn), lambda i, j: (0, j)))
        args.append(w)
    if act:
        in_specs.append(pl.BlockSpec((1, tn), lambda i, j: (0, j)))
        args.append(bias.reshape(1, -1).astype(F32))
    if residual is not None:
        in_specs.append(pl.BlockSpec((tm, tn), lambda i, j: (i, j)))
        args.append(residual)
    body = functools.partial(_linear_body, n_x=n_x, use_norm=gain is not None, act=act,
                             has_res=residual is not None)
    return pl.pallas_call(
        body,
        grid=(m // tm, n // tn),
        in_specs=in_specs,
        out_specs=pl.BlockSpec((tm, tn), lambda i, j: (i, j)),
        out_shape=jax.ShapeDtypeStruct((m, n), F32),
        scratch_shapes=[pltpu.VMEM((tm, x.shape[1]), BF16) for x in xs],
        compiler_params=_cparams("parallel", "arbitrary"),
        name=name,
    )(*args)


SUB = 8


def _conv_body(x_ref, prev_ref, buf_ref, w_ref, b_ref, o_ref):
    first = pl.program_id(1) == 0
    ext = jnp.concatenate([jnp.where(first, buf_ref[0], prev_ref[0]), x_ref[0]], axis=0)
    w = w_ref[...]

    def tap(j):
        back = CONV_K - 1 - j
        shifted = pltpu.roll(ext, back, axis=0) if back else ext
        return shifted[SUB:, :] * w[j:j + 1, :]

    y = tap(0)
    for j in range(1, CONV_K):
        y = y + tap(j)
    y = y + b_ref[...]
    o_ref[0] = _silu(y)


def conv_silu(x, buf, w, b):
    bsz, seq, ch = x.shape
    tt = _pick(seq, (512, 256, 128, 64, 32, 16, 8))
    tc = ch if tt <= 64 else _pick(ch, (1024, 640, 512, 256, 128))
    bufp = jnp.concatenate([jnp.zeros((bsz, SUB - (CONV_K - 1), ch), F32), buf], axis=1)
    wp = jnp.concatenate([w, jnp.zeros((SUB - CONV_K, ch), F32)], axis=0)
    nsub = tt // SUB
    return pl.pallas_call(
        _conv_body,
        grid=(bsz, seq // tt, ch // tc),
        in_specs=[
            pl.BlockSpec((1, tt, tc), lambda bi, ti, ci: (bi, ti, ci)),
            pl.BlockSpec((1, SUB, tc), lambda bi, ti, ci: (bi, jnp.maximum(ti * nsub - 1, 0), ci)),
            pl.BlockSpec((1, SUB, tc), lambda bi, ti, ci: (bi, 0, ci)),
            pl.BlockSpec((SUB, tc), lambda bi, ti, ci: (0, ci)),
            pl.BlockSpec((1, tc), lambda bi, ti, ci: (0, ci)),
        ],
        out_specs=pl.BlockSpec((1, tt, tc), lambda bi, ti, ci: (bi, ti, ci)),
        out_shape=jax.ShapeDtypeStruct((bsz, seq, ch), F32),
        compiler_params=_cparams("parallel", "parallel", "parallel"),
        name="conv_silu",
    )(x, x, bufp, wp, b.reshape(1, ch))


HG = SSD_HEADS // SSD_GROUPS
GW = HG * SSD_HEAD_DIM


def _expand_heads(v, e_bf16):
    hi = v.astype(BF16)
    r1 = v - hi.astype(F32)
    mid = r1.astype(BF16)
    lo = (r1 - mid.astype(F32)).astype(BF16)
    return _dot(hi, e_bf16) + _dot(mid, e_bf16) + _dot(lo, e_bf16)


def _ssd_body(xbc_ref, z_ref, small_ref, dtt_ref, h0_ref, prow_ref, pcol_ref, dexp_ref, norm_ref,
              y_ref, hout_ref, h_sc, y_sc, *, q):
    @pl.when(pl.program_id(1) == 0)
    def _():
        h_sc[...] = h0_ref[0]

    xbc = xbc_ref[0]
    xs = xbc[:, :SSD_INNER]
    bm = xbc[:, SSD_INNER:SSD_INNER + LANES]
    cm = xbc[:, SSD_INNER + LANES:SSD_INNER + 2 * LANES]
    dt_c = _softplus(small_ref[0][:, 0:SSD_HEADS] + prow_ref[0:1, 0:SSD_HEADS])
    a_c = dt_c * (-jnp.exp(prow_ref[1:2, 0:SSD_HEADS]))
    dt_t = _softplus(dtt_ref[0] + pcol_ref[:, 0:1])
    a_t = dt_t * (-jnp.exp(pcol_ref[:, 1:2]))
    ri = _iota((q, q), 0)
    ci = _iota((q, q), 1)
    causal = ci <= ri
    cum = _dot_hi(causal.astype(F32), a_c)
    cum_t = _dot_hi(a_t, (ri <= ci).astype(F32))
    e_heads = (_iota((SSD_HEADS, SSD_INNER), 1) // SSD_HEAD_DIM == _iota((SSD_HEADS, SSD_INNER), 0)).astype(BF16)
    dt_x = _expand_heads(dt_c, e_heads)
    ecum_x = _expand_heads(jnp.exp(cum), e_heads)
    wlast_x = _expand_heads(jnp.exp(cum[q - 1:q, :] - cum), e_heads)
    xdt = xs * dt_x
    xw = (xdt * wlast_x).astype(BF16)
    xdt_b = xdt.astype(BF16)

    lane = _iota((1, LANES), 1)
    low = lane < SSD_STATE
    bm_b = bm.astype(BF16)
    hs = h_sc[...]
    hs_b = hs.astype(BF16)
    y_inter = []
    cbs = []
    upd = []
    for g in range(SSD_GROUPS):
        cm_g = jnp.where(low if g == 0 else jnp.logical_not(low), cm, 0.0).astype(BF16)
        cbs.append(_dot_nt(cm_g, bm_b))
        y_inter.append(_dot(cm_g, hs_b))
        upd.append(_dot_tn(bm_b, xw[:, g * GW:(g + 1) * GW]))
    for j in range(SSD_HEADS // 2):
        g = (2 * j) // HG
        xp = xdt_b[:, j * LANES:(j + 1) * LANES]
        ys = []
        for hh in (2 * j, 2 * j + 1):
            seg = cum[:, hh:hh + 1] - cum_t[hh:hh + 1, :]
            lm = jnp.exp(jnp.where(causal, seg, -jnp.inf))
            ys.append(_dot((cbs[g] * lm).astype(BF16), xp))
        y_sc[:, j * LANES:(j + 1) * LANES] = jnp.where(low, ys[0], ys[1])
    y = y_sc[...] + jnp.concatenate(y_inter, axis=1) * ecum_x + dexp_ref[...] * xs
    y = y * _silu(z_ref[0])
    y = jnp.concatenate([_rms_rows(y[:, g * GW:(g + 1) * GW]) for g in range(SSD_GROUPS)], axis=1)
    y_ref[0] = y * norm_ref[...]

    row_low = _iota((2 * SSD_STATE, 1), 0) < SSD_STATE
    ecl = ecum_x[q - 1:q, :]
    decay = jnp.where(row_low, ecl[:, 0:GW], ecl[:, GW:2 * GW])
    h_new = hs * decay + jnp.where(row_low, upd[0], upd[1])
    h_sc[...] = h_new

    @pl.when(pl.program_id(1) == pl.num_programs(1) - 1)
    def _():
        hout_ref[0] = h_new


def ssd_scan(xbc_c, z, small, h0, dt_bias, a_log, d_skip, norm_w):
    bsz, seq, _ = xbc_c.shape
    q = _pick(seq, (128, 64, 32, 16, 8))
    dtt = jnp.swapaxes(small[:, :, 0:SSD_HEADS], 1, 2)
    hs0 = h0.reshape(bsz, SSD_GROUPS, HG, SSD_HEAD_DIM, SSD_STATE).transpose(0, 1, 4, 2, 3)
    hs0 = hs0.reshape(bsz, SSD_GROUPS * SSD_STATE, GW)
    prow = jnp.zeros((SUB, LANES), F32).at[0, :SSD_HEADS].set(dt_bias).at[1, :SSD_HEADS].set(a_log)
    pcol = jnp.zeros((SSD_HEADS, LANES), F32).at[:, 0].set(dt_bias).at[:, 1].set(a_log)
    dexp = jnp.repeat(d_skip, SSD_HEAD_DIM).reshape(1, SSD_INNER)
    y, hs = pl.pallas_call(
        functools.partial(_ssd_body, q=q),
        grid=(bsz, seq // q),
        in_specs=[
            pl.BlockSpec((1, q, SSD_CONV_DIM), lambda b, c: (b, c, 0)),
            pl.BlockSpec((1, q, SSD_INNER), lambda b, c: (b, c, 0)),
            pl.BlockSpec((1, q, LANES), lambda b, c: (b, c, 0)),
            pl.BlockSpec((1, SSD_HEADS, q), lambda b, c: (b, 0, c)),
            pl.BlockSpec((1, 2 * SSD_STATE, GW), lambda b, c: (b, 0, 0)),
            pl.BlockSpec((SUB, LANES), lambda b, c: (0, 0)),
            pl.BlockSpec((SSD_HEADS, LANES), lambda b, c: (0, 0)),
            pl.BlockSpec((1, SSD_INNER), lambda b, c: (0, 0)),
            pl.BlockSpec((1, SSD_INNER), lambda b, c: (0, 0)),
        ],
        out_specs=[
            pl.BlockSpec((1, q, SSD_INNER), lambda b, c: (b, c, 0)),
            pl.BlockSpec((1, 2 * SSD_STATE, GW), lambda b, c: (b, 0, 0)),
        ],
        out_shape=[
            jax.ShapeDtypeStruct((bsz, seq, SSD_INNER), F32),
            jax.ShapeDtypeStruct((bsz, 2 * SSD_STATE, GW), F32),
        ],
        scratch_shapes=[pltpu.VMEM((2 * SSD_STATE, GW), F32), pltpu.VMEM((q, SSD_INNER), F32)],
        compiler_params=_cparams("parallel", "arbitrary"),
        name="ssd_scan",
    )(xbc_c, z, small, dtt, hs0, prow, pcol, dexp, norm_w.reshape(1, SSD_INNER))
    h_new = hs.reshape(bsz, SSD_GROUPS, SSD_STATE, HG, SSD_HEAD_DIM).transpose(0, 1, 3, 4, 2)
    return y, h_new.reshape(bsz, SSD_HEADS, SSD_HEAD_DIM, SSD_STATE)


def _split2(x):
    hi = x.astype(BF16)
    return hi, (x - hi.astype(F32)).astype(BF16)


def _dot3(a, b):
    ah, al = _split2(a)
    bh, bl = _split2(b)
    return _dot(ah, bh) + (_dot(ah, bl) + _dot(al, bh))


def _gdn_prep_body(q_ref, k_ref, v_ref, small_ref, gbt_ref, prow_ref,
                   u_ref, w_ref, qg_ref, kd_ref, qk_ref, eg_ref, *, c, nch, hp):
    lane = _iota((1, LANES), 1)

    def pick_lane(x, idx):
        return jnp.sum(jnp.where(lane == idx, x, 0.0), axis=1, keepdims=True)

    small = small_ref[0]
    beta_col, g_col, g_row = [], [], []
    for i in range(hp):
        h = pl.program_id(1) * hp + i
        neg_a = -jnp.exp(pick_lane(prow_ref[0:1, :], h))
        dtb = pick_lane(prow_ref[1:2, :], h)
        beta_col.append(_sigmoid(pick_lane(small, SSD_HEADS + h)))
        g_col.append(neg_a * _softplus(pick_lane(small, SSD_HEADS + GDN_HEADS + h) + dtb))
        g_row.append(neg_a * _softplus(gbt_ref[0, pl.ds(GDN_HEADS + h, 1), :] + dtb))

    ri = _iota((c, c), 0)
    ci = _iota((c, c), 1)
    tril = ri >= ci
    tril_f = tril.astype(F32)
    triu_f = (ri <= ci).astype(F32)
    eye = (ri == ci).astype(F32)
    units = [(i, ch) for i in range(hp) for ch in range(nch)]
    rows = [slice(ch * c, (ch + 1) * c) for _, ch in units]
    lanes = [slice(i * GDN_DK, (i + 1) * GDN_DK) for i, _ in units]
    nu = range(len(units))
    qn, kn, kb, gam, dec, nmat = [], [], [], [], [], []
    for n in nu:
        i = units[n][0]
        qc = q_ref[0, rows[n], lanes[n]]
        kc = k_ref[0, rows[n], lanes[n]]
        qn.append(qc * lax.rsqrt(jnp.sum(qc * qc, axis=-1, keepdims=True) + 1e-6) * (GDN_DK ** -0.5))
        kn.append(kc * lax.rsqrt(jnp.sum(kc * kc, axis=-1, keepdims=True) + 1e-6))
        kb.append(kn[n] * beta_col[i][rows[n], :])
        gam.append(_dot_hi(tril_f, jnp.broadcast_to(g_col[i][rows[n], :], (c, LANES))))
    for n in nu:
        gam_r = _dot_hi(jnp.broadcast_to(g_row[units[n][0]][:, rows[n]], (SUB, c)), triu_f)[0:1, :]
        dec.append(jnp.exp(jnp.where(tril, gam[n][:, 0:c] - gam_r, -jnp.inf)))
        nmat.append(jnp.where(ri > ci, _dot_nt(kb[n].astype(BF16), kn[n].astype(BF16)) * dec[n], 0.0) * -1.0)
    tinv = [eye + m for m in nmat]
    npow = [_dot3(m, m) for m in nmat]
    levels = int(math.log2(c))
    for lvl in range(1, levels):
        for n in nu:
            if lvl == levels - 1:
                tinv[n] = tinv[n] + _dot3(npow[n], tinv[n])
            else:
                both = _dot3(npow[n], jnp.concatenate([tinv[n], npow[n]], axis=1))
                tinv[n] = tinv[n] + both[:, :c]
                npow[n] = both[:, c:]
    for n in nu:
        i, ch = units[n]
        egam = jnp.exp(gam[n])
        vb = (v_ref[0, rows[n], lanes[n]] * beta_col[i][rows[n], :]).astype(BF16)
        uw = _dot(tinv[n].astype(BF16), jnp.concatenate([vb, (kb[n] * egam).astype(BF16)], axis=1))
        glast = gam[n][c - 1:c, :]
        u_ref[0, rows[n], lanes[n]] = uw[:, :GDN_DV]
        w_ref[0, rows[n], lanes[n]] = uw[:, GDN_DV:].astype(w_ref.dtype)
        qg_ref[0, rows[n], lanes[n]] = (qn[n] * egam).astype(qg_ref.dtype)
        kd_ref[0, rows[n], lanes[n]] = (kn[n] * jnp.exp(glast - gam[n])).astype(kd_ref.dtype)
        qk_ref[0, i, rows[n], :] = (_dot_nt(qn[n].astype(BF16), kn[n].astype(BF16)) * dec[n]).astype(qk_ref.dtype)
        eg_ref[0, i, ch:ch + 1, :] = jnp.exp(glast)


GDN_GROUP = 4


def _gdn_scan_body(u_ref, w_ref, qg_ref, kd_ref, qk_ref, eg_ref, gate_ref, s0_ref, norm_ref,
                   o_ref, sout_ref, s_sc, *, c, nch):
    @pl.when(pl.program_id(2) == 0)
    def _():
        s_sc[...] = s0_ref[0]

    s = [s_sc[i] for i in range(GDN_GROUP)]
    for ch in range(nch):
        rows = slice(ch * c, (ch + 1) * c)
        for i in range(GDN_GROUP):
            lanes = slice(i * GDN_DV, (i + 1) * GDN_DV)
            s_b = s[i].astype(BF16)
            wq = jnp.concatenate([w_ref[0, rows, lanes].astype(BF16), qg_ref[0, rows, lanes].astype(BF16)], axis=0)
            ws = _dot(wq, s_b)
            vn_b = (u_ref[0, rows, lanes] - ws[:c]).astype(BF16)
            o = ws[c:] + _dot(qk_ref[0, i, rows, :].astype(BF16), vn_b)
            s[i] = s[i] * eg_ref[0, i, ch:ch + 1, :] + _dot_tn(kd_ref[0, rows, lanes].astype(BF16), vn_b)
            o_ref[0, rows, lanes] = _rms_rows(o) * norm_ref[...] * _silu(gate_ref[0, rows, lanes])
    for i in range(GDN_GROUP):
        s_sc[i] = s[i]

    @pl.when(pl.program_id(2) == pl.num_programs(2) - 1)
    def _():
        for i in range(GDN_GROUP):
            sout_ref[0, i] = s[i]


def gdn_scan(qkv_c, gate, small, s0, dt_bias, a_log, norm_w):
    bsz, seq, _ = qkv_c.shape
    c = _pick(seq, (64, 32, 16, 8))
    tb = _pick(seq, (512, 256, 128, 64, 32, 16, 8))
    nch = tb // c
    wdt = BF16 if c % 16 == 0 else F32
    gbt = jnp.swapaxes(small[:, :, SSD_HEADS:SSD_HEADS + 2 * GDN_HEADS], 1, 2)
    prow = jnp.zeros((SUB, LANES), F32).at[0, :GDN_HEADS].set(a_log).at[1, :GDN_HEADS].set(dt_bias)
    hp = 2 if nch >= 4 else GDN_HEADS
    ng = GDN_HEADS // hp
    head_blk = pl.BlockSpec((1, tb, hp * GDN_DV), lambda b, h, t: (b, t, h))
    u, w, qg, kd, qk, eg = pl.pallas_call(
        functools.partial(_gdn_prep_body, c=c, nch=nch, hp=hp),
        grid=(bsz, ng, seq // tb),
        in_specs=[
            pl.BlockSpec((1, tb, hp * GDN_DK), lambda b, h, t: (b, t, h)),
            pl.BlockSpec((1, tb, hp * GDN_DK), lambda b, h, t: (b, t, ng + h)),
            pl.BlockSpec((1, tb, hp * GDN_DV), lambda b, h, t: (b, t, 2 * ng + h)),
            pl.BlockSpec((1, tb, LANES), lambda b, h, t: (b, t, 0)),
            pl.BlockSpec((1, 2 * GDN_HEADS, tb), lambda b, h, t: (b, 0, t)),
            pl.BlockSpec((SUB, LANES), lambda b, h, t: (0, 0)),
        ],
        out_specs=[head_blk, head_blk, head_blk, head_blk,
                   pl.BlockSpec((1, hp, tb, c), lambda b, h, t: (b, h, t, 0)),
                   pl.BlockSpec((1, hp, nch, LANES), lambda b, h, t: (b, h, t, 0))],
        out_shape=[
            jax.ShapeDtypeStruct((bsz, seq, GDN_VW), F32),
            jax.ShapeDtypeStruct((bsz, seq, GDN_VW), wdt),
            jax.ShapeDtypeStruct((bsz, seq, GDN_QK), wdt),
            jax.ShapeDtypeStruct((bsz, seq, GDN_QK), wdt),
            jax.ShapeDtypeStruct((bsz, GDN_HEADS, seq, c), wdt),
            jax.ShapeDtypeStruct((bsz, GDN_HEADS, seq // c, LANES), F32),
        ],
        compiler_params=_cparams("parallel", "parallel", "parallel"),
        name="gdn_prep",
    )(qkv_c, qkv_c, qkv_c, small, gbt, prow)
    gw = GDN_GROUP * GDN_DV
    grp_blk = pl.BlockSpec((1, tb, gw), lambda b, g, t: (b, t, g))
    state_blk = pl.BlockSpec((1, GDN_GROUP, GDN_DK, GDN_DV), lambda b, g, t: (b, g, 0, 0))
    o, s_new = pl.pallas_call(
        functools.partial(_gdn_scan_body, c=c, nch=nch),
        grid=(bsz, GDN_HEADS // GDN_GROUP, seq // tb),
        in_specs=[
            grp_blk, grp_blk, grp_blk, grp_blk,
            pl.BlockSpec((1, GDN_GROUP, tb, c), lambda b, g, t: (b, g, t, 0)),
            pl.BlockSpec((1, GDN_GROUP, nch, LANES), lambda b, g, t: (b, g, t, 0)),
            grp_blk,
            state_blk,
            pl.BlockSpec((1, GDN_DV), lambda b, g, t: (0, 0)),
        ],
        out_specs=[grp_blk, state_blk],
        out_shape=[
            jax.ShapeDtypeStruct((bsz, seq, GDN_VW), F32),
            jax.ShapeDtypeStruct((bsz, GDN_HEADS, GDN_DK, GDN_DV), F32),
        ],
        scratch_shapes=[pltpu.VMEM((GDN_GROUP, GDN_DK, GDN_DV), F32)],
        compiler_params=_cparams("parallel", "parallel", "arbitrary"),
        name="gdn_scan",
    )(u, w, qg, kd, qk, eg, gate, s0, norm_w.reshape(1, GDN_DV))
    return o, s_new


def _xattn_body(h_ref, g_ref, wq_ref, mk_ref, mv_ref, wo_ref, o_ref):
    x = h_ref[0]
    xn = (_rms_rows(x) * g_ref[...]).astype(BF16)
    qf = _dot(xn, wq_ref[...])
    mk = mk_ref[0].astype(BF16)
    mv = mv_ref[0].astype(BF16)
    outs = []
    for hd in range(X_HEADS):
        sl = slice(hd * X_HEAD_DIM, (hd + 1) * X_HEAD_DIM)
        s = _dot_nt(qf[:, sl].astype(BF16), mk[:, sl]) * (X_HEAD_DIM ** -0.5)
        p = jnp.exp(s - jnp.max(s, axis=-1, keepdims=True))
        p = p / jnp.sum(p, axis=-1, keepdims=True)
        outs.append(_dot(p.astype(BF16), mv[:, sl]))
    o = jnp.concatenate(outs, axis=1).astype(BF16)
    o_ref[0] = x + _dot(o, wo_ref[...])


def cross_attn(h, gain, wq, mk, mv, wo):
    bsz, seq, d = h.shape
    tm = _pick(seq, (512, 256, 128, 64, 32, 16, 8))
    return pl.pallas_call(
        _xattn_body,
        grid=(bsz, seq // tm),
        in_specs=[
            pl.BlockSpec((1, tm, d), lambda b, i: (b, i, 0)),
            pl.BlockSpec((1, d), lambda b, i: (0, 0)),
            pl.BlockSpec((d, X_W), lambda b, i: (0, 0)),
            pl.BlockSpec((1, MEM_LEN, X_W), lambda b, i: (b, 0, 0)),
            pl.BlockSpec((1, MEM_LEN, X_W), lambda b, i: (b, 0, 0)),
            pl.BlockSpec((X_W, d), lambda b, i: (0, 0)),
        ],
        out_specs=pl.BlockSpec((1, tm, d), lambda b, i: (b, i, 0)),
        out_shape=jax.ShapeDtypeStruct((bsz, seq, d), F32),
        compiler_params=_cparams("parallel", "parallel"),
        name="cross_attn",
    )(h, gain.reshape(1, d), wq, mk, mv, wo)


def _swiglu_body(h_ref, g_ref, w1_ref, w3_ref, w2_ref, *rest, final_norm):
    gf_ref = rest[0] if final_norm else None
    o_ref, xn_sc = rest[-2:]
    j = pl.program_id(1)

    @pl.when(j == 0)
    def _():
        x = h_ref[...]
        xn_sc[...] = (_rms_rows(x) * g_ref[...]).astype(BF16)
        o_ref[...] = x

    xn = xn_sc[...]
    a = _dot(xn, w1_ref[0])
    b = _dot(xn, w3_ref[0])
    o_ref[...] += _dot((_silu(a) * b).astype(BF16), w2_ref[0])

    if final_norm:
        @pl.when(j == pl.num_programs(1) - 1)
        def _():
            o_ref[...] = _rms_rows(o_ref[...]) * gf_ref[...]


def swiglu(h, gain, w1, w3, w2, layer, final_gain=None):
    m, d = h.shape
    ff = w1.shape[2]
    tm = _pick(m, (512, 256, 128, 64, 32, 16, 8))
    tf = _pick(ff, (1408, 1024, 512, 256, 128))
    in_specs = [
        pl.BlockSpec((tm, d), lambda i, j: (i, 0)),
        pl.BlockSpec((1, d), lambda i, j: (0, 0)),
        pl.BlockSpec((1, d, tf), lambda i, j: (layer, 0, j)),
        pl.BlockSpec((1, d, tf), lambda i, j: (layer, 0, j)),
        pl.BlockSpec((1, tf, d), lambda i, j: (layer, j, 0)),
    ]
    args = [h, gain.reshape(1, d), w1, w3, w2]
    if final_gain is not None:
        in_specs.append(pl.BlockSpec((1, d), lambda i, j: (0, 0)))
        args.append(final_gain.reshape(1, d))
    return pl.pallas_call(
        functools.partial(_swiglu_body, final_norm=final_gain is not None),
        grid=(m // tm, ff // tf),
        in_specs=in_specs,
        out_specs=pl.BlockSpec((tm, d), lambda i, j: (i, 0)),
        out_shape=jax.ShapeDtypeStruct((m, d), F32),
        scratch_shapes=[pltpu.VMEM((tm, d), BF16)],
        compiler_params=_cparams("parallel", "arbitrary"),
        name="swiglu",
    )(*args)


def _cumsum_lanes_body(x_ref, o_ref, carry_sc, *, tc):
    @pl.when(pl.program_id(1) == 0)
    def _():
        carry_sc[...] = jnp.zeros_like(carry_sc)

    upper = (_iota((tc, tc), 0) <= _iota((tc, tc), 1)).astype(F32)
    f = _dot_hi(x_ref[0], upper) + carry_sc[:, 0:1]
    o_ref[0] = f
    carry_sc[...] = jnp.broadcast_to(f[:, tc - 1:tc], carry_sc.shape)


def cumsum_lanes(x):
    bsz, r, seq = x.shape
    tc = _pick(seq, (512, 256, 128))
    return pl.pallas_call(
        functools.partial(_cumsum_lanes_body, tc=tc),
        grid=(bsz, seq // tc),
        in_specs=[pl.BlockSpec((1, r, tc), lambda b, i: (b, 0, i))],
        out_specs=pl.BlockSpec((1, r, tc), lambda b, i: (b, 0, i)),
        out_shape=jax.ShapeDtypeStruct((bsz, r, seq), F32),
        scratch_shapes=[pltpu.VMEM((r, LANES), F32)],
        compiler_params=_cparams("parallel", "arbitrary"),
        name="cumsum_lanes",
    )(x)


FOX_ROWS = 512
FOX_PAIRS = 2


LOG2E = 1.4426950408889634


def _fox_body(qi_ref, ki_ref, q_ref, k_ref, v_ref, f_ref, o_ref, m_sc, acc_sc, *, t):
    step = pl.program_id(2)
    qi = qi_ref[step]
    ki = ki_ref[step]
    low = _iota((1, LANES), 1) < FOX_HEAD_DIM
    rb = min(FOX_ROWS, t)

    @pl.when(ki == 0)
    def _():
        m_sc[...] = jnp.full(m_sc.shape, -jnp.inf, F32)
        acc_sc[...] = jnp.zeros_like(acc_sc)

    def update(masked):
        units = [(pr, r, hh) for pr in range(FOX_PAIRS) for r in range(t // rb) for hh in range(2)]
        kbs, vbs = {}, {}

        def nkeys(r):
            return (r + 1) * rb if masked else t

        def qk(u):
            pr, r, hh = u
            lanes = slice(pr * LANES, (pr + 1) * LANES)
            if pr not in kbs:
                kbs[pr] = k_ref[0, :, lanes].astype(BF16)
                v = v_ref[0, :, lanes]
                vbs[pr] = (jnp.where(low, v, 1.0).astype(BF16), jnp.where(low, 1.0, v).astype(BF16))
            qs = q_ref[0, r * rb:(r + 1) * rb, lanes] * (FOX_HEAD_DIM ** -0.5 * LOG2E)
            qm = jnp.where(low if hh == 0 else jnp.logical_not(low), qs, 0.0).astype(BF16)
            return _dot_nt(qm, kbs[pr][:nkeys(r)])

        s_next = qk(units[0])
        for i, (pr, r, hh) in enumerate(units):
            rows = slice(r * rb, (r + 1) * rb)
            nk = nkeys(r)
            s = s_next - f_ref[0, pr, hh:hh + 1, :nk] * LOG2E
            if i + 1 < len(units):
                s_next = qk(units[i + 1])
            if masked:
                s = jnp.where(_iota((rb, nk), 1) <= _iota((rb, nk), 0) + r * rb, s, -jnp.inf)
            m_prev = m_sc[pr, hh, rows, :]
            m_new = jnp.maximum(m_prev, jnp.max(s, axis=-1, keepdims=True))
            alpha = jnp.exp2(m_prev - m_new)
            p = jnp.exp2(s - jnp.concatenate([m_new] * (nk // LANES), axis=1))
            m_sc[pr, hh, rows, :] = m_new
            acc_sc[pr, hh, rows, :] = alpha * acc_sc[pr, hh, rows, :] + _dot(p.astype(BF16), vbs[pr][hh][:nk])

    @pl.when(ki < qi)
    def _():
        update(False)

    @pl.when(ki == qi)
    def _():
        update(True)
        for pr in range(FOX_PAIRS):
            outs = [acc_sc[pr, hh] / pltpu.roll(acc_sc[pr, hh], FOX_HEAD_DIM, axis=1) for hh in range(2)]
            o_ref[0, :, pr * LANES:(pr + 1) * LANES] = jnp.where(low, outs[0], outs[1])


def fox_prompt_attn(q, k, v, ft):
    bsz, seq, _ = q.shape
    t = _pick(seq, (1024, 512, 256, 128))
    nb = seq // t
    pairs =[(i, j) for i in range(nb) for j in range(i + 1)]
    qi = jnp.asarray([p[0] for p in pairs], jnp.int32)
    ki = jnp.asarray([p[1] for p in pairs], jnp.int32)
    npair = FOX_HEADS // 2
    w = FOX_PAIRS * LANES
    ft4 = ft.reshape(bsz, npair, 2, seq)
    grid_spec = pltpu.PrefetchScalarGridSpec(
        num_scalar_prefetch=2,
        grid=(bsz, npair // FOX_PAIRS, len(pairs)),
        in_specs=[
            pl.BlockSpec((1, t, w), lambda b, j, s, qi, ki: (b, qi[s], j)),
            pl.BlockSpec((1, t, w), lambda b, j, s, qi, ki: (b, ki[s], j)),
            pl.BlockSpec((1, t, w), lambda b, j, s, qi, ki: (b, ki[s], j)),
            pl.BlockSpec((1, FOX_PAIRS, 2, t), lambda b, j, s, qi, ki: (b, j, 0, ki[s])),
        ],
        out_specs=pl.BlockSpec((1, t, w), lambda b, j, s, qi, ki: (b, qi[s], j)),
        scratch_shapes=[pltpu.VMEM((FOX_PAIRS, 2, t, LANES), F32)] * 2,
    )
    return pl.pallas_call(
        functools.partial(_fox_body, t=t),
        grid_spec=grid_spec,
        out_shape=jax.ShapeDtypeStruct((bsz, seq, FOX_W), F32),
        compiler_params=_cparams("parallel", "parallel", "arbitrary"),
        name="fox_prompt_attn",
    )(qi, ki, q, k, v, ft4)


def _page_suffix_body(x_ref, rin_ref, tot_ref):
    x = x_ref[...]
    after = (_iota((PAGE_SIZE, PAGE_SIZE), 0) > _iota((PAGE_SIZE, PAGE_SIZE), 1)).astype(F32)
    rin_ref[...] = _dot_hi(x, after)
    tot_ref[...] = jnp.broadcast_to(jnp.sum(x, axis=1, keepdims=True), x.shape)


def page_suffix(lft):
    r = lft.shape[0]
    tr = _pick(r, (2048, 1024, 512, 256, 128, 64, 32, 16))
    spec = pl.BlockSpec((tr, PAGE_SIZE), lambda i: (i, 0))
    return pl.pallas_call(
        _page_suffix_body,
        grid=(r // tr,),
        in_specs=[spec],
        out_specs=[spec, spec],
        out_shape=[jax.ShapeDtypeStruct(lft.shape, F32)] * 2,
        compiler_params=_cparams("parallel"),
        name="page_suffix",
    )(lft)


DEC_PAGES = 8


def _rep_rows(x, n):
    r, c = x.shape
    return jnp.broadcast_to(x[:, None, :], (r, n, c)).reshape(r * n, c)


def _fox_decode_body(pt_ref, q_ref, *refs, nt, nsteps):
    k_refs = refs[0:DEC_PAGES]
    v_refs = refs[DEC_PAGES:2 * DEC_PAGES]
    rin_refs = refs[2 * DEC_PAGES:3 * DEC_PAGES]
    tot_refs = refs[3 * DEC_PAGES:4 * DEC_PAGES]
    kn_ref, vn_ref, lfn_ref, o_ref, qbd_sc, m_sc, l_sc, acc_sc, run_sc = refs[4 * DEC_PAGES:]
    j = pl.program_id(1)
    rows = FOX_HEADS * nt

    @pl.when(j == 0)
    def _():
        qt = jnp.concatenate([q_ref[0] * (FOX_HEAD_DIM ** -0.5)] * FOX_HEADS, axis=0)
        own = _iota((rows, FOX_W), 1) // FOX_HEAD_DIM == _iota((rows, FOX_W), 0) // nt
        qbd_sc[...] = jnp.where(own, qt, 0.0).astype(BF16)
        m_sc[...] = jnp.full(m_sc.shape, -jnp.inf, F32)
        l_sc[...] = jnp.zeros_like(l_sc)
        acc_sc[...] = jnp.zeros_like(acc_sc)
        run_sc[...] = jnp.zeros_like(run_sc)

    def update(ss, vals, transposed):
        m_prev = m_sc[...]
        m_new = m_prev
        for s in ss:
            m_new = jnp.maximum(m_new, jnp.max(s, axis=-1, keepdims=True))
        alpha = jnp.exp(m_prev - m_new)
        l_new = alpha * l_sc[...]
        acc = jnp.concatenate([alpha] * (FOX_W // LANES), axis=1) * acc_sc[...]
        for s, val in zip(ss, vals):
            p = jnp.exp(s - m_new)
            l_new = l_new + jnp.sum(p, axis=-1, keepdims=True)
            acc = acc + (_dot_nt(p.astype(BF16), val) if transposed else _dot(p.astype(BF16), val))
        m_sc[...] = m_new
        l_sc[...] = l_new
        acc_sc[...] = acc

    @pl.when(j < nsteps)
    def _():
        qbd = qbd_sc[...]
        raw = [_dot(qbd, k_refs[i][0, 0].reshape(FOX_W, PAGE_SIZE).astype(BF16)) for i in range(DEC_PAGES)]
        run = run_sc[...]
        ss = []
        for i in range(DEC_PAGES):
            ss.append(raw[i] + _rep_rows(rin_refs[i][0] + run, nt))
            run = run + tot_refs[i][0]
        run_sc[...] = run
        update(ss, [v_refs[i][0, 0].reshape(FOX_W, PAGE_SIZE).astype(BF16) for i in range(DEC_PAGES)], True)

    @pl.when(j == nsteps)
    def _():
        pad = jnp.zeros((PAGE_SIZE - nt, FOX_W), F32)
        kb = jnp.concatenate([kn_ref[0], pad], axis=0).astype(BF16)
        vb = jnp.concatenate([vn_ref[0], pad], axis=0).astype(BF16)
        s = _dot_nt(qbd_sc[...], kb)
        incl = (_iota((PAGE_SIZE, PAGE_SIZE), 0) <= _iota((PAGE_SIZE, PAGE_SIZE), 1)).astype(F32)
        fn = _dot_hi(lfn_ref[0], incl)
        tok = _iota((rows, PAGE_SIZE), 0) % nt
        update([jnp.where(_iota((rows, PAGE_SIZE), 1) <= tok, s - _rep_rows(fn, nt), -jnp.inf)], [vb], False)
        o = acc_sc[...] / jnp.concatenate([l_sc[...]] * (FOX_W // LANES), axis=1)
        lane_head = _iota((nt, FOX_W), 1) // FOX_HEAD_DIM
        out = jnp.zeros((nt, FOX_W), F32)
        for hd in range(FOX_HEADS):
            out = out + jnp.where(lane_head == hd, o[hd * nt:(hd + 1) * nt, :], 0.0)
        o_ref[0] = out


def fox_decode_attn(q, k_new, v_new, lfn_t, k_cache_t, v_cache_t, layer, rin, tot, page_table):
    bsz, nt, _ = q.shape
    npages = page_table.shape[1]
    assert npages % DEC_PAGES == 0
    nsteps = npages // DEC_PAGES
    rows = FOX_HEADS * nt

    def page(i):
        return lambda b, j, pt: pt[b, npages - 1 - (jnp.minimum(j, nsteps - 1) * DEC_PAGES + i)]

    cache_specs = [pl.BlockSpec((1, 1, FOX_HEADS, FOX_HEAD_DIM, PAGE_SIZE),
                                functools.partial(lambda b, j, pt, pg: (layer, pg(b, j, pt), 0, 0, 0), pg=page(i)))
                   for i in range(DEC_PAGES)]
    bias_specs = [pl.BlockSpec((1, FOX_HEADS, PAGE_SIZE), functools.partial(lambda b, j, pt, pg: (pg(b, j, pt), 0, 0), pg=page(i)))
                  for i in range(DEC_PAGES)]
    per_seq = lambda shape: pl.BlockSpec(shape, lambda b, j, pt: (b, 0, 0))
    grid_spec = pltpu.PrefetchScalarGridSpec(
        num_scalar_prefetch=1,
        grid=(bsz, nsteps + 1),
        in_specs=[per_seq((1, nt, FOX_W))] + cache_specs + cache_specs + bias_specs + bias_specs
        + [per_seq((1, nt, FOX_W)), per_seq((1, nt, FOX_W)), per_seq((1, FOX_HEADS, PAGE_SIZE))],
        out_specs=per_seq((1, nt, FOX_W)),
        scratch_shapes=[
            pltpu.VMEM((rows, FOX_W), BF16),
            pltpu.VMEM((rows, LANES), F32),
            pltpu.VMEM((rows, LANES), F32),
            pltpu.VMEM((rows, FOX_W), F32),
            pltpu.VMEM((FOX_HEADS, PAGE_SIZE), F32),
        ],
    )
    return pl.pallas_call(
        functools.partial(_fox_decode_body, nt=nt, nsteps=nsteps),
        grid_spec=grid_spec,
        out_shape=jax.ShapeDtypeStruct((bsz, nt, FOX_W), F32),
        compiler_params=_cparams("parallel", "arbitrary"),
        name="fox_decode_attn",
    )(page_table, q, *([k_cache_t] * DEC_PAGES), *([v_cache_t] * DEC_PAGES), *([rin] * DEC_PAGES),
      *([tot] * DEC_PAGES), k_new, v_new, lfn_t)


def _split_hyb_weights(w_in):
    o = 0
    parts = {}
    for name, width in (("z", SSD_INNER), ("xbc", SSD_CONV_DIM), ("dt", SSD_HEADS), ("qkv", GDN_CONV_DIM),
                        ("gate", GDN_VW), ("b", GDN_HEADS), ("a", GDN_HEADS)):
        parts[name] = w_in[:, o:o + width]
        o += width
    small = jnp.concatenate([parts["dt"], parts["b"], parts["a"]], axis=1)
    small = jnp.pad(small, ((0, 0), (0, LANES - small.shape[1])))
    return {k: parts[k].astype(BF16) for k in ("z", "xbc", "qkv", "gate")} | {"small": small.astype(BF16)}


def _hybrid_layer(h, e, W, st):
    bsz, seq, d = h.shape
    m = bsz * seq
    h2 = h.reshape(m, d)
    wp = _split_hyb_weights(W["w_in_hyb"][e])
    gain = W["norm_mix"][2 * e]
    proj = {k: fused_linear([h2], [wp[k]], gain=gain, name="hyb_in_" + k) for k in ("z", "xbc", "qkv", "gate", "small")}
    xbc = proj["xbc"].reshape(bsz, seq, SSD_CONV_DIM)
    qkv = proj["qkv"].reshape(bsz, seq, GDN_CONV_DIM)
    small = proj["small"].reshape(bsz, seq, LANES)
    xbc_c = conv_silu(xbc, st["ssd_conv"], W["ssd_conv_w"][e], W["ssd_conv_b"][e])
    qkv_c = conv_silu(qkv, st["gdn_conv"], W["gdn_conv_w"][e], jnp.zeros((GDN_CONV_DIM,), F32))
    y, ssd_h = ssd_scan(xbc_c, proj["z"].reshape(bsz, seq, SSD_INNER), small, st["ssd"],
                        W["ssd_dt_bias"][e], W["ssd_A_log"][e], W["ssd_D"][e], W["ssd_norm"][e])
    o, gdn_s = gdn_scan(qkv_c, proj["gate"].reshape(bsz, seq, GDN_VW), small, st["gdn"],
                        W["gdn_dt_bias"][e], W["gdn_A_log"][e], W["gdn_norm"][e])
    w_out = W["w_out_hyb"][e].astype(BF16)
    h_new = fused_linear([y.reshape(m, SSD_INNER), o.reshape(m, GDN_VW)], [w_out[:SSD_INNER], w_out[SSD_INNER:]],
                         residual=h2, name="hyb_out")
    new = dict(ssd=ssd_h, ssd_conv=xbc[:, seq - (CONV_K - 1):], gdn=gdn_s, gdn_conv=qkv[:, seq - (CONV_K - 1):])
    return h_new.reshape(bsz, seq, d), new


def _fox_layer(h, o_idx, layer, W, st, prompt):
    bsz, seq, d = h.shape
    m = bsz * seq
    h2 = h.reshape(m, d)
    w_in = W["w_in_fox"][o_idx]
    gain = W["norm_mix"][layer]
    wq, wk, wv = (w_in[:, i * FOX_W:(i + 1) * FOX_W].astype(BF16) for i in range(3))
    wf = jnp.pad(w_in[:, 3 * FOX_W:], ((0, 0), (0, LANES - FOX_HEADS))).astype(BF16)
    bf = jnp.pad(W["b_fox_f"][o_idx], (0, LANES - FOX_HEADS))
    q = fused_linear([h2], [wq], gain=gain, name="fox_q").reshape(bsz, seq, FOX_W)
    k = fused_linear([h2], [wk], gain=gain, name="fox_k").reshape(bsz, seq, FOX_W)
    v = fused_linear([h2], [wv], gain=gain, name="fox_v").reshape(bsz, seq, FOX_W)
    lf = fused_linear([h2], [wf], gain=gain, bias=bf, act="log_sigmoid", name="fox_f")[:, :FOX_HEADS]
    lf = lf.reshape(bsz, seq, FOX_HEADS)
    lf_t = jnp.swapaxes(lf, 1, 2)
    if prompt:
        att = fox_prompt_attn(q, k, v, cumsum_lanes(lf_t))
    else:
        n_pool = st["cache_fox_k"].shape[1]
        lfc_t = jnp.swapaxes(st["cache_fox_lf"][o_idx].astype(F32), 1, 2).reshape(n_pool * FOX_HEADS, PAGE_SIZE)
        rin, tot = page_suffix(lfc_t)
        rin = rin.reshape(n_pool, FOX_HEADS, PAGE_SIZE)
        tot = tot.reshape(n_pool, FOX_HEADS, PAGE_SIZE)
        lfn_t = jnp.pad(lf_t, ((0, 0), (0, 0), (0, PAGE_SIZE - seq)))
        kc_t = jnp.transpose(st["cache_fox_k"], (0, 1, 3, 4, 2))
        vc_t = jnp.transpose(st["cache_fox_v"], (0, 1, 3, 4, 2))
        att = fox_decode_attn(q, k, v, lfn_t, kc_t, vc_t, o_idx, rin, tot, st["page_table"])
    h_new = fused_linear([att.reshape(m, FOX_W)], [W["w_out_fox"][o_idx].astype(BF16)], residual=h2, name="fox_out")
    shp = (bsz, seq, FOX_HEADS, FOX_HEAD_DIM)
    return h_new.reshape(bsz, seq, d), dict(fox_k=k.reshape(shp), fox_v=v.reshape(shp), fox_lf=lf)


def _run_group(x, W, st, prompt):
    bsz, seq, d = x.shape
    depth = W["norm_mix"].shape[0]
    names = ("ssd", "ssd_conv", "gdn", "gdn_conv", "fox_k", "fox_v", "fox_lf", "mem_k", "mem_v")
    out = {n: [] for n in names}
    h = x
    for layer in range(depth):
        if layer % 2 == 0:
            e = layer // 2
            if prompt:
                s0 = dict(ssd=jnp.zeros((bsz, SSD_HEADS, SSD_HEAD_DIM, SSD_STATE), F32),
                          ssd_conv=jnp.zeros((bsz, CONV_K - 1, SSD_CONV_DIM), F32),
                          gdn=jnp.zeros((bsz, GDN_HEADS, GDN_DK, GDN_DV), F32),
                          gdn_conv=jnp.zeros((bsz, CONV_K - 1, GDN_CONV_DIM), F32))
            else:
                s0 = dict(ssd=st["state_ssd"][e], ssd_conv=st["state_ssd_conv"][e],
                          gdn=st["state_gdn"][e], gdn_conv=st["state_gdn_conv"][e])
            h, new = _hybrid_layer(h, e, W, s0)
        else:
            h, new = _fox_layer(h, layer // 2, layer, W, st, prompt)
        for n, val in new.items():
            out[n].append(val)
        if prompt:
            mem = st["mem"]
            mem2 = mem.reshape(bsz * MEM_LEN, d)
            mk = fused_linear([mem2], [W["wk_x"][layer].astype(BF16)], gain=W["norm_mem"][layer], name="mem_k")
            mv = fused_linear([mem2], [W["wv_x"][layer].astype(BF16)], gain=W["norm_mem"][layer], name="mem_v")
            mk = mk.reshape(bsz, MEM_LEN, X_W)
            mv = mv.reshape(bsz, MEM_LEN, X_W)
            out["mem_k"].append(mk.reshape(bsz, MEM_LEN, X_HEADS, X_HEAD_DIM))
            out["mem_v"].append(mv.reshape(bsz, MEM_LEN, X_HEADS, X_HEAD_DIM))
        else:
            mk = st["cache_mem_k"][layer].reshape(bsz, MEM_LEN, X_W)
            mv = st["cache_mem_v"][layer].reshape(bsz, MEM_LEN, X_W)
        h = cross_attn(h, W["norm_x"][layer], W["wq_x"][layer].astype(BF16), mk, mv, W["wo_x"][layer].astype(BF16))
        h = swiglu(h.reshape(bsz * seq, d), W["norm_ffn"][layer], W["w1_b"], W["w3_b"], W["w2_b"], layer,
                   final_gain=W["norm_final"] if layer == depth - 1 else None).reshape(bsz, seq, d)
    new = {n: jnp.stack(out[n]) for n in names if out[n]}
    return h, new


def kernel(x_prompt, x_sample, mem_prompt, state_ssd, state_ssd_conv, state_gdn, state_gdn_conv, cache_fox_k, cache_fox_v, cache_fox_lf, page_table, cache_mem_k, cache_mem_v, norm_mix, norm_x, norm_mem, norm_ffn, norm_final, w_in_hyb, w_out_hyb, ssd_conv_w, ssd_conv_b, ssd_dt_bias, ssd_A_log, ssd_D, ssd_norm, gdn_conv_w, gdn_dt_bias, gdn_A_log, gdn_norm, w_in_fox, b_fox_f, w_out_fox, wq_x, wk_x, wv_x, wo_x, w1, w3, w2):
    W = dict(norm_mix=norm_mix, norm_x=norm_x, norm_mem=norm_mem, norm_ffn=norm_ffn, norm_final=norm_final,
             w_in_hyb=w_in_hyb, w_out_hyb=w_out_hyb, ssd_conv_w=ssd_conv_w, ssd_conv_b=ssd_conv_b,
             ssd_dt_bias=ssd_dt_bias, ssd_A_log=ssd_A_log, ssd_D=ssd_D, ssd_norm=ssd_norm,
             gdn_conv_w=gdn_conv_w, gdn_dt_bias=gdn_dt_bias, gdn_A_log=gdn_A_log, gdn_norm=gdn_norm,
             w_in_fox=w_in_fox, b_fox_f=b_fox_f, w_out_fox=w_out_fox,
             wq_x=wq_x, wk_x=wk_x, wv_x=wv_x, wo_x=wo_x,
             w1_b=w1.astype(BF16), w3_b=w3.astype(BF16), w2_b=w2.astype(BF16))
    y_prompt, pn = _run_group(x_prompt, W, dict(mem=mem_prompt), True)
    st = dict(state_ssd=state_ssd, state_ssd_conv=state_ssd_conv, state_gdn=state_gdn,
              state_gdn_conv=state_gdn_conv, cache_fox_k=cache_fox_k, cache_fox_v=cache_fox_v,
              cache_fox_lf=cache_fox_lf, page_table=page_table, cache_mem_k=cache_mem_k, cache_mem_v=cache_mem_v)
    y_sample, sn = _run_group(x_sample, W, st, False)
    return (y_prompt, y_sample,
            pn["ssd"], pn["ssd_conv"], pn["gdn"], pn["gdn_conv"],
            pn["fox_k"], pn["fox_v"], pn["fox_lf"], pn["mem_k"], pn["mem_v"],
            sn["ssd"], sn["ssd_conv"], sn["gdn"], sn["gdn_conv"],
            sn["fox_k"], sn["fox_v"], sn["fox_lf"])
```

```python
import functools
import math

import jax
import jax.numpy as jnp
from jax import lax
from jax.experimental import pallas as pl
from jax.experimental.pallas import tpu as pltpu

F32 = jnp.float32
BF16 = jnp.bfloat16
HI = lax.Precision.HIGHEST
NT_DIMS = (((1,), (1,)), ((), ()))
TN_DIMS = (((0,), (0,)), ((), ()))

D_MODEL = 1024
EPS = 1e-6
CONV_K = 4
SSD_HEADS = 16
SSD_HEAD_DIM = 64
SSD_INNER = SSD_HEADS * SSD_HEAD_DIM
SSD_GROUPS = 2
SSD_STATE = 64
SSD_CONV_DIM = SSD_INNER + 2 * SSD_GROUPS * SSD_STATE
GDN_HEADS = 8
GDN_DK = 128
GDN_DV = 128
GDN_QK = GDN_HEADS * GDN_DK
GDN_VW = GDN_HEADS * GDN_DV
GDN_CONV_DIM = 2 * GDN_QK + GDN_VW
FOX_HEADS = 16
FOX_HEAD_DIM = 64
FOX_W = FOX_HEADS * FOX_HEAD_DIM
PAGE_SIZE = 128
MEM_LEN = 256
X_HEADS = 4
X_HEAD_DIM = 128
X_W = X_HEADS * X_HEAD_DIM
LANES = 128
VMEM_LIMIT = 56 * 1024 * 1024


def _cparams(*sem):
    return pltpu.CompilerParams(dimension_semantics=sem, vmem_limit_bytes=VMEM_LIMIT)


def _pick(n, cands):
    for c in cands:
        if n % c == 0:
            return c
    return n


def _softplus(x):
    return jnp.maximum(x, 0.0) + jnp.log1p(jnp.exp(-jnp.abs(x)))


def _log_sigmoid(x):
    return jnp.minimum(x, 0.0) - jnp.log1p(jnp.exp(-jnp.abs(x)))


def _sigmoid(x):
    return 1.0 / (1.0 + jnp.exp(-x))


def _silu(x):
    return x * _sigmoid(x)


def _rms_rows(x):
    return x * lax.rsqrt(jnp.mean(x * x, axis=-1, keepdims=True) + EPS)


def _dot(a, b):
    return jnp.dot(a, b, preferred_element_type=F32)


def _dot_hi(a, b):
    return jnp.dot(a, b, preferred_element_type=F32, precision=HI)


def _dot_nt(a, b):
    return lax.dot_general(a, b, NT_DIMS, preferred_element_type=F32)


def _dot_tn(a, b):
    return lax.dot_general(a, b, TN_DIMS, preferred_element_type=F32)


def _iota(shape, dim):
    return lax.broadcasted_iota(jnp.int32, shape, dim)


def _linear_body(*refs, n_x, use_norm, act, has_res, transposed):
    it = iter(refs)
    x_refs = [next(it) for _ in range(n_x)]
    g_ref = next(it) if use_norm else None
    w_refs = [next(it) for _ in range(n_x)]
    b_ref = next(it) if act else None
    r_ref = next(it) if has_res else None
    o_ref = next(it)
    xs_refs = [next(it) for _ in range(n_x)]

    @pl.when(pl.program_id(1) == 0)
    def _():
        for i in range(n_x):
            x = x_refs[i][...]
            if use_norm and i == 0:
                x = _rms_rows(x) * g_ref[...]
            xs_refs[i][...] = x.astype(BF16)

    acc = _dot(xs_refs[0][...], w_refs[0][...])
    for i in range(1, n_x):
        acc = acc + _dot(xs_refs[i][...], w_refs[i][...])
    if act == "log_sigmoid":
        acc = _log_sigmoid(acc + b_ref[...])
    if has_res:
        acc = acc + r_ref[...]
    if transposed:
        o_ref[0] = acc.T
    else:
        o_ref[...] = acc


def fused_linear(xs, ws, gain=None, bias=None, act=None, residual=None, name="linear", seq_t=None):
    m = xs[0].shape[0]
    n = ws[0].shape[1]
    tm = _pick(seq_t or m, (1024, 512, 256, 128, 64, 32, 16, 8))
    tn = _pick(n, (1536, 1280, 1024, 768, 640, 512, 384, 256, 128))
    n_x = len(xs)
    in_specs, args = [], []
    for x in xs:
        in_specs.append(pl.BlockSpec((tm, x.shape[1]), lambda i, j: (i, 0)))
        args.append(x)
    if gain is not None:
        in_specs.append(pl.BlockSpec((1, xs[0].shape[1]), lambda i, j: (0, 0)))
        args.append(gain.reshape(1, -1).astype(F32))
    for w in ws:
        in_specs.append(pl.BlockSpec((w.shape[0], tn), lambda i, j: (0, j)))
        args.append(w)
    if act:
        in_specs.append(pl.BlockSpec((1, tn), lambda i, j: (0, j)))
        args.append(bias.reshape(1, -1).astype(F32))
    if residual is not None:
        in_specs.append(pl.BlockSpec((tm, tn), lambda i, j: (i, j)))
        args.append(residual)
    body = functools.partial(_linear_body, n_x=n_x, use_norm=gain is not None, act=act,
                             has_res=residual is not None, transposed=seq_t is not None)
    if seq_t is None:
        out_spec = pl.BlockSpec((tm, tn), lambda i, j: (i, j))
        out_shape = jax.ShapeDtypeStruct((m, n), F32)
    else:
        per_seq = seq_t // tm
        out_spec = pl.BlockSpec((1, tn, tm), lambda i, j: (i // per_seq, j, i % per_seq))
        out_shape = jax.ShapeDtypeStruct((m // seq_t, n, seq_t), F32)
    return pl.pallas_call(
        body,
        grid=(m // tm, n // tn),
        in_specs=in_specs,
        out_specs=out_spec,
        out_shape=out_shape,
        scratch_shapes=[pltpu.VMEM((tm, x.shape[1]), BF16) for x in xs],
        compiler_params=_cparams("parallel", "arbitrary"),
        name=name,
    )(*args)


SUB = 8


def _conv_body(x_ref, prev_ref, buf_ref, w_ref, b_ref, o_ref):
    first = pl.program_id(1) == 0
    ext = jnp.concatenate([jnp.where(first, buf_ref[0], prev_ref[0]), x_ref[0]], axis=0)
    w = w_ref[...]

    def tap(j):
        back = CONV_K - 1 - j
        shifted = pltpu.roll(ext, back, axis=0) if back else ext
        return shifted[SUB:, :] * w[j:j + 1, :]

    y = tap(0)
    for j in range(1, CONV_K):
        y = y + tap(j)
    y = y + b_ref[...]
    o_ref[0] = _silu(y)


def conv_silu(x, buf, w, b):
    bsz, seq, ch = x.shape
    tt = _pick(seq, (512, 256, 128, 64, 32, 16, 8))
    tc = ch if tt <= 64 else _pick(ch, (1024, 640, 512, 256, 128))
    bufp = jnp.concatenate([jnp.zeros((bsz, SUB - (CONV_K - 1), ch), F32), buf], axis=1)
    wp = jnp.concatenate([w, jnp.zeros((SUB - CONV_K, ch), F32)], axis=0)
    nsub = tt // SUB
    return pl.pallas_call(
        _conv_body,
        grid=(bsz, seq // tt, ch // tc),
        in_specs=[
            pl.BlockSpec((1, tt, tc), lambda bi, ti, ci: (bi, ti, ci)),
            pl.BlockSpec((1, SUB, tc), lambda bi, ti, ci: (bi, jnp.maximum(ti * nsub - 1, 0), ci)),
            pl.BlockSpec((1, SUB, tc), lambda bi, ti, ci: (bi, 0, ci)),
            pl.BlockSpec((SUB, tc), lambda bi, ti, ci: (0, ci)),
            pl.BlockSpec((1, tc), lambda bi, ti, ci: (0, ci)),
        ],
        out_specs=pl.BlockSpec((1, tt, tc), lambda bi, ti, ci: (bi, ti, ci)),
        out_shape=jax.ShapeDtypeStruct((bsz, seq, ch), F32),
        compiler_params=_cparams("parallel", "parallel", "parallel"),
        name="conv_silu",
    )(x, x, bufp, wp, b.reshape(1, ch))


HG = SSD_HEADS // SSD_GROUPS
GW = HG * SSD_HEAD_DIM


def _expand_heads(v, e_bf16):
    hi = v.astype(BF16)
    r1 = v - hi.astype(F32)
    mid = r1.astype(BF16)
    lo = (r1 - mid.astype(F32)).astype(BF16)
    return _dot(hi, e_bf16) + _dot(mid, e_bf16) + _dot(lo, e_bf16)


def _ssd_body(xbc_ref, z_ref, small_ref, dtt_ref, h0_ref, prow_ref, pcol_ref, dexp_ref, norm_ref,
              y_ref, hout_ref, h_sc, y_sc, *, q):
    @pl.when(pl.program_id(1) == 0)
    def _():
        h_sc[...] = h0_ref[0]

    xbc = xbc_ref[0]
    xs = xbc[:, :SSD_INNER]
    bm = xbc[:, SSD_INNER:SSD_INNER + LANES]
    cm = xbc[:, SSD_INNER + LANES:SSD_INNER + 2 * LANES]
    dt_c = _softplus(small_ref[0][:, 0:SSD_HEADS] + prow_ref[0:1, 0:SSD_HEADS])
    a_c = dt_c * (-jnp.exp(prow_ref[1:2, 0:SSD_HEADS]))
    dt_t = _softplus(dtt_ref[0] + pcol_ref[:, 0:1])
    a_t = dt_t * (-jnp.exp(pcol_ref[:, 1:2]))
    ri = _iota((q, q), 0)
    ci = _iota((q, q), 1)
    causal = ci <= ri
    cum = _dot_hi(causal.astype(F32), a_c)
    cum_t = _dot_hi(a_t, (ri <= ci).astype(F32))
    e_heads = (_iota((SSD_HEADS, SSD_INNER), 1) // SSD_HEAD_DIM == _iota((SSD_HEADS, SSD_INNER), 0)).astype(BF16)
    dt_x = _expand_heads(dt_c, e_heads)
    ecum_x = _expand_heads(jnp.exp(cum), e_heads)
    wlast_x = _expand_heads(jnp.exp(cum[q - 1:q, :] - cum), e_heads)
    xdt = xs * dt_x
    xw = (xdt * wlast_x).astype(BF16)
    xdt_b = xdt.astype(BF16)

    lane = _iota((1, LANES), 1)
    low = lane < SSD_STATE
    bm_b = bm.astype(BF16)
    hs = h_sc[...]
    hs_b = hs.astype(BF16)
    y_inter = []
    cbs = []
    upd = []
    for g in range(SSD_GROUPS):
        cm_g = jnp.where(low if g == 0 else jnp.logical_not(low), cm, 0.0).astype(BF16)
        cbs.append(_dot_nt(cm_g, bm_b))
        y_inter.append(_dot(cm_g, hs_b))
        upd.append(_dot_tn(bm_b, xw[:, g * GW:(g + 1) * GW]))
    for j in range(SSD_HEADS // 2):
        g = (2 * j) // HG
        xp = xdt_b[:, j * LANES:(j + 1) * LANES]
        ys = []
        for hh in (2 * j, 2 * j + 1):
            seg = cum[:, hh:hh + 1] - cum_t[hh:hh + 1, :]
            lm = jnp.exp(jnp.where(causal, seg, -jnp.inf))
            ys.append(_dot((cbs[g] * lm).astype(BF16), xp))
        y_sc[:, j * LANES:(j + 1) * LANES] = jnp.where(low, ys[0], ys[1])
    y = y_sc[...] + jnp.concatenate(y_inter, axis=1) * ecum_x + dexp_ref[...] * xs
    y = y * _silu(z_ref[0])
    y = jnp.concatenate([_rms_rows(y[:, g * GW:(g + 1) * GW]) for g in range(SSD_GROUPS)], axis=1)
    y_ref[0] = y * norm_ref[...]

    row_low = _iota((2 * SSD_STATE, 1), 0) < SSD_STATE
    ecl = ecum_x[q - 1:q, :]
    decay = jnp.where(row_low, ecl[:, 0:GW], ecl[:, GW:2 * GW])
    h_new = hs * decay + jnp.where(row_low, upd[0], upd[1])
    h_sc[...] = h_new

    @pl.when(pl.program_id(1) == pl.num_programs(1) - 1)
    def _():
        hout_ref[0] = h_new


def ssd_scan(xbc_c, z, small, h0, dt_bias, a_log, d_skip, norm_w):
    bsz, seq, _ = xbc_c.shape
    q = _pick(seq, (128, 64, 32, 16, 8))
    dtt = jnp.swapaxes(small[:, :, 0:SSD_HEADS], 1, 2)
    hs0 = h0.reshape(bsz, SSD_GROUPS, HG, SSD_HEAD_DIM, SSD_STATE).transpose(0, 1, 4, 2, 3)
    hs0 = hs0.reshape(bsz, SSD_GROUPS * SSD_STATE, GW)
    prow = jnp.zeros((SUB, LANES), F32).at[0, :SSD_HEADS].set(dt_bias).at[1, :SSD_HEADS].set(a_log)
    pcol = jnp.zeros((SSD_HEADS, LANES), F32).at[:, 0].set(dt_bias).at[:, 1].set(a_log)
    dexp = jnp.repeat(d_skip, SSD_HEAD_DIM).reshape(1, SSD_INNER)
    y, hs = pl.pallas_call(
        functools.partial(_ssd_body, q=q),
        grid=(bsz, seq // q),
        in_specs=[
            pl.BlockSpec((1, q, SSD_CONV_DIM), lambda b, c: (b, c, 0)),
            pl.BlockSpec((1, q, SSD_INNER), lambda b, c: (b, c, 0)),
            pl.BlockSpec((1, q, LANES), lambda b, c: (b, c, 0)),
            pl.BlockSpec((1, SSD_HEADS, q), lambda b, c: (b, 0, c)),
            pl.BlockSpec((1, 2 * SSD_STATE, GW), lambda b, c: (b, 0, 0)),
            pl.BlockSpec((SUB, LANES), lambda b, c: (0, 0)),
            pl.BlockSpec((SSD_HEADS, LANES), lambda b, c: (0, 0)),
            pl.BlockSpec((1, SSD_INNER), lambda b, c: (0, 0)),
            pl.BlockSpec((1, SSD_INNER), lambda b, c: (0, 0)),
        ],
        out_specs=[
            pl.BlockSpec((1, q, SSD_INNER), lambda b, c: (b, c, 0)),
            pl.BlockSpec((1, 2 * SSD_STATE, GW), lambda b, c: (b, 0, 0)),
        ],
        out_shape=[
            jax.ShapeDtypeStruct((bsz, seq, SSD_INNER), F32),
            jax.ShapeDtypeStruct((bsz, 2 * SSD_STATE, GW), F32),
        ],
        scratch_shapes=[pltpu.VMEM((2 * SSD_STATE, GW), F32), pltpu.VMEM((q, SSD_INNER), F32)],
        compiler_params=_cparams("parallel", "arbitrary"),
        name="ssd_scan",
    )(xbc_c, z, small, dtt, hs0, prow, pcol, dexp, norm_w.reshape(1, SSD_INNER))
    h_new = hs.reshape(bsz, SSD_GROUPS, SSD_STATE, HG, SSD_HEAD_DIM).transpose(0, 1, 3, 4, 2)
    return y, h_new.reshape(bsz, SSD_HEADS, SSD_HEAD_DIM, SSD_STATE)


def _split2(x):
    hi = x.astype(BF16)
    return hi, (x - hi.astype(F32)).astype(BF16)


def _dot3(a, b):
    ah, al = _split2(a)
    bh, bl = _split2(b)
    return _dot(ah, bh) + (_dot(ah, bl) + _dot(al, bh))


def _gdn_prep_body(q_ref, k_ref, v_ref, small_ref, gbt_ref, prow_ref,
                   u_ref, w_ref, qg_ref, kd_ref, qk_ref, eg_ref, *, c, nch, hp):
    lane = _iota((1, LANES), 1)

    def pick_lane(x, idx):
        return jnp.sum(jnp.where(lane == idx, x, 0.0), axis=1, keepdims=True)

    small = small_ref[0]
    beta_col, g_col, g_row = [], [], []
    for i in range(hp):
        h = pl.program_id(1) * hp + i
        neg_a = -jnp.exp(pick_lane(prow_ref[0:1, :], h))
        dtb = pick_lane(prow_ref[1:2, :], h)
        beta_col.append(_sigmoid(pick_lane(small, SSD_HEADS + h)))
        g_col.append(neg_a * _softplus(pick_lane(small, SSD_HEADS + GDN_HEADS + h) + dtb))
        g_row.append(neg_a * _softplus(gbt_ref[0, pl.ds(GDN_HEADS + h, 1), :] + dtb))

    ri = _iota((c, c), 0)
    ci = _iota((c, c), 1)
    tril = ri >= ci
    tril_f = tril.astype(F32)
    triu_f = (ri <= ci).astype(F32)
    eye = (ri == ci).astype(F32)
    units = [(i, ch) for i in range(hp) for ch in range(nch)]
    rows = [slice(ch * c, (ch + 1) * c) for _, ch in units]
    lanes = [slice(i * GDN_DK, (i + 1) * GDN_DK) for i, _ in units]
    nu = range(len(units))
    qn, kn, kb, gam, dec, nmat = [], [], [], [], [], []
    for n in nu:
        i = units[n][0]
        qc = q_ref[0, rows[n], lanes[n]]
        kc = k_ref[0, rows[n], lanes[n]]
        qn.append(qc * lax.rsqrt(jnp.sum(qc * qc, axis=-1, keepdims=True) + 1e-6) * (GDN_DK ** -0.5))
        kn.append(kc * lax.rsqrt(jnp.sum(kc * kc, axis=-1, keepdims=True) + 1e-6))
        kb.append(kn[n] * beta_col[i][rows[n], :])
        gam.append(_dot_hi(tril_f, jnp.broadcast_to(g_col[i][rows[n], :], (c, LANES))))
    for n in nu:
        gam_r = _dot_hi(jnp.broadcast_to(g_row[units[n][0]][:, rows[n]], (SUB, c)), triu_f)[0:1, :]
        dec.append(jnp.exp(jnp.where(tril, gam[n][:, 0:c] - gam_r, -jnp.inf)))
        nmat.append(jnp.where(ri > ci, _dot_nt(kb[n].astype(BF16), kn[n].astype(BF16)) * dec[n], 0.0) * -1.0)
    tinv = [eye + m for m in nmat]
    npow = [_dot3(m, m) for m in nmat]
    levels = int(math.log2(c))
    for lvl in range(1, levels):
        for n in nu:
            if lvl == levels - 1:
                tinv[n] = tinv[n] + _dot3(npow[n], tinv[n])
            else:
                both = _dot3(npow[n], jnp.concatenate([tinv[n], npow[n]], axis=1))
                tinv[n] = tinv[n] + both[:, :c]
                npow[n] = both[:, c:]
    for n in nu:
        i, ch = units[n]
        egam = jnp.exp(gam[n])
        vb = (v_ref[0, rows[n], lanes[n]] * beta_col[i][rows[n], :]).astype(BF16)
        uw = _dot(tinv[n].astype(BF16), jnp.concatenate([vb, (kb[n] * egam).astype(BF16)], axis=1))
        glast = gam[n][c - 1:c, :]
        u_ref[0, rows[n], lanes[n]] = uw[:, :GDN_DV]
        w_ref[0, rows[n], lanes[n]] = uw[:, GDN_DV:].astype(w_ref.dtype)
        qg_ref[0, rows[n], lanes[n]] = (qn[n] * egam).astype(qg_ref.dtype)
        kd_ref[0, rows[n], lanes[n]] = (kn[n] * jnp.exp(glast - gam[n])).astype(kd_ref.dtype)
        qk_ref[0, i, rows[n], :] = (_dot_nt(qn[n].astype(BF16), kn[n].astype(BF16)) * dec[n]).astype(qk_ref.dtype)
        eg_ref[0, i, ch:ch + 1, :] = jnp.exp(glast)


GDN_GROUP = 8


def _gdn_scan_body(u_ref, w_ref, qg_ref, kd_ref, qk_ref, eg_ref, gate_ref, s0_ref, norm_ref,
                   o_ref, sout_ref, s_sc, *, c, nch):
    @pl.when(pl.program_id(2) == 0)
    def _():
        s_sc[...] = s0_ref[0]

    s = [s_sc[i] for i in range(GDN_GROUP)]
    heads = range(GDN_GROUP)
    lanes = [slice(i * GDN_DV, (i + 1) * GDN_DV) for i in heads]
    for ch in range(nch):
        rows = slice(ch * c, (ch + 1) * c)
        s_b = [s[i].astype(BF16) for i in heads]
        ws = [_dot(jnp.concatenate([w_ref[0, rows, lanes[i]].astype(BF16), qg_ref[0, rows, lanes[i]].astype(BF16)],
                                   axis=0), s_b[i]) for i in heads]
        vn_b = [(u_ref[0, rows, lanes[i]] - ws[i][:c]).astype(BF16) for i in heads]
        o = [ws[i][c:] + _dot(qk_ref[0, i, rows, :].astype(BF16), vn_b[i]) for i in heads]
        s = [s[i] * eg_ref[0, i, ch:ch + 1, :] + _dot_tn(kd_ref[0, rows, lanes[i]].astype(BF16), vn_b[i]) for i in heads]
        for i in heads:
            o_ref[0, rows, lanes[i]] = _rms_rows(o[i]) * norm_ref[...] * _silu(gate_ref[0, rows, lanes[i]])
    for i in range(GDN_GROUP):
        s_sc[i] = s[i]

    @pl.when(pl.program_id(2) == pl.num_programs(2) - 1)
    def _():
        for i in range(GDN_GROUP):
            sout_ref[0, i] = s[i]


def gdn_scan(qkv_c, gate, small, s0, dt_bias, a_log, norm_w):
    bsz, seq, _ = qkv_c.shape
    c = _pick(seq, (64, 32, 16, 8))
    tb = _pick(seq, (512, 256, 128, 64, 32, 16, 8))
    nch = tb // c
    wdt = BF16 if c % 16 == 0 else F32
    gbt = jnp.swapaxes(small[:, :, SSD_HEADS:SSD_HEADS + 2 * GDN_HEADS], 1, 2)
    prow = jnp.zeros((SUB, LANES), F32).at[0, :GDN_HEADS].set(a_log).at[1, :GDN_HEADS].set(dt_bias)
    hp = 2 if nch >= 4 else GDN_HEADS
    ng = GDN_HEADS // hp
    head_blk = pl.BlockSpec((1, tb, hp * GDN_DV), lambda b, h, t: (b, t, h))
    u, w, qg, kd, qk, eg = pl.pallas_call(
        functools.partial(_gdn_prep_body, c=c, nch=nch, hp=hp),
        grid=(bsz, ng, seq // tb),
        in_specs=[
            pl.BlockSpec((1, tb, hp * GDN_DK), lambda b, h, t: (b, t, h)),
            pl.BlockSpec((1, tb, hp * GDN_DK), lambda b, h, t: (b, t, ng + h)),
            pl.BlockSpec((1, tb, hp * GDN_DV), lambda b, h, t: (b, t, 2 * ng + h)),
            pl.BlockSpec((1, tb, LANES), lambda b, h, t: (b, t, 0)),
            pl.BlockSpec((1, 2 * GDN_HEADS, tb), lambda b, h, t: (b, 0, t)),
            pl.BlockSpec((SUB, LANES), lambda b, h, t: (0, 0)),
        ],
        out_specs=[head_blk, head_blk, head_blk, head_blk,
                   pl.BlockSpec((1, hp, tb, c), lambda b, h, t: (b, h, t, 0)),
                   pl.BlockSpec((1, hp, nch, LANES), lambda b, h, t: (b, h, t, 0))],
        out_shape=[
            jax.ShapeDtypeStruct((bsz, seq, GDN_VW), F32),
            jax.ShapeDtypeStruct((bsz, seq, GDN_VW), wdt),
            jax.ShapeDtypeStruct((bsz, seq, GDN_QK), wdt),
            jax.ShapeDtypeStruct((bsz, seq, GDN_QK), wdt),
            jax.ShapeDtypeStruct((bsz, GDN_HEADS, seq, c), wdt),
            jax.ShapeDtypeStruct((bsz, GDN_HEADS, seq // c, LANES), F32),
        ],
        compiler_params=_cparams("parallel", "parallel", "parallel"),
        name="gdn_prep",
    )(qkv_c, qkv_c, qkv_c, small, gbt, prow)
    gw = GDN_GROUP * GDN_DV
    grp_blk = pl.BlockSpec((1, tb, gw), lambda b, g, t: (b, t, g))
    state_blk = pl.BlockSpec((1, GDN_GROUP, GDN_DK, GDN_DV), lambda b, g, t: (b, g, 0, 0))
    o, s_new = pl.pallas_call(
        functools.partial(_gdn_scan_body, c=c, nch=nch),
        grid=(bsz, GDN_HEADS // GDN_GROUP, seq // tb),
        in_specs=[
            grp_blk, grp_blk, grp_blk, grp_blk,
            pl.BlockSpec((1, GDN_GROUP, tb, c), lambda b, g, t: (b, g, t, 0)),
            pl.BlockSpec((1, GDN_GROUP, nch, LANES), lambda b, g, t: (b, g, t, 0)),
            grp_blk,
            state_blk,
            pl.BlockSpec((1, GDN_DV), lambda b, g, t: (0, 0)),
        ],
        out_specs=[grp_blk, state_blk],
        out_shape=[
            jax.ShapeDtypeStruct((bsz, seq, GDN_VW), F32),
            jax.ShapeDtypeStruct((bsz, GDN_HEADS, GDN_DK, GDN_DV), F32),
        ],
        scratch_shapes=[pltpu.VMEM((GDN_GROUP, GDN_DK, GDN_DV), F32)],
        compiler_params=_cparams("parallel", "parallel", "arbitrary"),
        name="gdn_scan",
    )(u, w, qg, kd, qk, eg, gate, s0, norm_w.reshape(1, GDN_DV))
    return o, s_new


def _xattn_body(h_ref, g_ref, wq_ref, mk_ref, mv_ref, wo_ref, o_ref, *, nb, tm):
    d = h_ref.shape[-1]
    x = h_ref[...].reshape(nb * tm, d)
    xn = (_rms_rows(x) * g_ref[...]).astype(BF16)
    qf = _dot(xn, wq_ref[...])
    outs = []
    for bi in range(nb):
        mk = mk_ref[bi].astype(BF16)
        mv = mv_ref[bi].astype(BF16)
        heads = []
        for hd in range(X_HEADS):
            sl = slice(hd * X_HEAD_DIM, (hd + 1) * X_HEAD_DIM)
            s = _dot_nt(qf[bi * tm:(bi + 1) * tm, sl].astype(BF16), mk[:, sl]) * (X_HEAD_DIM ** -0.5)
            p = jnp.exp(s - jnp.max(s, axis=-1, keepdims=True))
            p = p / jnp.sum(p, axis=-1, keepdims=True)
            heads.append(_dot(p.astype(BF16), mv[:, sl]))
        outs.append(jnp.concatenate(heads, axis=1))
    o = jnp.concatenate(outs, axis=0).astype(BF16)
    o_ref[...] = (x + _dot(o, wo_ref[...])).reshape(nb, tm, d)


def cross_attn(h, gain, wq, mk, mv, wo):
    bsz, seq, d = h.shape
    tm = _pick(seq, (512, 256, 128, 64, 32, 16, 8))
    nb = _pick(bsz, (8, 4, 2, 1)) if seq <= 64 else 1
    return pl.pallas_call(
        functools.partial(_xattn_body, nb=nb, tm=tm),
        grid=(bsz // nb, seq // tm),
        in_specs=[
            pl.BlockSpec((nb, tm, d), lambda b, i: (b, i, 0)),
            pl.BlockSpec((1, d), lambda b, i: (0, 0)),
            pl.BlockSpec((d, X_W), lambda b, i: (0, 0)),
            pl.BlockSpec((nb, MEM_LEN, X_W), lambda b, i: (b, 0, 0)),
            pl.BlockSpec((nb, MEM_LEN, X_W), lambda b, i: (b, 0, 0)),
            pl.BlockSpec((X_W, d), lambda b, i: (0, 0)),
        ],
        out_specs=pl.BlockSpec((nb, tm, d), lambda b, i: (b, i, 0)),
        out_shape=jax.ShapeDtypeStruct((bsz, seq, d), F32),
        compiler_params=_cparams("parallel", "parallel"),
        name="cross_attn",
    )(h, gain.reshape(1, d), wq, mk, mv, wo)


def _swiglu_body(h_ref, g_ref, w1_ref, w3_ref, w2_ref, *rest, final_norm):
    gf_ref = rest[0] if final_norm else None
    o_ref, xn_sc = rest[-2:]
    j = pl.program_id(1)

    @pl.when(j == 0)
    def _():
        x = h_ref[...]
        xn_sc[...] = (_rms_rows(x) * g_ref[...]).astype(BF16)
        o_ref[...] = x

    xn = xn_sc[...]
    a = _dot(xn, w1_ref[0])
    b = _dot(xn, w3_ref[0])
    o_ref[...] += _dot((_silu(a) * b).astype(BF16), w2_ref[0])

    if final_norm:
        @pl.when(j == pl.num_programs(1) - 1)
        def _():
            o_ref[...] = _rms_rows(o_ref[...]) * gf_ref[...]


def swiglu(h, gain, w1, w3, w2, layer, final_gain=None):
    m, d = h.shape
    ff = w1.shape[2]
    tm = _pick(m, (512, 256, 128, 64, 32, 16, 8))
    tf = _pick(ff, (1408, 1024, 512, 256, 128))
    in_specs = [
        pl.BlockSpec((tm, d), lambda i, j: (i, 0)),
        pl.BlockSpec((1, d), lambda i, j: (0, 0)),
        pl.BlockSpec((1, d, tf), lambda i, j: (layer, 0, j)),
        pl.BlockSpec((1, d, tf), lambda i, j: (layer, 0, j)),
        pl.BlockSpec((1, tf, d), lambda i, j: (layer, j, 0)),
    ]
    args = [h, gain.reshape(1, d), w1, w3, w2]
    if final_gain is not None:
        in_specs.append(pl.BlockSpec((1, d), lambda i, j: (0, 0)))
        args.append(final_gain.reshape(1, d))
    return pl.pallas_call(
        functools.partial(_swiglu_body, final_norm=final_gain is not None),
        grid=(m // tm, ff // tf),
        in_specs=in_specs,
        out_specs=pl.BlockSpec((tm, d), lambda i, j: (i, 0)),
        out_shape=jax.ShapeDtypeStruct((m, d), F32),
        scratch_shapes=[pltpu.VMEM((tm, d), BF16)],
        compiler_params=_cparams("parallel", "arbitrary"),
        name="swiglu",
    )(*args)


def _cumsum_lanes_body(x_ref, o_ref, carry_sc, *, tc):
    @pl.when(pl.program_id(1) == 0)
    def _():
        carry_sc[...] = jnp.zeros_like(carry_sc)

    upper = (_iota((tc, tc), 0) <= _iota((tc, tc), 1)).astype(F32)
    f = _dot_hi(x_ref[0], upper) + carry_sc[:, 0:1]
    o_ref[0] = f
    carry_sc[...] = jnp.broadcast_to(f[:, tc - 1:tc], carry_sc.shape)


def cumsum_lanes(x):
    bsz, r, seq = x.shape
    tc = _pick(seq, (512, 256, 128))
    return pl.pallas_call(
        functools.partial(_cumsum_lanes_body, tc=tc),
        grid=(bsz, seq // tc),
        in_specs=[pl.BlockSpec((1, r, tc), lambda b, i: (b, 0, i))],
        out_specs=pl.BlockSpec((1, r, tc), lambda b, i: (b, 0, i)),
        out_shape=jax.ShapeDtypeStruct((bsz, r, seq), F32),
        scratch_shapes=[pltpu.VMEM((r, LANES), F32)],
        compiler_params=_cparams("parallel", "arbitrary"),
        name="cumsum_lanes",
    )(x)


FOX_ROWS = 512
FOX_PAIRS = 2


LOG2E = 1.4426950408889634


def _fox_body(qi_ref, ki_ref, q_ref, k_ref, v_ref, f_ref, o_ref, m_sc, acc_sc, *, t):
    step = pl.program_id(2)
    qi = qi_ref[step]
    ki = ki_ref[step]
    low = _iota((1, LANES), 1) < FOX_HEAD_DIM
    rb = min(FOX_ROWS, t)

    @pl.when(ki == 0)
    def _():
        m_sc[...] = jnp.full(m_sc.shape, -jnp.inf, F32)
        acc_sc[...] = jnp.zeros_like(acc_sc)

    def update(masked):
        units = [(pr, r, hh) for pr in range(FOX_PAIRS) for r in range(t // rb) for hh in range(2)]
        kbs, vbs = {}, {}

        def nkeys(r):
            return (r + 1) * rb if masked else t

        def qk(u):
            pr, r, hh = u
            lanes = slice(pr * LANES, (pr + 1) * LANES)
            if pr not in kbs:
                kbs[pr] = k_ref[0, lanes, :].astype(BF16)
                v = v_ref[0, lanes, :]
                first = _iota((LANES, 1), 0) < FOX_HEAD_DIM
                vbs[pr] = (jnp.where(first, v, 1.0).astype(BF16), jnp.where(first, 1.0, v).astype(BF16))
            qs = q_ref[0, r * rb:(r + 1) * rb, lanes] * (FOX_HEAD_DIM ** -0.5 * LOG2E)
            qm = jnp.where(low if hh == 0 else jnp.logical_not(low), qs, 0.0).astype(BF16)
            return _dot(qm, kbs[pr][:, :nkeys(r)])

        s_next = qk(units[0])
        for i, (pr, r, hh) in enumerate(units):
            rows = slice(r * rb, (r + 1) * rb)
            nk = nkeys(r)
            s = s_next - f_ref[0, pr, hh:hh + 1, :nk] * LOG2E
            if i + 1 < len(units):
                s_next = qk(units[i + 1])
            if masked:
                s = jnp.where(_iota((rb, nk), 1) <= _iota((rb, nk), 0) + r * rb, s, -jnp.inf)
            m_prev = m_sc[pr, hh, rows, :]
            m_new = jnp.maximum(m_prev, jnp.max(s, axis=-1, keepdims=True))
            alpha = jnp.exp2(m_prev - m_new)
            p = jnp.exp2(s - jnp.concatenate([m_new] * (nk // LANES), axis=1))
            m_sc[pr, hh, rows, :] = m_new
            acc_sc[pr, hh, rows, :] = alpha * acc_sc[pr, hh, rows, :] + _dot_nt(p.astype(BF16), vbs[pr][hh][:, :nk])

    @pl.when(ki < qi)
    def _():
        update(False)

    @pl.when(ki == qi)
    def _():
        update(True)
        for pr in range(FOX_PAIRS):
            outs = [acc_sc[pr, hh] / pltpu.roll(acc_sc[pr, hh], FOX_HEAD_DIM, axis=1) for hh in range(2)]
            o_ref[0, :, pr * LANES:(pr + 1) * LANES] = jnp.where(low, outs[0], outs[1])


def fox_prompt_attn(q, k, v, ft):
    bsz, seq, _ = q.shape
    t = _pick(seq, (1024, 512, 256, 128))
    nb = seq // t
    pairs =[(i, j) for i in range(nb) for j in range(i + 1)]
    qi = jnp.asarray([p[0] for p in pairs], jnp.int32)
    ki = jnp.asarray([p[1] for p in pairs], jnp.int32)
    npair = FOX_HEADS // 2
    w = FOX_PAIRS * LANES
    ft4 = ft.reshape(bsz, npair, 2, seq)
    grid_spec = pltpu.PrefetchScalarGridSpec(
        num_scalar_prefetch=2,
        grid=(bsz, npair // FOX_PAIRS, len(pairs)),
        in_specs=[
            pl.BlockSpec((1, t, w), lambda b, j, s, qi, ki: (b, qi[s], j)),
            pl.BlockSpec((1, w, t), lambda b, j, s, qi, ki: (b, j, ki[s])),
            pl.BlockSpec((1, w, t), lambda b, j, s, qi, ki: (b, j, ki[s])),
            pl.BlockSpec((1, FOX_PAIRS, 2, t), lambda b, j, s, qi, ki: (b, j, 0, ki[s])),
        ],
        out_specs=pl.BlockSpec((1, t, w), lambda b, j, s, qi, ki: (b, qi[s], j)),
        scratch_shapes=[pltpu.VMEM((FOX_PAIRS, 2, t, LANES), F32)] * 2,
    )
    return pl.pallas_call(
        functools.partial(_fox_body, t=t),
        grid_spec=grid_spec,
        out_shape=jax.ShapeDtypeStruct((bsz, seq, FOX_W), F32),
        compiler_params=_cparams("parallel", "parallel", "arbitrary"),
        name="fox_prompt_attn",
    )(qi, ki, q, k, v, ft4)


def _page_suffix_body(x_ref, rin_ref, tot_ref):
    x = x_ref[...]
    after = (_iota((PAGE_SIZE, PAGE_SIZE), 0) > _iota((PAGE_SIZE, PAGE_SIZE), 1)).astype(F32)
    rin_ref[...] = _dot_hi(x, after)
    tot_ref[...] = jnp.broadcast_to(jnp.sum(x, axis=1, keepdims=True), x.shape)


def page_suffix(lft):
    r = lft.shape[0]
    tr = _pick(r, (2048, 1024, 512, 256, 128, 64, 32, 16))
    spec = pl.BlockSpec((tr, PAGE_SIZE), lambda i: (i, 0))
    return pl.pallas_call(
        _page_suffix_body,
        grid=(r // tr,),
        in_specs=[spec],
        out_specs=[spec, spec],
        out_shape=[jax.ShapeDtypeStruct(lft.shape, F32)] * 2,
        compiler_params=_cparams("parallel"),
        name="page_suffix",
    )(lft)


DEC_PAGES = 8


def _rep_rows(x, n):
    r, c = x.shape
    return jnp.broadcast_to(x[:, None, :], (r, n, c)).reshape(r * n, c)


def _fox_decode_body(pt_ref, q_ref, *refs, nt, nsteps):
    k_refs = refs[0:DEC_PAGES]
    v_refs = refs[DEC_PAGES:2 * DEC_PAGES]
    rin_refs = refs[2 * DEC_PAGES:3 * DEC_PAGES]
    tot_refs = refs[3 * DEC_PAGES:4 * DEC_PAGES]
    kn_ref, vn_ref, lfn_ref, o_ref, qbd_sc, m_sc, l_sc, acc_sc, run_sc = refs[4 * DEC_PAGES:]
    j = pl.program_id(1)
    rows = FOX_HEADS * nt

    @pl.when(j == 0)
    def _():
        qt = jnp.concatenate([q_ref[0] * (FOX_HEAD_DIM ** -0.5)] * FOX_HEADS, axis=0)
        own = _iota((rows, FOX_W), 1) // FOX_HEAD_DIM == _iota((rows, FOX_W), 0) // nt
        qbd_sc[...] = jnp.where(own, qt, 0.0).astype(BF16)
        m_sc[...] = jnp.full(m_sc.shape, -jnp.inf, F32)
        l_sc[...] = jnp.zeros_like(l_sc)
        acc_sc[...] = jnp.zeros_like(acc_sc)
        run_sc[...] = jnp.zeros_like(run_sc)

    def update(ss, vals, transposed):
        m_prev = m_sc[...]
        m_new = m_prev
        for s in ss:
            m_new = jnp.maximum(m_new, jnp.max(s, axis=-1, keepdims=True))
        alpha = jnp.exp(m_prev - m_new)
        l_new = alpha * l_sc[...]
        acc = jnp.concatenate([alpha] * (FOX_W // LANES), axis=1) * acc_sc[...]
        for s, val in zip(ss, vals):
            p = jnp.exp(s - m_new)
            l_new = l_new + jnp.sum(p, axis=-1, keepdims=True)
            acc = acc + (_dot_nt(p.astype(BF16), val) if transposed else _dot(p.astype(BF16), val))
        m_sc[...] = m_new
        l_sc[...] = l_new
        acc_sc[...] = acc

    @pl.when(j < nsteps)
    def _():
        qbd = qbd_sc[...]
        raw = [_dot(qbd, k_refs[i][0, 0].reshape(FOX_W, PAGE_SIZE).astype(BF16)) for i in range(DEC_PAGES)]
        run = run_sc[...]
        ss = []
        for i in range(DEC_PAGES):
            ss.append(raw[i] + _rep_rows(rin_refs[i][0] + run, nt))
            run = run + tot_refs[i][0]
        run_sc[...] = run
        update(ss, [v_refs[i][0, 0].reshape(FOX_W, PAGE_SIZE).astype(BF16) for i in range(DEC_PAGES)], True)

    @pl.when(j == nsteps)
    def _():
        pad = jnp.zeros((PAGE_SIZE - nt, FOX_W), F32)
        kb = jnp.concatenate([kn_ref[0], pad], axis=0).astype(BF16)
        vb = jnp.concatenate([vn_ref[0], pad], axis=0).astype(BF16)
        s = _dot_nt(qbd_sc[...], kb)
        incl = (_iota((PAGE_SIZE, PAGE_SIZE), 0) <= _iota((PAGE_SIZE, PAGE_SIZE), 1)).astype(F32)
        fn = _dot_hi(lfn_ref[0], incl)
        tok = _iota((rows, PAGE_SIZE), 0) % nt
        update([jnp.where(_iota((rows, PAGE_SIZE), 1) <= tok, s - _rep_rows(fn, nt), -jnp.inf)], [vb], False)
        o = acc_sc[...] / jnp.concatenate([l_sc[...]] * (FOX_W // LANES), axis=1)
        lane_head = _iota((nt, FOX_W), 1) // FOX_HEAD_DIM
        out = jnp.zeros((nt, FOX_W), F32)
        for hd in range(FOX_HEADS):
            out = out + jnp.where(lane_head == hd, o[hd * nt:(hd + 1) * nt, :], 0.0)
        o_ref[0] = out


def fox_decode_attn(q, k_new, v_new, lfn_t, k_cache_t, v_cache_t, layer, rin, tot, page_table):
    bsz, nt, _ = q.shape
    npages = page_table.shape[1]
    assert npages % DEC_PAGES == 0
    nsteps = npages // DEC_PAGES
    rows = FOX_HEADS * nt

    def page(i):
        return lambda b, j, pt: pt[b, npages - 1 - (jnp.minimum(j, nsteps - 1) * DEC_PAGES + i)]

    cache_specs = [pl.BlockSpec((1, 1, FOX_HEADS, FOX_HEAD_DIM, PAGE_SIZE),
                                functools.partial(lambda b, j, pt, pg: (layer, pg(b, j, pt), 0, 0, 0), pg=page(i)))
                   for i in range(DEC_PAGES)]
    bias_specs = [pl.BlockSpec((1, FOX_HEADS, PAGE_SIZE), functools.partial(lambda b, j, pt, pg: (pg(b, j, pt), 0, 0), pg=page(i)))
                  for i in range(DEC_PAGES)]
    per_seq = lambda shape: pl.BlockSpec(shape, lambda b, j, pt: (b, 0, 0))
    grid_spec = pltpu.PrefetchScalarGridSpec(
        num_scalar_prefetch=1,
        grid=(bsz, nsteps + 1),
        in_specs=[per_seq((1, nt, FOX_W))] + cache_specs + cache_specs + bias_specs + bias_specs
        + [per_seq((1, nt, FOX_W)), per_seq((1, nt, FOX_W)), per_seq((1, FOX_HEADS, PAGE_SIZE))],
        out_specs=per_seq((1, nt, FOX_W)),
        scratch_shapes=[
            pltpu.VMEM((rows, FOX_W), BF16),
            pltpu.VMEM((rows, LANES), F32),
            pltpu.VMEM((rows, LANES), F32),
            pltpu.VMEM((rows, FOX_W), F32),
            pltpu.VMEM((FOX_HEADS, PAGE_SIZE), F32),
        ],
    )
    return pl.pallas_call(
        functools.partial(_fox_decode_body, nt=nt, nsteps=nsteps),
        grid_spec=grid_spec,
        out_shape=jax.ShapeDtypeStruct((bsz, nt, FOX_W), F32),
        compiler_params=_cparams("parallel", "arbitrary"),
        name="fox_decode_attn",
    )(page_table, q, *([k_cache_t] * DEC_PAGES), *([v_cache_t] * DEC_PAGES), *([rin] * DEC_PAGES),
      *([tot] * DEC_PAGES), k_new, v_new, lfn_t)


def _split_hyb_weights(w_in):
    o = 0
    parts = {}
    for name, width in (("z", SSD_INNER), ("xbc", SSD_CONV_DIM), ("dt", SSD_HEADS), ("qkv", GDN_CONV_DIM),
                        ("gate", GDN_VW), ("b", GDN_HEADS), ("a", GDN_HEADS)):
        parts[name] = w_in[:, o:o + width]
        o += width
    small = jnp.concatenate([parts["dt"], parts["b"], parts["a"]], axis=1)
    small = jnp.pad(small, ((0, 0), (0, LANES - small.shape[1])))
    return {k: parts[k].astype(BF16) for k in ("z", "xbc", "qkv", "gate")} | {"small": small.astype(BF16)}


def _hybrid_layer(h, e, W, st):
    bsz, seq, d = h.shape
    m = bsz * seq
    h2 = h.reshape(m, d)
    wp = _split_hyb_weights(W["w_in_hyb"][e])
    gain = W["norm_mix"][2 * e]
    proj = {k: fused_linear([h2], [wp[k]], gain=gain, name="hyb_in_" + k) for k in ("z", "xbc", "qkv", "gate", "small")}
    xbc = proj["xbc"].reshape(bsz, seq, SSD_CONV_DIM)
    qkv = proj["qkv"].reshape(bsz, seq, GDN_CONV_DIM)
    small = proj["small"].reshape(bsz, seq, LANES)
    xbc_c = conv_silu(xbc, st["ssd_conv"], W["ssd_conv_w"][e], W["ssd_conv_b"][e])
    qkv_c = conv_silu(qkv, st["gdn_conv"], W["gdn_conv_w"][e], jnp.zeros((GDN_CONV_DIM,), F32))
    y, ssd_h = ssd_scan(xbc_c, proj["z"].reshape(bsz, seq, SSD_INNER), small, st["ssd"],
                        W["ssd_dt_bias"][e], W["ssd_A_log"][e], W["ssd_D"][e], W["ssd_norm"][e])
    o, gdn_s = gdn_scan(qkv_c, proj["gate"].reshape(bsz, seq, GDN_VW), small, st["gdn"],
                        W["gdn_dt_bias"][e], W["gdn_A_log"][e], W["gdn_norm"][e])
    w_out = W["w_out_hyb"][e].astype(BF16)
    h_new = fused_linear([y.reshape(m, SSD_INNER), o.reshape(m, GDN_VW)], [w_out[:SSD_INNER], w_out[SSD_INNER:]],
                         residual=h2, name="hyb_out")
    new = dict(ssd=ssd_h, ssd_conv=xbc[:, seq - (CONV_K - 1):], gdn=gdn_s, gdn_conv=qkv[:, seq - (CONV_K - 1):])
    return h_new.reshape(bsz, seq, d), new


def _fox_layer(h, o_idx, layer, W, st, prompt):
    bsz, seq, d = h.shape
    m = bsz * seq
    h2 = h.reshape(m, d)
    w_in = W["w_in_fox"][o_idx]
    gain = W["norm_mix"][layer]
    wq, wk, wv = (w_in[:, i * FOX_W:(i + 1) * FOX_W].astype(BF16) for i in range(3))
    wf = jnp.pad(w_in[:, 3 * FOX_W:], ((0, 0), (0, LANES - FOX_HEADS))).astype(BF16)
    bf = jnp.pad(W["b_fox_f"][o_idx], (0, LANES - FOX_HEADS))
    q = fused_linear([h2], [wq], gain=gain, name="fox_q").reshape(bsz, seq, FOX_W)
    lf = fused_linear([h2], [wf], gain=gain, bias=bf, act="log_sigmoid", name="fox_f")[:, :FOX_HEADS]
    lf = lf.reshape(bsz, seq, FOX_HEADS)
    lf_t = jnp.swapaxes(lf, 1, 2)
    shp = (bsz, seq, FOX_HEADS, FOX_HEAD_DIM)
    if prompt:
        k_t = fused_linear([h2], [wk], gain=gain, name="fox_k", seq_t=seq)
        v_t = fused_linear([h2], [wv], gain=gain, name="fox_v", seq_t=seq)
        att = fox_prompt_attn(q, k_t, v_t, cumsum_lanes(lf_t))
        k_out = k_t.reshape(bsz, FOX_HEADS, FOX_HEAD_DIM, seq).transpose(0, 3, 1, 2)
        v_out = v_t.reshape(bsz, FOX_HEADS, FOX_HEAD_DIM, seq).transpose(0, 3, 1, 2)
    else:
        k = fused_linear([h2], [wk], gain=gain, name="fox_k").reshape(bsz, seq, FOX_W)
        v = fused_linear([h2], [wv], gain=gain, name="fox_v").reshape(bsz, seq, FOX_W)
        k_out, v_out = k.reshape(shp), v.reshape(shp)
        n_pool = st["cache_fox_k"].shape[1]
        lfc_t = jnp.swapaxes(st["cache_fox_lf"][o_idx].astype(F32), 1, 2).reshape(n_pool * FOX_HEADS, PAGE_SIZE)
        rin, tot = page_suffix(lfc_t)
        rin = rin.reshape(n_pool, FOX_HEADS, PAGE_SIZE)
        tot = tot.reshape(n_pool, FOX_HEADS, PAGE_SIZE)
        lfn_t = jnp.pad(lf_t, ((0, 0), (0, 0), (0, PAGE_SIZE - seq)))
        kc_t = jnp.transpose(st["cache_fox_k"], (0, 1, 3, 4, 2))
        vc_t = jnp.transpose(st["cache_fox_v"], (0, 1, 3, 4, 2))
        att = fox_decode_attn(q, k, v, lfn_t, kc_t, vc_t, o_idx, rin, tot, st["page_table"])
    h_new = fused_linear([att.reshape(m, FOX_W)], [W["w_out_fox"][o_idx].astype(BF16)], residual=h2, name="fox_out")
    return h_new.reshape(bsz, seq, d), dict(fox_k=k_out, fox_v=v_out, fox_lf=lf)


def _run_group(x, W, st, prompt):
    bsz, seq, d = x.shape
    depth = W["norm_mix"].shape[0]
    names = ("ssd", "ssd_conv", "gdn", "gdn_conv", "fox_k", "fox_v", "fox_lf", "mem_k", "mem_v")
    out = {n: [] for n in names}
    h = x
    for layer in range(depth):
        if layer % 2 == 0:
            e = layer // 2
            if prompt:
                s0 = dict(ssd=jnp.zeros((bsz, SSD_HEADS, SSD_HEAD_DIM, SSD_STATE), F32),
                          ssd_conv=jnp.zeros((bsz, CONV_K - 1, SSD_CONV_DIM), F32),
                          gdn=jnp.zeros((bsz, GDN_HEADS, GDN_DK, GDN_DV), F32),
                          gdn_conv=jnp.zeros((bsz, CONV_K - 1, GDN_CONV_DIM), F32))
            else:
                s0 = dict(ssd=st["state_ssd"][e], ssd_conv=st["state_ssd_conv"][e],
                          gdn=st["state_gdn"][e], gdn_conv=st["state_gdn_conv"][e])
            h, new = _hybrid_layer(h, e, W, s0)
        else:
            h, new = _fox_layer(h, layer // 2, layer, W, st, prompt)
        for n, val in new.items():
            out[n].append(val)
        if prompt:
            mem = st["mem"]
            mem2 = mem.reshape(bsz * MEM_LEN, d)
            mk = fused_linear([mem2], [W["wk_x"][layer].astype(BF16)], gain=W["norm_mem"][layer], name="mem_k")
            mv = fused_linear([mem2], [W["wv_x"][layer].astype(BF16)], gain=W["norm_mem"][layer], name="mem_v")
            mk = mk.reshape(bsz, MEM_LEN, X_W)
            mv = mv.reshape(bsz, MEM_LEN, X_W)
            out["mem_k"].append(mk.reshape(bsz, MEM_LEN, X_HEADS, X_HEAD_DIM))
            out["mem_v"].append(mv.reshape(bsz, MEM_LEN, X_HEADS, X_HEAD_DIM))
        else:
            mk = st["cache_mem_k"][layer].reshape(bsz, MEM_LEN, X_W)
            mv = st["cache_mem_v"][layer].reshape(bsz, MEM_LEN, X_W)
        h = cross_attn(h, W["norm_x"][layer], W["wq_x"][layer].astype(BF16), mk, mv, W["wo_x"][layer].astype(BF16))
        h = swiglu(h.reshape(bsz * seq, d), W["norm_ffn"][layer], W["w1_b"], W["w3_b"], W["w2_b"], layer,
                   final_gain=W["norm_final"] if layer == depth - 1 else None).reshape(bsz, seq, d)
    new = {n: jnp.stack(out[n]) for n in names if out[n]}
    return h, new


def kernel(x_prompt, x_sample, mem_prompt, state_ssd, state_ssd_conv, state_gdn, state_gdn_conv, cache_fox_k, cache_fox_v, cache_fox_lf, page_table, cache_mem_k, cache_mem_v, norm_mix, norm_x, norm_mem, norm_ffn, norm_final, w_in_hyb, w_out_hyb, ssd_conv_w, ssd_conv_b, ssd_dt_bias, ssd_A_log, ssd_D, ssd_norm, gdn_conv_w, gdn_dt_bias, gdn_A_log, gdn_norm, w_in_fox, b_fox_f, w_out_fox, wq_x, wk_x, wv_x, wo_x, w1, w3, w2):
    W = dict(norm_mix=norm_mix, norm_x=norm_x, norm_mem=norm_mem, norm_ffn=norm_ffn, norm_final=norm_final,
             w_in_hyb=w_in_hyb, w_out_hyb=w_out_hyb, ssd_conv_w=ssd_conv_w, ssd_conv_b=ssd_conv_b,
             ssd_dt_bias=ssd_dt_bias, ssd_A_log=ssd_A_log, ssd_D=ssd_D, ssd_norm=ssd_norm,
             gdn_conv_w=gdn_conv_w, gdn_dt_bias=gdn_dt_bias, gdn_A_log=gdn_A_log, gdn_norm=gdn_norm,
             w_in_fox=w_in_fox, b_fox_f=b_fox_f, w_out_fox=w_out_fox,
             wq_x=wq_x, wk_x=wk_x, wv_x=wv_x, wo_x=wo_x,
             w1_b=w1.astype(BF16), w3_b=w3.astype(BF16), w2_b=w2.astype(BF16))
    y_prompt, pn = _run_group(x_prompt, W, dict(mem=mem_prompt), True)
    st = dict(state_ssd=state_ssd, state_ssd_conv=state_ssd_conv, state_gdn=state_gdn,
              state_gdn_conv=state_gdn_conv, cache_fox_k=cache_fox_k, cache_fox_v=cache_fox_v,
              cache_fox_lf=cache_fox_lf, page_table=page_table, cache_mem_k=cache_mem_k, cache_mem_v=cache_mem_v)
    y_sample, sn = _run_group(x_sample, W, st, False)
    return (y_prompt, y_sample,
            pn["ssd"], pn["ssd_conv"], pn["gdn"], pn["gdn_conv"],
            pn["fox_k"], pn["fox_v"], pn["fox_lf"], pn["mem_k"], pn["mem_v"],
            sn["ssd"], sn["ssd_conv"], sn["gdn"], sn["gdn_conv"],
            sn["fox_k"], sn["fox_v"], sn["fox_lf"])
```

```python
import functools
import math

import jax
import jax.numpy as jnp
from jax import lax
from jax.experimental import pallas as pl
from jax.experimental.pallas import tpu as pltpu

F32 = jnp.float32
BF16 = jnp.bfloat16
HI = lax.Precision.HIGHEST
NT_DIMS = (((1,), (1,)), ((), ()))
TN_DIMS = (((0,), (0,)), ((), ()))

D_MODEL = 1024
EPS = 1e-6
CONV_K = 4
SSD_HEADS = 16
SSD_HEAD_DIM = 64
SSD_INNER = SSD_HEADS * SSD_HEAD_DIM
SSD_GROUPS = 2
SSD_STATE = 64
SSD_CONV_DIM = SSD_INNER + 2 * SSD_GROUPS * SSD_STATE
GDN_HEADS = 8
GDN_DK = 128
GDN_DV = 128
GDN_QK = GDN_HEADS * GDN_DK
GDN_VW = GDN_HEADS * GDN_DV
GDN_CONV_DIM = 2 * GDN_QK + GDN_VW
FOX_HEADS = 16
FOX_HEAD_DIM = 64
FOX_W = FOX_HEADS * FOX_HEAD_DIM
PAGE_SIZE = 128
MEM_LEN = 256
X_HEADS = 4
X_HEAD_DIM = 128
X_W = X_HEADS * X_HEAD_DIM
LANES = 128
VMEM_LIMIT = 56 * 1024 * 1024


def _cparams(*sem):
    return pltpu.CompilerParams(dimension_semantics=sem, vmem_limit_bytes=VMEM_LIMIT)


def _pick(n, cands):
    for c in cands:
        if n % c == 0:
            return c
    return n


def _softplus(x):
    return jnp.maximum(x, 0.0) + jnp.log1p(jnp.exp(-jnp.abs(x)))


def _log_sigmoid(x):
    return jnp.minimum(x, 0.0) - jnp.log1p(jnp.exp(-jnp.abs(x)))


def _sigmoid(x):
    return 1.0 / (1.0 + jnp.exp(-x))


def _silu(x):
    return x * _sigmoid(x)


def _rms_rows(x):
    return x * lax.rsqrt(jnp.mean(x * x, axis=-1, keepdims=True) + EPS)


def _dot(a, b):
    return jnp.dot(a, b, preferred_element_type=F32)


def _dot_hi(a, b):
    return jnp.dot(a, b, preferred_element_type=F32, precision=HI)


def _dot_nt(a, b):
    return lax.dot_general(a, b, NT_DIMS, preferred_element_type=F32)


def _dot_tn(a, b):
    return lax.dot_general(a, b, TN_DIMS, preferred_element_type=F32)


def _iota(shape, dim):
    return lax.broadcasted_iota(jnp.int32, shape, dim)


def _linear_body(*refs, n_x, use_norm, act, has_res, transposed):
    it = iter(refs)
    x_refs = [next(it) for _ in range(n_x)]
    g_ref = next(it) if use_norm else None
    w_refs = [next(it) for _ in range(n_x)]
    b_ref = next(it) if act else None
    r_ref = next(it) if has_res else None
    o_ref = next(it)
    xs_refs = [next(it) for _ in range(n_x)]

    @pl.when(pl.program_id(1) == 0)
    def _():
        for i in range(n_x):
            x = x_refs[i][...]
            if use_norm and i == 0:
                x = _rms_rows(x) * g_ref[...]
            xs_refs[i][...] = x.astype(BF16)

    acc = _dot(xs_refs[0][...], w_refs[0][...])
    for i in range(1, n_x):
        acc = acc + _dot(xs_refs[i][...], w_refs[i][...])
    if act == "log_sigmoid":
        acc = _log_sigmoid(acc + b_ref[...])
    if has_res:
        acc = acc + r_ref[...]
    if transposed:
        o_ref[0] = acc.T
    else:
        o_ref[...] = acc


def fused_linear(xs, ws, gain=None, bias=None, act=None, residual=None, name="linear", seq_t=None):
    m = xs[0].shape[0]
    n = ws[0].shape[1]
    tm = _pick(seq_t or m, (1024, 512, 256, 128, 64, 32, 16, 8))
    tn = _pick(n, (1536, 1280, 1024, 768, 640, 512, 384, 256, 128))
    n_x = len(xs)
    in_specs, args = [], []
    for x in xs:
        in_specs.append(pl.BlockSpec((tm, x.shape[1]), lambda i, j: (i, 0)))
        args.append(x)
    if gain is not None:
        in_specs.append(pl.BlockSpec((1, xs[0].shape[1]), lambda i, j: (0, 0)))
        args.append(gain.reshape(1, -1).astype(F32))
    for w in ws:
        in_specs.append(pl.BlockSpec((w.shape[0], tn), lambda i, j: (0, j)))
        args.append(w)
    if act:
        in_specs.append(pl.BlockSpec((1, tn), lambda i, j: (0, j)))
        args.append(bias.reshape(1, -1).astype(F32))
    if residual is not None:
        in_specs.append(pl.BlockSpec((tm, tn), lambda i, j: (i, j)))
        args.append(residual)
    body = functools.partial(_linear_body, n_x=n_x, use_norm=gain is not None, act=act,
                             has_res=residual is not None, transposed=seq_t is not None)
    if seq_t is None:
        out_spec = pl.BlockSpec((tm, tn), lambda i, j: (i, j))
        out_shape = jax.ShapeDtypeStruct((m, n), F32)
    else:
        per_seq = seq_t // tm
        out_spec = pl.BlockSpec((1, tn, tm), lambda i, j: (i // per_seq, j, i % per_seq))
        out_shape = jax.ShapeDtypeStruct((m // seq_t, n, seq_t), F32)
    return pl.pallas_call(
        body,
        grid=(m // tm, n // tn),
        in_specs=in_specs,
        out_specs=out_spec,
        out_shape=out_shape,
        scratch_shapes=[pltpu.VMEM((tm, x.shape[1]), BF16) for x in xs],
        compiler_params=_cparams("parallel", "arbitrary"),
        name=name,
    )(*args)


SUB = 8


def _conv_body(x_ref, prev_ref, buf_ref, w_ref, b_ref, o_ref):
    first = pl.program_id(1) == 0
    ext = jnp.concatenate([jnp.where(first, buf_ref[0], prev_ref[0]), x_ref[0]], axis=0)
    w = w_ref[...]

    def tap(j):
        back = CONV_K - 1 - j
        shifted = pltpu.roll(ext, back, axis=0) if back else ext
        return shifted[SUB:, :] * w[j:j + 1, :]

    y = tap(0)
    for j in range(1, CONV_K):
        y = y + tap(j)
    y = y + b_ref[...]
    o_ref[0] = _silu(y)


def conv_silu(x, buf, w, b):
    bsz, seq, ch = x.shape
    tt = _pick(seq, (512, 256, 128, 64, 32, 16, 8))
    tc = ch if tt <= 64 else _pick(ch, (1024, 640, 512, 256, 128))
    bufp = jnp.concatenate([jnp.zeros((bsz, SUB - (CONV_K - 1), ch), F32), buf], axis=1)
    wp = jnp.concatenate([w, jnp.zeros((SUB - CONV_K, ch), F32)], axis=0)
    nsub = tt // SUB
    return pl.pallas_call(
        _conv_body,
        grid=(bsz, seq // tt, ch // tc),
        in_specs=[
            pl.BlockSpec((1, tt, tc), lambda bi, ti, ci: (bi, ti, ci)),
            pl.BlockSpec((1, SUB, tc), lambda bi, ti, ci: (bi, jnp.maximum(ti * nsub - 1, 0), ci)),
            pl.BlockSpec((1, SUB, tc), lambda bi, ti, ci: (bi, 0, ci)),
            pl.BlockSpec((SUB, tc), lambda bi, ti, ci: (0, ci)),
            pl.BlockSpec((1, tc), lambda bi, ti, ci: (0, ci)),
        ],
        out_specs=pl.BlockSpec((1, tt, tc), lambda bi, ti, ci: (bi, ti, ci)),
        out_shape=jax.ShapeDtypeStruct((bsz, seq, ch), F32),
        compiler_params=_cparams("parallel", "parallel", "parallel"),
        name="conv_silu",
    )(x, x, bufp, wp, b.reshape(1, ch))


HG = SSD_HEADS // SSD_GROUPS
GW = HG * SSD_HEAD_DIM


def _expand_heads(v, e_bf16):
    hi = v.astype(BF16).astype(F32)
    mid = (v - hi).astype(BF16).astype(F32)
    lo = (v - hi - mid).astype(BF16).astype(F32)
    return _dot(jnp.concatenate([hi, mid, lo], axis=1).astype(BF16), jnp.concatenate([e_bf16] * 3, axis=0))


def _ssd_body(xbc_ref, z_ref, small_ref, dtt_ref, h0_ref, prow_ref, pcol_ref, dexp_ref, norm_ref,
              y_ref, hout_ref, h_sc, y_sc, *, q):
    @pl.when(pl.program_id(1) == 0)
    def _():
        h_sc[...] = h0_ref[0]

    xbc = xbc_ref[0]
    xs = xbc[:, :SSD_INNER]
    bm = xbc[:, SSD_INNER:SSD_INNER + LANES]
    cm = xbc[:, SSD_INNER + LANES:SSD_INNER + 2 * LANES]
    dt_c = _softplus(small_ref[0][:, 0:SSD_HEADS] + prow_ref[0:1, 0:SSD_HEADS])
    a_c = dt_c * (-jnp.exp(prow_ref[1:2, 0:SSD_HEADS]))
    dt_t = _softplus(dtt_ref[0] + pcol_ref[:, 0:1])
    a_t = dt_t * (-jnp.exp(pcol_ref[:, 1:2]))
    ri = _iota((q, q), 0)
    ci = _iota((q, q), 1)
    causal = ci <= ri
    cum = _dot_hi(causal.astype(F32), a_c)
    cum_t = _dot_hi(a_t, (ri <= ci).astype(F32))
    e_heads = (_iota((SSD_HEADS, SSD_INNER), 1) // SSD_HEAD_DIM == _iota((SSD_HEADS, SSD_INNER), 0)).astype(BF16)
    dt_x = _expand_heads(dt_c, e_heads)
    ecum_x = _expand_heads(jnp.exp(cum), e_heads)
    wlast_x = _expand_heads(jnp.exp(cum[q - 1:q, :] - cum), e_heads)
    xdt = xs * dt_x
    xw = (xdt * wlast_x).astype(BF16)
    xdt_b = xdt.astype(BF16)

    lane = _iota((1, LANES), 1)
    low = lane < SSD_STATE
    bm_b = bm.astype(BF16)
    hs = h_sc[...]
    hs_b = hs.astype(BF16)
    y_inter = []
    cbs = []
    upd = []
    for g in range(SSD_GROUPS):
        cm_g = jnp.where(low if g == 0 else jnp.logical_not(low), cm, 0.0).astype(BF16)
        cbs.append(_dot_nt(cm_g, bm_b))
        y_inter.append(_dot(cm_g, hs_b))
        upd.append(_dot_tn(bm_b, xw[:, g * GW:(g + 1) * GW]))
    for j in range(SSD_HEADS // 2):
        g = (2 * j) // HG
        xp = xdt_b[:, j * LANES:(j + 1) * LANES]
        ys = []
        for hh in (2 * j, 2 * j + 1):
            seg = cum[:, hh:hh + 1] - cum_t[hh:hh + 1, :]
            lm = jnp.exp(jnp.where(causal, seg, -jnp.inf))
            ys.append(_dot((cbs[g] * lm).astype(BF16), xp))
        y_sc[:, j * LANES:(j + 1) * LANES] = jnp.where(low, ys[0], ys[1])
    y = y_sc[...] + jnp.concatenate(y_inter, axis=1) * ecum_x + dexp_ref[...] * xs
    y = y * _silu(z_ref[0])
    y = jnp.concatenate([_rms_rows(y[:, g * GW:(g + 1) * GW]) for g in range(SSD_GROUPS)], axis=1)
    y_ref[0] = y * norm_ref[...]

    row_low = _iota((2 * SSD_STATE, 1), 0) < SSD_STATE
    ecl = ecum_x[q - 1:q, :]
    decay = jnp.where(row_low, ecl[:, 0:GW], ecl[:, GW:2 * GW])
    h_new = hs * decay + jnp.where(row_low, upd[0], upd[1])
    h_sc[...] = h_new

    @pl.when(pl.program_id(1) == pl.num_programs(1) - 1)
    def _():
        hout_ref[0] = h_new


def ssd_scan(xbc_c, z, small, h0, dt_bias, a_log, d_skip, norm_w):
    bsz, seq, _ = xbc_c.shape
    q = _pick(seq, (128, 64, 32, 16, 8))
    dtt = jnp.swapaxes(small[:, :, 0:SSD_HEADS], 1, 2)
    hs0 = h0.reshape(bsz, SSD_GROUPS, HG, SSD_HEAD_DIM, SSD_STATE).transpose(0, 1, 4, 2, 3)
    hs0 = hs0.reshape(bsz, SSD_GROUPS * SSD_STATE, GW)
    prow = jnp.zeros((SUB, LANES), F32).at[0, :SSD_HEADS].set(dt_bias).at[1, :SSD_HEADS].set(a_log)
    pcol = jnp.zeros((SSD_HEADS, LANES), F32).at[:, 0].set(dt_bias).at[:, 1].set(a_log)
    dexp = jnp.repeat(d_skip, SSD_HEAD_DIM).reshape(1, SSD_INNER)
    y, hs = pl.pallas_call(
        functools.partial(_ssd_body, q=q),
        grid=(bsz, seq // q),
        in_specs=[
            pl.BlockSpec((1, q, SSD_CONV_DIM), lambda b, c: (b, c, 0)),
            pl.BlockSpec((1, q, SSD_INNER), lambda b, c: (b, c, 0)),
            pl.BlockSpec((1, q, LANES), lambda b, c: (b, c, 0)),
            pl.BlockSpec((1, SSD_HEADS, q), lambda b, c: (b, 0, c)),
            pl.BlockSpec((1, 2 * SSD_STATE, GW), lambda b, c: (b, 0, 0)),
            pl.BlockSpec((SUB, LANES), lambda b, c: (0, 0)),
            pl.BlockSpec((SSD_HEADS, LANES), lambda b, c: (0, 0)),
            pl.BlockSpec((1, SSD_INNER), lambda b, c: (0, 0)),
            pl.BlockSpec((1, SSD_INNER), lambda b, c: (0, 0)),
        ],
        out_specs=[
            pl.BlockSpec((1, q, SSD_INNER), lambda b, c: (b, c, 0)),
            pl.BlockSpec((1, 2 * SSD_STATE, GW), lambda b, c: (b, 0, 0)),
        ],
        out_shape=[
            jax.ShapeDtypeStruct((bsz, seq, SSD_INNER), F32),
            jax.ShapeDtypeStruct((bsz, 2 * SSD_STATE, GW), F32),
        ],
        scratch_shapes=[pltpu.VMEM((2 * SSD_STATE, GW), F32), pltpu.VMEM((q, SSD_INNER), F32)],
        compiler_params=_cparams("parallel", "arbitrary"),
        name="ssd_scan",
    )(xbc_c, z, small, dtt, hs0, prow, pcol, dexp, norm_w.reshape(1, SSD_INNER))
    h_new = hs.reshape(bsz, SSD_GROUPS, SSD_STATE, HG, SSD_HEAD_DIM).transpose(0, 1, 3, 4, 2)
    return y, h_new.reshape(bsz, SSD_HEADS, SSD_HEAD_DIM, SSD_STATE)


def _split2(x):
    hi = x.astype(BF16)
    return hi, (x - hi.astype(F32)).astype(BF16)


def _dot3(a, b):
    ah, al = _split2(a)
    bh, bl = _split2(b)
    return _dot(ah, bh) + (_dot(ah, bl) + _dot(al, bh))


def _gdn_prep_body(q_ref, k_ref, v_ref, small_ref, gbt_ref, prow_ref,
                   u_ref, w_ref, qg_ref, kd_ref, qk_ref, eg_ref, *, c, nch, hp):
    lane = _iota((1, LANES), 1)

    def pick_lane(x, idx):
        return jnp.sum(jnp.where(lane == idx, x, 0.0), axis=1, keepdims=True)

    small = small_ref[0]
    beta_col, g_col, g_row = [], [], []
    for i in range(hp):
        h = pl.program_id(1) * hp + i
        neg_a = -jnp.exp(pick_lane(prow_ref[0:1, :], h))
        dtb = pick_lane(prow_ref[1:2, :], h)
        beta_col.append(_sigmoid(pick_lane(small, SSD_HEADS + h)))
        g_col.append(neg_a * _softplus(pick_lane(small, SSD_HEADS + GDN_HEADS + h) + dtb))
        g_row.append(neg_a * _softplus(gbt_ref[0, pl.ds(GDN_HEADS + h, 1), :] + dtb))

    ri = _iota((c, c), 0)
    ci = _iota((c, c), 1)
    tril = ri >= ci
    tril_f = tril.astype(F32)
    triu_f = (ri <= ci).astype(F32)
    eye = (ri == ci).astype(F32)
    units = [(i, ch) for i in range(hp) for ch in range(nch)]
    rows = [slice(ch * c, (ch + 1) * c) for _, ch in units]
    lanes = [slice(i * GDN_DK, (i + 1) * GDN_DK) for i, _ in units]
    nu = range(len(units))
    qn, kn, kb, gam, dec, nmat = [], [], [], [], [], []
    for n in nu:
        i = units[n][0]
        qc = q_ref[0, rows[n], lanes[n]]
        kc = k_ref[0, rows[n], lanes[n]]
        qn.append(qc * lax.rsqrt(jnp.sum(qc * qc, axis=-1, keepdims=True) + 1e-6) * (GDN_DK ** -0.5))
        kn.append(kc * lax.rsqrt(jnp.sum(kc * kc, axis=-1, keepdims=True) + 1e-6))
        kb.append(kn[n] * beta_col[i][rows[n], :])
        gam.append(_dot_hi(tril_f, jnp.broadcast_to(g_col[i][rows[n], :], (c, LANES))))
    for n in nu:
        gam_r = _dot_hi(jnp.broadcast_to(g_row[units[n][0]][:, rows[n]], (SUB, c)), triu_f)[0:1, :]
        dec.append(jnp.exp(jnp.where(tril, gam[n][:, 0:c] - gam_r, -jnp.inf)))
        nmat.append(jnp.where(ri > ci, _dot_nt(kb[n].astype(BF16), kn[n].astype(BF16)) * dec[n], 0.0) * -1.0)
    tinv = [eye + m for m in nmat]
    npow = [_dot3(m, m) for m in nmat]
    levels = int(math.log2(c))
    for lvl in range(1, levels):
        for n in nu:
            if lvl == levels - 1:
                tinv[n] = tinv[n] + _dot3(npow[n], tinv[n])
            else:
                both = _dot3(npow[n], jnp.concatenate([tinv[n], npow[n]], axis=1))
                tinv[n] = tinv[n] + both[:, :c]
                npow[n] = both[:, c:]
    for n in nu:
        i, ch = units[n]
        egam = jnp.exp(gam[n])
        vb = (v_ref[0, rows[n], lanes[n]] * beta_col[i][rows[n], :]).astype(BF16)
        uw = _dot(tinv[n].astype(BF16), jnp.concatenate([vb, (kb[n] * egam).astype(BF16)], axis=1))
        glast = gam[n][c - 1:c, :]
        u_ref[0, rows[n], lanes[n]] = uw[:, :GDN_DV]
        w_ref[0, rows[n], lanes[n]] = uw[:, GDN_DV:].astype(w_ref.dtype)
        qg_ref[0, rows[n], lanes[n]] = (qn[n] * egam).astype(qg_ref.dtype)
        kd_ref[0, rows[n], lanes[n]] = (kn[n] * jnp.exp(glast - gam[n])).astype(kd_ref.dtype)
        qk_ref[0, i, rows[n], :] = (_dot_nt(qn[n].astype(BF16), kn[n].astype(BF16)) * dec[n]).astype(qk_ref.dtype)
        eg_ref[0, i, ch:ch + 1, :] = jnp.exp(glast)


GDN_GROUP = 8


def _gdn_scan_body(u_ref, w_ref, qg_ref, kd_ref, qk_ref, eg_ref, gate_ref, s0_ref, norm_ref,
                   o_ref, sout_ref, s_sc, *, c, nch):
    @pl.when(pl.program_id(2) == 0)
    def _():
        s_sc[...] = s0_ref[0]

    s = [s_sc[i] for i in range(GDN_GROUP)]
    heads = range(GDN_GROUP)
    lanes = [slice(i * GDN_DV, (i + 1) * GDN_DV) for i in heads]
    for ch in range(nch):
        rows = slice(ch * c, (ch + 1) * c)
        s_b = [s[i].astype(BF16) for i in heads]
        ws = [_dot(jnp.concatenate([w_ref[0, rows, lanes[i]].astype(BF16), qg_ref[0, rows, lanes[i]].astype(BF16)],
                                   axis=0), s_b[i]) for i in heads]
        vn_b = [(u_ref[0, rows, lanes[i]] - ws[i][:c]).astype(BF16) for i in heads]
        o = [ws[i][c:] + _dot(qk_ref[0, i, rows, :].astype(BF16), vn_b[i]) for i in heads]
        s = [s[i] * eg_ref[0, i, ch:ch + 1, :] + _dot_tn(kd_ref[0, rows, lanes[i]].astype(BF16), vn_b[i]) for i in heads]
        for i in heads:
            o_ref[0, rows, lanes[i]] = _rms_rows(o[i]) * norm_ref[...] * _silu(gate_ref[0, rows, lanes[i]])
    for i in range(GDN_GROUP):
        s_sc[i] = s[i]

    @pl.when(pl.program_id(2) == pl.num_programs(2) - 1)
    def _():
        for i in range(GDN_GROUP):
            sout_ref[0, i] = s[i]


def gdn_scan(qkv_c, gate, small, s0, dt_bias, a_log, norm_w):
    bsz, seq, _ = qkv_c.shape
    c = _pick(seq, (64, 32, 16, 8))
    tb = _pick(seq, (512, 256, 128, 64, 32, 16, 8))
    nch = tb // c
    wdt = BF16 if c % 16 == 0 else F32
    gbt = jnp.swapaxes(small[:, :, SSD_HEADS:SSD_HEADS + 2 * GDN_HEADS], 1, 2)
    prow = jnp.zeros((SUB, LANES), F32).at[0, :GDN_HEADS].set(a_log).at[1, :GDN_HEADS].set(dt_bias)
    hp = 2 if nch >= 4 else GDN_HEADS
    ng = GDN_HEADS // hp
    head_blk = pl.BlockSpec((1, tb, hp * GDN_DV), lambda b, h, t: (b, t, h))
    u, w, qg, kd, qk, eg = pl.pallas_call(
        functools.partial(_gdn_prep_body, c=c, nch=nch, hp=hp),
        grid=(bsz, ng, seq // tb),
        in_specs=[
            pl.BlockSpec((1, tb, hp * GDN_DK), lambda b, h, t: (b, t, h)),
            pl.BlockSpec((1, tb, hp * GDN_DK), lambda b, h, t: (b, t, ng + h)),
            pl.BlockSpec((1, tb, hp * GDN_DV), lambda b, h, t: (b, t, 2 * ng + h)),
            pl.BlockSpec((1, tb, LANES), lambda b, h, t: (b, t, 0)),
            pl.BlockSpec((1, 2 * GDN_HEADS, tb), lambda b, h, t: (b, 0, t)),
            pl.BlockSpec((SUB, LANES), lambda b, h, t: (0, 0)),
        ],
        out_specs=[head_blk, head_blk, head_blk, head_blk,
                   pl.BlockSpec((1, hp, tb, c), lambda b, h, t: (b, h, t, 0)),
                   pl.BlockSpec((1, hp, nch, LANES), lambda b, h, t: (b, h, t, 0))],
        out_shape=[
            jax.ShapeDtypeStruct((bsz, seq, GDN_VW), F32),
            jax.ShapeDtypeStruct((bsz, seq, GDN_VW), wdt),
            jax.ShapeDtypeStruct((bsz, seq, GDN_QK), wdt),
            jax.ShapeDtypeStruct((bsz, seq, GDN_QK), wdt),
            jax.ShapeDtypeStruct((bsz, GDN_HEADS, seq, c), wdt),
            jax.ShapeDtypeStruct((bsz, GDN_HEADS, seq // c, LANES), F32),
        ],
        compiler_params=_cparams("parallel", "parallel", "parallel"),
        name="gdn_prep",
    )(qkv_c, qkv_c, qkv_c, small, gbt, prow)
    gw = GDN_GROUP * GDN_DV
    grp_blk = pl.BlockSpec((1, tb, gw), lambda b, g, t: (b, t, g))
    state_blk = pl.BlockSpec((1, GDN_GROUP, GDN_DK, GDN_DV), lambda b, g, t: (b, g, 0, 0))
    o, s_new = pl.pallas_call(
        functools.partial(_gdn_scan_body, c=c, nch=nch),
        grid=(bsz, GDN_HEADS // GDN_GROUP, seq // tb),
        in_specs=[
            grp_blk, grp_blk, grp_blk, grp_blk,
            pl.BlockSpec((1, GDN_GROUP, tb, c), lambda b, g, t: (b, g, t, 0)),
            pl.BlockSpec((1, GDN_GROUP, nch, LANES), lambda b, g, t: (b, g, t, 0)),
            grp_blk,
            state_blk,
            pl.BlockSpec((1, GDN_DV), lambda b, g, t: (0, 0)),
        ],
        out_specs=[grp_blk, state_blk],
        out_shape=[
            jax.ShapeDtypeStruct((bsz, seq, GDN_VW), F32),
            jax.ShapeDtypeStruct((bsz, GDN_HEADS, GDN_DK, GDN_DV), F32),
        ],
        scratch_shapes=[pltpu.VMEM((GDN_GROUP, GDN_DK, GDN_DV), F32)],
        compiler_params=_cparams("parallel", "parallel", "arbitrary"),
        name="gdn_scan",
    )(u, w, qg, kd, qk, eg, gate, s0, norm_w.reshape(1, GDN_DV))
    return o, s_new


def _xattn_body(h_ref, g_ref, wq_ref, mk_ref, mv_ref, wo_ref, o_ref, *, nb, tm):
    d = h_ref.shape[-1]
    x = h_ref[...].reshape(nb * tm, d)
    xn = (_rms_rows(x) * g_ref[...]).astype(BF16)
    qf = _dot(xn, wq_ref[...])
    stacked = len(mk_ref.shape) == 5
    units = [(bi, hd) for bi in range(nb) for hd in range(X_HEADS)]

    def mem(ref, bi, hd):
        m = ref[0, bi, :, hd, :] if stacked else ref[bi, :, hd * X_HEAD_DIM:(hd + 1) * X_HEAD_DIM]
        return m.astype(BF16)

    ss = [_dot_nt(qf[bi * tm:(bi + 1) * tm, hd * X_HEAD_DIM:(hd + 1) * X_HEAD_DIM].astype(BF16), mem(mk_ref, bi, hd))
          * (X_HEAD_DIM ** -0.5) for bi, hd in units]
    ps = []
    for s in ss:
        p = jnp.exp(s - jnp.max(s, axis=-1, keepdims=True))
        ps.append((p / jnp.sum(p, axis=-1, keepdims=True)).astype(BF16))
    os_ = [_dot(ps[n], mem(mv_ref, bi, hd)) for n, (bi, hd) in enumerate(units)]
    outs = [jnp.concatenate(os_[bi * X_HEADS:(bi + 1) * X_HEADS], axis=1) for bi in range(nb)]
    o = jnp.concatenate(outs, axis=0).astype(BF16)
    o_ref[...] = (x + _dot(o, wo_ref[...])).reshape(nb, tm, d)


def cross_attn(h, gain, wq, mk, mv, wo, mem_layer=None):
    bsz, seq, d = h.shape
    tm = _pick(seq, (512, 256, 128, 64, 32, 16, 8))
    nb = _pick(bsz, (8, 4, 2, 1)) if seq <= 64 else 1
    if mem_layer is None:
        mem_spec = pl.BlockSpec((nb, MEM_LEN, X_W), lambda b, i: (b, 0, 0))
    else:
        mem_spec = pl.BlockSpec((1, nb, MEM_LEN, X_HEADS, X_HEAD_DIM), lambda b, i: (mem_layer, b, 0, 0, 0))
    return pl.pallas_call(
        functools.partial(_xattn_body, nb=nb, tm=tm),
        grid=(bsz // nb, seq // tm),
        in_specs=[
            pl.BlockSpec((nb, tm, d), lambda b, i: (b, i, 0)),
            pl.BlockSpec((1, d), lambda b, i: (0, 0)),
            pl.BlockSpec((d, X_W), lambda b, i: (0, 0)),
            mem_spec,
            mem_spec,
            pl.BlockSpec((X_W, d), lambda b, i: (0, 0)),
        ],
        out_specs=pl.BlockSpec((nb, tm, d), lambda b, i: (b, i, 0)),
        out_shape=jax.ShapeDtypeStruct((bsz, seq, d), F32),
        compiler_params=_cparams("parallel", "parallel"),
        name="cross_attn",
    )(h, gain.reshape(1, d), wq, mk, mv, wo)


def _swiglu_body(h_ref, g_ref, w1_ref, w3_ref, w2_ref, *rest, final_norm):
    gf_ref = rest[0] if final_norm else None
    o_ref, xn_sc = rest[-2:]
    j = pl.program_id(1)

    @pl.when(j == 0)
    def _():
        x = h_ref[...]
        xn_sc[...] = (_rms_rows(x) * g_ref[...]).astype(BF16)
        o_ref[...] = x

    xn = xn_sc[...]
    a = _dot(xn, w1_ref[0])
    b = _dot(xn, w3_ref[0])
    o_ref[...] += _dot((_silu(a) * b).astype(BF16), w2_ref[0])

    if final_norm:
        @pl.when(j == pl.num_programs(1) - 1)
        def _():
            o_ref[...] = _rms_rows(o_ref[...]) * gf_ref[...]


def swiglu(h, gain, w1, w3, w2, layer, final_gain=None):
    m, d = h.shape
    ff = w1.shape[2]
    tm = _pick(m, (512, 256, 128, 64, 32, 16, 8))
    tf = _pick(ff, (1408, 1024, 512, 256, 128))
    in_specs = [
        pl.BlockSpec((tm, d), lambda i, j: (i, 0)),
        pl.BlockSpec((1, d), lambda i, j: (0, 0)),
        pl.BlockSpec((1, d, tf), lambda i, j: (layer, 0, j)),
        pl.BlockSpec((1, d, tf), lambda i, j: (layer, 0, j)),
        pl.BlockSpec((1, tf, d), lambda i, j: (layer, j, 0)),
    ]
    args = [h, gain.reshape(1, d), w1, w3, w2]
    if final_gain is not None:
        in_specs.append(pl.BlockSpec((1, d), lambda i, j: (0, 0)))
        args.append(final_gain.reshape(1, d))
    return pl.pallas_call(
        functools.partial(_swiglu_body, final_norm=final_gain is not None),
        grid=(m // tm, ff // tf),
        in_specs=in_specs,
        out_specs=pl.BlockSpec((tm, d), lambda i, j: (i, 0)),
        out_shape=jax.ShapeDtypeStruct((m, d), F32),
        scratch_shapes=[pltpu.VMEM((tm, d), BF16)],
        compiler_params=_cparams("parallel", "arbitrary"),
        name="swiglu",
    )(*args)


def _cumsum_lanes_body(x_ref, o_ref, carry_sc, *, tc):
    @pl.when(pl.program_id(1) == 0)
    def _():
        carry_sc[...] = jnp.zeros_like(carry_sc)

    upper = (_iota((tc, tc), 0) <= _iota((tc, tc), 1)).astype(F32)
    f = _dot_hi(x_ref[0], upper) + carry_sc[:, 0:1]
    o_ref[0] = f
    carry_sc[...] = jnp.broadcast_to(f[:, tc - 1:tc], carry_sc.shape)


def cumsum_lanes(x):
    bsz, r, seq = x.shape
    tc = _pick(seq, (512, 256, 128))
    return pl.pallas_call(
        functools.partial(_cumsum_lanes_body, tc=tc),
        grid=(bsz, seq // tc),
        in_specs=[pl.BlockSpec((1, r, tc), lambda b, i: (b, 0, i))],
        out_specs=pl.BlockSpec((1, r, tc), lambda b, i: (b, 0, i)),
        out_shape=jax.ShapeDtypeStruct((bsz, r, seq), F32),
        scratch_shapes=[pltpu.VMEM((r, LANES), F32)],
        compiler_params=_cparams("parallel", "arbitrary"),
        name="cumsum_lanes",
    )(x)


FOX_ROWS = 512
FOX_PAIRS = 2


LOG2E = 1.4426950408889634


def _fox_body(qi_ref, ki_ref, q_ref, k_ref, v_ref, f_ref, o_ref, m_sc, acc_sc, *, t):
    step = pl.program_id(2)
    qi = qi_ref[step]
    ki = ki_ref[step]
    low = _iota((1, LANES), 1) < FOX_HEAD_DIM
    rb = min(FOX_ROWS, t)

    @pl.when(ki == 0)
    def _():
        m_sc[...] = jnp.full(m_sc.shape, -jnp.inf, F32)
        acc_sc[...] = jnp.zeros_like(acc_sc)

    def update(masked):
        units = [(pr, r, hh) for pr in range(FOX_PAIRS) for r in range(t // rb) for hh in range(2)]
        kbs, vbs = {}, {}

        def nkeys(r):
            return (r + 1) * rb if masked else t

        def qk(u):
            pr, r, hh = u
            lanes = slice(pr * LANES, (pr + 1) * LANES)
            if pr not in kbs:
                kbs[pr] = k_ref[0, lanes, :].astype(BF16)
                v = v_ref[0, lanes, :]
                first = _iota((LANES, 1), 0) < FOX_HEAD_DIM
                vbs[pr] = (jnp.where(first, v, 1.0).astype(BF16), jnp.where(first, 1.0, v).astype(BF16))
            qs = q_ref[0, r * rb:(r + 1) * rb, lanes] * (FOX_HEAD_DIM ** -0.5 * LOG2E)
            qm = jnp.where(low if hh == 0 else jnp.logical_not(low), qs, 0.0).astype(BF16)
            return _dot(qm, kbs[pr][:, :nkeys(r)])

        s_next = qk(units[0])
        for i, (pr, r, hh) in enumerate(units):
            rows = slice(r * rb, (r + 1) * rb)
            nk = nkeys(r)
            s = s_next - f_ref[0, pr, hh:hh + 1, :nk] * LOG2E
            if i + 1 < len(units):
                s_next = qk(units[i + 1])
            if masked:
                s = jnp.where(_iota((rb, nk), 1) <= _iota((rb, nk), 0) + r * rb, s, -jnp.inf)
            m_prev = m_sc[pr, hh, rows, :]
            m_new = jnp.maximum(m_prev, jnp.max(s, axis=-1, keepdims=True))
            alpha = jnp.exp2(m_prev - m_new)
            p = jnp.exp2(s - jnp.concatenate([m_new] * (nk // LANES), axis=1))
            m_sc[pr, hh, rows, :] = m_new
            acc_sc[pr, hh, rows, :] = alpha * acc_sc[pr, hh, rows, :] + _dot_nt(p.astype(BF16), vbs[pr][hh][:, :nk])

    @pl.when(ki < qi)
    def _():
        update(False)

    @pl.when(ki == qi)
    def _():
        update(True)
        for pr in range(FOX_PAIRS):
            outs = [acc_sc[pr, hh] / pltpu.roll(acc_sc[pr, hh], FOX_HEAD_DIM, axis=1) for hh in range(2)]
            o_ref[0, :, pr * LANES:(pr + 1) * LANES] = jnp.where(low, outs[0], outs[1])


def fox_prompt_attn(q, k, v, ft):
    bsz, seq, _ = q.shape
    t = _pick(seq, (1024, 512, 256, 128))
    nb = seq // t
    pairs =[(i, j) for i in range(nb) for j in range(i + 1)]
    qi = jnp.asarray([p[0] for p in pairs], jnp.int32)
    ki = jnp.asarray([p[1] for p in pairs], jnp.int32)
    npair = FOX_HEADS // 2
    w = FOX_PAIRS * LANES
    ft4 = ft.reshape(bsz, npair, 2, seq)
    grid_spec = pltpu.PrefetchScalarGridSpec(
        num_scalar_prefetch=2,
        grid=(bsz, npair // FOX_PAIRS, len(pairs)),
        in_specs=[
            pl.BlockSpec((1, t, w), lambda b, j, s, qi, ki: (b, qi[s], j)),
            pl.BlockSpec((1, w, t), lambda b, j, s, qi, ki: (b, j, ki[s])),
            pl.BlockSpec((1, w, t), lambda b, j, s, qi, ki: (b, j, ki[s])),
            pl.BlockSpec((1, FOX_PAIRS, 2, t), lambda b, j, s, qi, ki: (b, j, 0, ki[s])),
        ],
        out_specs=pl.BlockSpec((1, t, w), lambda b, j, s, qi, ki: (b, qi[s], j)),
        scratch_shapes=[pltpu.VMEM((FOX_PAIRS, 2, t, LANES), F32)] * 2,
    )
    return pl.pallas_call(
        functools.partial(_fox_body, t=t),
        grid_spec=grid_spec,
        out_shape=jax.ShapeDtypeStruct((bsz, seq, FOX_W), F32),
        compiler_params=_cparams("parallel", "parallel", "arbitrary"),
        name="fox_prompt_attn",
    )(qi, ki, q, k, v, ft4)


def _page_suffix_body(x_ref, rin_ref, tot_ref):
    x = x_ref[...]
    after = (_iota((PAGE_SIZE, PAGE_SIZE), 0) > _iota((PAGE_SIZE, PAGE_SIZE), 1)).astype(F32)
    rin_ref[...] = _dot_hi(x, after)
    tot_ref[...] = jnp.broadcast_to(jnp.sum(x, axis=1, keepdims=True), x.shape)


def page_suffix(lft):
    r = lft.shape[0]
    tr = _pick(r, (2048, 1024, 512, 256, 128, 64, 32, 16))
    spec = pl.BlockSpec((tr, PAGE_SIZE), lambda i: (i, 0))
    return pl.pallas_call(
        _page_suffix_body,
        grid=(r // tr,),
        in_specs=[spec],
        out_specs=[spec, spec],
        out_shape=[jax.ShapeDtypeStruct(lft.shape, F32)] * 2,
        compiler_params=_cparams("parallel"),
        name="page_suffix",
    )(lft)


DEC_PAGES = 8


def _rep_rows(x, n):
    r, c = x.shape
    return jnp.broadcast_to(x[:, None, :], (r, n, c)).reshape(r * n, c)


def _fox_decode_body(pt_ref, q_ref, *refs, nt, nsteps):
    k_refs = refs[0:DEC_PAGES]
    v_refs = refs[DEC_PAGES:2 * DEC_PAGES]
    rin_refs = refs[2 * DEC_PAGES:3 * DEC_PAGES]
    tot_refs = refs[3 * DEC_PAGES:4 * DEC_PAGES]
    kn_ref, vn_ref, lfn_ref, o_ref, qbd_sc, m_sc, l_sc, acc_sc, run_sc = refs[4 * DEC_PAGES:]
    j = pl.program_id(1)
    rows = FOX_HEADS * nt

    @pl.when(j == 0)
    def _():
        qt = jnp.concatenate([q_ref[0] * (FOX_HEAD_DIM ** -0.5)] * FOX_HEADS, axis=0)
        own = _iota((rows, FOX_W), 1) // FOX_HEAD_DIM == _iota((rows, FOX_W), 0) // nt
        qbd_sc[...] = jnp.where(own, qt, 0.0).astype(BF16)
        m_sc[...] = jnp.full(m_sc.shape, -jnp.inf, F32)
        l_sc[...] = jnp.zeros_like(l_sc)
        acc_sc[...] = jnp.zeros_like(acc_sc)
        run_sc[...] = jnp.zeros_like(run_sc)

    def update(ss, vals, transposed):
        m_prev = m_sc[...]
        m_new = m_prev
        for s in ss:
            m_new = jnp.maximum(m_new, jnp.max(s, axis=-1, keepdims=True))
        alpha = jnp.exp(m_prev - m_new)
        l_new = alpha * l_sc[...]
        acc = jnp.concatenate([alpha] * (FOX_W // LANES), axis=1) * acc_sc[...]
        for s, val in zip(ss, vals):
            p = jnp.exp(s - m_new)
            l_new = l_new + jnp.sum(p, axis=-1, keepdims=True)
            acc = acc + (_dot_nt(p.astype(BF16), val) if transposed else _dot(p.astype(BF16), val))
        m_sc[...] = m_new
        l_sc[...] = l_new
        acc_sc[...] = acc

    @pl.when(j < nsteps)
    def _():
        qbd = qbd_sc[...]
        raw = [_dot(qbd, k_refs[i][0, 0].reshape(FOX_W, PAGE_SIZE).astype(BF16)) for i in range(DEC_PAGES)]
        run = run_sc[...]
        ss = []
        for i in range(DEC_PAGES):
            ss.append(raw[i] + _rep_rows(rin_refs[i][0] + run, nt))
            run = run + tot_refs[i][0]
        run_sc[...] = run
        update(ss, [v_refs[i][0, 0].reshape(FOX_W, PAGE_SIZE).astype(BF16) for i in range(DEC_PAGES)], True)

    @pl.when(j == nsteps)
    def _():
        pad = jnp.zeros((PAGE_SIZE - nt, FOX_W), F32)
        kb = jnp.concatenate([kn_ref[0], pad], axis=0).astype(BF16)
        vb = jnp.concatenate([vn_ref[0], pad], axis=0).astype(BF16)
        s = _dot_nt(qbd_sc[...], kb)
        incl = (_iota((PAGE_SIZE, PAGE_SIZE), 0) <= _iota((PAGE_SIZE, PAGE_SIZE), 1)).astype(F32)
        fn = _dot_hi(lfn_ref[0], incl)
        tok = _iota((rows, PAGE_SIZE), 0) % nt
        update([jnp.where(_iota((rows, PAGE_SIZE), 1) <= tok, s - _rep_rows(fn, nt), -jnp.inf)], [vb], False)
        o = acc_sc[...] / jnp.concatenate([l_sc[...]] * (FOX_W // LANES), axis=1)
        lane_head = _iota((nt, FOX_W), 1) // FOX_HEAD_DIM
        out = jnp.zeros((nt, FOX_W), F32)
        for hd in range(FOX_HEADS):
            out = out + jnp.where(lane_head == hd, o[hd * nt:(hd + 1) * nt, :], 0.0)
        o_ref[0] = out


def fox_decode_attn(q, k_new, v_new, lfn_t, k_cache_t, v_cache_t, layer, rin, tot, page_table):
    bsz, nt, _ = q.shape
    npages = page_table.shape[1]
    assert npages % DEC_PAGES == 0
    nsteps = npages // DEC_PAGES
    rows = FOX_HEADS * nt

    def page(i):
        return lambda b, j, pt: pt[b, npages - 1 - (jnp.minimum(j, nsteps - 1) * DEC_PAGES + i)]

    cache_specs = [pl.BlockSpec((1, 1, FOX_HEADS, FOX_HEAD_DIM, PAGE_SIZE),
                                functools.partial(lambda b, j, pt, pg: (layer, pg(b, j, pt), 0, 0, 0), pg=page(i)))
                   for i in range(DEC_PAGES)]
    bias_specs = [pl.BlockSpec((1, FOX_HEADS, PAGE_SIZE), functools.partial(lambda b, j, pt, pg: (pg(b, j, pt), 0, 0), pg=page(i)))
                  for i in range(DEC_PAGES)]
    per_seq = lambda shape: pl.BlockSpec(shape, lambda b, j, pt: (b, 0, 0))
    grid_spec = pltpu.PrefetchScalarGridSpec(
        num_scalar_prefetch=1,
        grid=(bsz, nsteps + 1),
        in_specs=[per_seq((1, nt, FOX_W))] + cache_specs + cache_specs + bias_specs + bias_specs
        + [per_seq((1, nt, FOX_W)), per_seq((1, nt, FOX_W)), per_seq((1, FOX_HEADS, PAGE_SIZE))],
        out_specs=per_seq((1, nt, FOX_W)),
        scratch_shapes=[
            pltpu.VMEM((rows, FOX_W), BF16),
            pltpu.VMEM((rows, LANES), F32),
            pltpu.VMEM((rows, LANES), F32),
            pltpu.VMEM((rows, FOX_W), F32),
            pltpu.VMEM((FOX_HEADS, PAGE_SIZE), F32),
        ],
    )
    return pl.pallas_call(
        functools.partial(_fox_decode_body, nt=nt, nsteps=nsteps),
        grid_spec=grid_spec,
        out_shape=jax.ShapeDtypeStruct((bsz, nt, FOX_W), F32),
        compiler_params=_cparams("parallel", "arbitrary"),
        name="fox_decode_attn",
    )(page_table, q, *([k_cache_t] * DEC_PAGES), *([v_cache_t] * DEC_PAGES), *([rin] * DEC_PAGES),
      *([tot] * DEC_PAGES), k_new, v_new, lfn_t)


def _split_hyb_weights(w_in):
    o = 0
    parts = {}
    for name, width in (("z", SSD_INNER), ("xbc", SSD_CONV_DIM), ("dt", SSD_HEADS), ("qkv", GDN_CONV_DIM),
                        ("gate", GDN_VW), ("b", GDN_HEADS), ("a", GDN_HEADS)):
        parts[name] = w_in[:, o:o + width]
        o += width
    small = jnp.concatenate([parts["dt"], parts["b"], parts["a"]], axis=1)
    small = jnp.pad(small, ((0, 0), (0, LANES - small.shape[1])))
    return {k: parts[k].astype(BF16) for k in ("z", "xbc", "qkv", "gate")} | {"small": small.astype(BF16)}


def _hybrid_layer(h, e, W, st):
    bsz, seq, d = h.shape
    m = bsz * seq
    h2 = h.reshape(m, d)
    wp = _split_hyb_weights(W["w_in_hyb"][e])
    gain = W["norm_mix"][2 * e]
    proj = {k: fused_linear([h2], [wp[k]], gain=gain, name="hyb_in_" + k) for k in ("z", "xbc", "qkv", "gate", "small")}
    xbc = proj["xbc"].reshape(bsz, seq, SSD_CONV_DIM)
    qkv = proj["qkv"].reshape(bsz, seq, GDN_CONV_DIM)
    small = proj["small"].reshape(bsz, seq, LANES)
    xbc_c = conv_silu(xbc, st["ssd_conv"], W["ssd_conv_w"][e], W["ssd_conv_b"][e])
    qkv_c = conv_silu(qkv, st["gdn_conv"], W["gdn_conv_w"][e], jnp.zeros((GDN_CONV_DIM,), F32))
    y, ssd_h = ssd_scan(xbc_c, proj["z"].reshape(bsz, seq, SSD_INNER), small, st["ssd"],
                        W["ssd_dt_bias"][e], W["ssd_A_log"][e], W["ssd_D"][e], W["ssd_norm"][e])
    o, gdn_s = gdn_scan(qkv_c, proj["gate"].reshape(bsz, seq, GDN_VW), small, st["gdn"],
                        W["gdn_dt_bias"][e], W["gdn_A_log"][e], W["gdn_norm"][e])
    w_out = W["w_out_hyb"][e].astype(BF16)
    h_new = fused_linear([y.reshape(m, SSD_INNER), o.reshape(m, GDN_VW)], [w_out[:SSD_INNER], w_out[SSD_INNER:]],
                         residual=h2, name="hyb_out")
    new = dict(ssd=ssd_h, ssd_conv=xbc[:, seq - (CONV_K - 1):], gdn=gdn_s, gdn_conv=qkv[:, seq - (CONV_K - 1):])
    return h_new.reshape(bsz, seq, d), new


def _fox_layer(h, o_idx, layer, W, st, prompt):
    bsz, seq, d = h.shape
    m = bsz * seq
    h2 = h.reshape(m, d)
    w_in = W["w_in_fox"][o_idx]
    gain = W["norm_mix"][layer]
    wq, wk, wv = (w_in[:, i * FOX_W:(i + 1) * FOX_W].astype(BF16) for i in range(3))
    wf = jnp.pad(w_in[:, 3 * FOX_W:], ((0, 0), (0, LANES - FOX_HEADS))).astype(BF16)
    bf = jnp.pad(W["b_fox_f"][o_idx], (0, LANES - FOX_HEADS))
    q = fused_linear([h2], [wq], gain=gain, name="fox_q").reshape(bsz, seq, FOX_W)
    lf = fused_linear([h2], [wf], gain=gain, bias=bf, act="log_sigmoid", name="fox_f")[:, :FOX_HEADS]
    lf = lf.reshape(bsz, seq, FOX_HEADS)
    lf_t = jnp.swapaxes(lf, 1, 2)
    shp = (bsz, seq, FOX_HEADS, FOX_HEAD_DIM)
    if prompt:
        k_t = fused_linear([h2], [wk], gain=gain, name="fox_k", seq_t=seq)
        v_t = fused_linear([h2], [wv], gain=gain, name="fox_v", seq_t=seq)
        att = fox_prompt_attn(q, k_t, v_t, cumsum_lanes(lf_t))
        k_out = k_t.reshape(bsz, FOX_HEADS, FOX_HEAD_DIM, seq).transpose(0, 3, 1, 2)
        v_out = v_t.reshape(bsz, FOX_HEADS, FOX_HEAD_DIM, seq).transpose(0, 3, 1, 2)
    else:
        k = fused_linear([h2], [wk], gain=gain, name="fox_k").reshape(bsz, seq, FOX_W)
        v = fused_linear([h2], [wv], gain=gain, name="fox_v").reshape(bsz, seq, FOX_W)
        k_out, v_out = k.reshape(shp), v.reshape(shp)
        n_pool = st["cache_fox_k"].shape[1]
        lfc_t = jnp.swapaxes(st["cache_fox_lf"][o_idx].astype(F32), 1, 2).reshape(n_pool * FOX_HEADS, PAGE_SIZE)
        rin, tot = page_suffix(lfc_t)
        rin = rin.reshape(n_pool, FOX_HEADS, PAGE_SIZE)
        tot = tot.reshape(n_pool, FOX_HEADS, PAGE_SIZE)
        lfn_t = jnp.pad(lf_t, ((0, 0), (0, 0), (0, PAGE_SIZE - seq)))
        kc_t = jnp.transpose(st["cache_fox_k"], (0, 1, 3, 4, 2))
        vc_t = jnp.transpose(st["cache_fox_v"], (0, 1, 3, 4, 2))
        att = fox_decode_attn(q, k, v, lfn_t, kc_t, vc_t, o_idx, rin, tot, st["page_table"])
    h_new = fused_linear([att.reshape(m, FOX_W)], [W["w_out_fox"][o_idx].astype(BF16)], residual=h2, name="fox_out")
    return h_new.reshape(bsz, seq, d), dict(fox_k=k_out, fox_v=v_out, fox_lf=lf)


def _run_group(x, W, st, prompt):
    bsz, seq, d = x.shape
    depth = W["norm_mix"].shape[0]
    names = ("ssd", "ssd_conv", "gdn", "gdn_conv", "fox_k", "fox_v", "fox_lf", "mem_k", "mem_v")
    out = {n: [] for n in names}
    h = x
    for layer in range(depth):
        if layer % 2 == 0:
            e = layer // 2
            if prompt:
                s0 = dict(ssd=jnp.zeros((bsz, SSD_HEADS, SSD_HEAD_DIM, SSD_STATE), F32),
                          ssd_conv=jnp.zeros((bsz, CONV_K - 1, SSD_CONV_DIM), F32),
                          gdn=jnp.zeros((bsz, GDN_HEADS, GDN_DK, GDN_DV), F32),
                          gdn_conv=jnp.zeros((bsz, CONV_K - 1, GDN_CONV_DIM), F32))
            else:
                s0 = dict(ssd=st["state_ssd"][e], ssd_conv=st["state_ssd_conv"][e],
                          gdn=st["state_gdn"][e], gdn_conv=st["state_gdn_conv"][e])
            h, new = _hybrid_layer(h, e, W, s0)
        else:
            h, new = _fox_layer(h, layer // 2, layer, W, st, prompt)
        for n, val in new.items():
            out[n].append(val)
        if prompt:
            mem = st["mem"]
            mem2 = mem.reshape(bsz * MEM_LEN, d)
            mk = fused_linear([mem2], [W["wk_x"][layer].astype(BF16)], gain=W["norm_mem"][layer], name="mem_k")
            mv = fused_linear([mem2], [W["wv_x"][layer].astype(BF16)], gain=W["norm_mem"][layer], name="mem_v")
            mk = mk.reshape(bsz, MEM_LEN, X_W)
            mv = mv.reshape(bsz, MEM_LEN, X_W)
            out["mem_k"].append(mk.reshape(bsz, MEM_LEN, X_HEADS, X_HEAD_DIM))
            out["mem_v"].append(mv.reshape(bsz, MEM_LEN, X_HEADS, X_HEAD_DIM))
        else:
            mk, mv = st["cache_mem_k"], st["cache_mem_v"]
        h = cross_attn(h, W["norm_x"][layer], W["wq_x"][layer].astype(BF16), mk, mv, W["wo_x"][layer].astype(BF16),
                       mem_layer=None if prompt else layer)
        h = swiglu(h.reshape(bsz * seq, d), W["norm_ffn"][layer], W["w1_b"], W["w3_b"], W["w2_b"], layer,
                   final_gain=W["norm_final"] if layer == depth - 1 else None).reshape(bsz, seq, d)
    new = {n: jnp.stack(out[n]) for n in names if out[n]}
    return h, new


def kernel(x_prompt, x_sample, mem_prompt, state_ssd, state_ssd_conv, state_gdn, state_gdn_conv, cache_fox_k, cache_fox_v, cache_fox_lf, page_table, cache_mem_k, cache_mem_v, norm_mix, norm_x, norm_mem, norm_ffn, norm_final, w_in_hyb, w_out_hyb, ssd_conv_w, ssd_conv_b, ssd_dt_bias, ssd_A_log, ssd_D, ssd_norm, gdn_conv_w, gdn_dt_bias, gdn_A_log, gdn_norm, w_in_fox, b_fox_f, w_out_fox, wq_x, wk_x, wv_x, wo_x, w1, w3, w2):
    W = dict(norm_mix=norm_mix, norm_x=norm_x, norm_mem=norm_mem, norm_ffn=norm_ffn, norm_final=norm_final,
             w_in_hyb=w_in_hyb, w_out_hyb=w_out_hyb, ssd_conv_w=ssd_conv_w, ssd_conv_b=ssd_conv_b,
             ssd_dt_bias=ssd_dt_bias, ssd_A_log=ssd_A_log, ssd_D=ssd_D, ssd_norm=ssd_norm,
             gdn_conv_w=gdn_conv_w, gdn_dt_bias=gdn_dt_bias, gdn_A_log=gdn_A_log, gdn_norm=gdn_norm,
             w_in_fox=w_in_fox, b_fox_f=b_fox_f, w_out_fox=w_out_fox,
             wq_x=wq_x, wk_x=wk_x, wv_x=wv_x, wo_x=wo_x,
             w1_b=w1.astype(BF16), w3_b=w3.astype(BF16), w2_b=w2.astype(BF16))
    y_prompt, pn = _run_group(x_prompt, W, dict(mem=mem_prompt), True)
    st = dict(state_ssd=state_ssd, state_ssd_conv=state_ssd_conv, state_gdn=state_gdn,
              state_gdn_conv=state_gdn_conv, cache_fox_k=cache_fox_k, cache_fox_v=cache_fox_v,
              cache_fox_lf=cache_fox_lf, page_table=page_table, cache_mem_k=cache_mem_k, cache_mem_v=cache_mem_v)
    y_sample, sn = _run_group(x_sample, W, st, False)
    return (y_prompt, y_sample,
            pn["ssd"], pn["ssd_conv"], pn["gdn"], pn["gdn_conv"],
            pn["fox_k"], pn["fox_v"], pn["fox_lf"], pn["mem_k"], pn["mem_v"],
            sn["ssd"], sn["ssd_conv"], sn["gdn"], sn["gdn_conv"],
            sn["fox_k"], sn["fox_v"], sn["fox_lf"])
```

```python
import functools
import math

import jax
import jax.numpy as jnp
from jax import lax
from jax.experimental import pallas as pl
from jax.experimental.pallas import tpu as pltpu

F32 = jnp.float32
BF16 = jnp.bfloat16
HI = lax.Precision.HIGHEST
NT_DIMS = (((1,), (1,)), ((), ()))
TN_DIMS = (((0,), (0,)), ((), ()))

D_MODEL = 1024
EPS = 1e-6
CONV_K = 4
SSD_HEADS = 16
SSD_HEAD_DIM = 64
SSD_INNER = SSD_HEADS * SSD_HEAD_DIM
SSD_GROUPS = 2
SSD_STATE = 64
SSD_CONV_DIM = SSD_INNER + 2 * SSD_GROUPS * SSD_STATE
GDN_HEADS = 8
GDN_DK = 128
GDN_DV = 128
GDN_QK = GDN_HEADS * GDN_DK
GDN_VW = GDN_HEADS * GDN_DV
GDN_CONV_DIM = 2 * GDN_QK + GDN_VW
FOX_HEADS = 16
FOX_HEAD_DIM = 64
FOX_W = FOX_HEADS * FOX_HEAD_DIM
PAGE_SIZE = 128
MEM_LEN = 256
X_HEADS = 4
X_HEAD_DIM = 128
X_W = X_HEADS * X_HEAD_DIM
LANES = 128
VMEM_LIMIT = 56 * 1024 * 1024


def _cparams(*sem):
    return pltpu.CompilerParams(dimension_semantics=sem, vmem_limit_bytes=VMEM_LIMIT)


def _pick(n, cands):
    for c in cands:
        if n % c == 0:
            return c
    return n


def _softplus(x):
    return jnp.maximum(x, 0.0) + jnp.log1p(jnp.exp(-jnp.abs(x)))


def _log_sigmoid(x):
    return jnp.minimum(x, 0.0) - jnp.log1p(jnp.exp(-jnp.abs(x)))


def _sigmoid(x):
    return 1.0 / (1.0 + jnp.exp(-x))


def _silu(x):
    return x * _sigmoid(x)


def _rms_rows(x):
    return x * lax.rsqrt(jnp.mean(x * x, axis=-1, keepdims=True) + EPS)


def _dot(a, b):
    return jnp.dot(a, b, preferred_element_type=F32)


def _dot_hi(a, b):
    return jnp.dot(a, b, preferred_element_type=F32, precision=HI)


def _dot_nt(a, b):
    return lax.dot_general(a, b, NT_DIMS, preferred_element_type=F32)


def _dot_tn(a, b):
    return lax.dot_general(a, b, TN_DIMS, preferred_element_type=F32)


def _iota(shape, dim):
    return lax.broadcasted_iota(jnp.int32, shape, dim)


def _linear_body(*refs, n_x, use_norm, act, has_res, transposed, side):
    it = iter(refs)
    x_refs = [next(it) for _ in range(n_x)]
    g_ref = next(it) if use_norm else None
    w_refs = [next(it) for _ in range(n_x)]
    b_ref = next(it) if act else None
    r_ref = next(it) if has_res else None
    ws_ref = next(it) if side else None
    bs_ref = next(it) if side == "log_sigmoid" else None
    o_ref = next(it)
    os_ref = next(it) if side else None
    xs_refs = [next(it) for _ in range(n_x)]

    @pl.when(pl.program_id(1) == 0)
    def _():
        for i in range(n_x):
            x = x_refs[i][...]
            if use_norm and i == 0:
                x = _rms_rows(x) * g_ref[...]
            xs_refs[i][...] = x.astype(BF16)
        if side:
            extra = _dot(xs_refs[0][...], ws_ref[...])
            os_ref[...] = _log_sigmoid(extra + bs_ref[...]) if side == "log_sigmoid" else extra

    acc = _dot(xs_refs[0][...], w_refs[0][...])
    for i in range(1, n_x):
        acc = acc + _dot(xs_refs[i][...], w_refs[i][...])
    if act == "log_sigmoid":
        acc = _log_sigmoid(acc + b_ref[...])
    if has_res:
        acc = acc + r_ref[...]
    if transposed:
        o_ref[0] = acc.T
    else:
        o_ref[...] = acc


def fused_linear(xs, ws, gain=None, bias=None, act=None, residual=None, name="linear", seq_t=None,
                 side_w=None, side_bias=None):
    m = xs[0].shape[0]
    n = ws[0].shape[1]
    tm = _pick(seq_t or m, (1024, 512, 256, 128, 64, 32, 16, 8))
    tn = _pick(n, (1536, 1280, 1024, 768, 640, 512, 384, 256, 128))
    n_x = len(xs)
    in_specs, args = [], []
    for x in xs:
        in_specs.append(pl.BlockSpec((tm, x.shape[1]), lambda i, j: (i, 0)))
        args.append(x)
    if gain is not None:
        in_specs.append(pl.BlockSpec((1, xs[0].shape[1]), lambda i, j: (0, 0)))
        args.append(gain.reshape(1, -1).astype(F32))
    for w in ws:
        in_specs.append(pl.BlockSpec((w.shape[0], tn), lambda i, j: (0, j)))
        args.append(w)
    if act:
        in_specs.append(pl.BlockSpec((1, tn), lambda i, j: (0, j)))
        args.append(bias.reshape(1, -1).astype(F32))
    if residual is not None:
        in_specs.append(pl.BlockSpec((tm, tn), lambda i, j: (i, j)))
        args.append(residual)
    side = None
    if side_w is not None:
        side = "log_sigmoid" if side_bias is not None else "plain"
        in_specs.append(pl.BlockSpec(side_w.shape, lambda i, j: (0, 0)))
        args.append(side_w)
        if side_bias is not None:
            in_specs.append(pl.BlockSpec((1, LANES), lambda i, j: (0, 0)))
            args.append(side_bias.reshape(1, LANES).astype(F32))
    body = functools.partial(_linear_body, n_x=n_x, use_norm=gain is not None, act=act,
                             has_res=residual is not None, transposed=seq_t is not None, side=side)
    if seq_t is None:
        out_spec = pl.BlockSpec((tm, tn), lambda i, j: (i, j))
        out_shape = jax.ShapeDtypeStruct((m, n), F32)
    else:
        per_seq = seq_t // tm
        out_spec = pl.BlockSpec((1, tn, tm), lambda i, j: (i // per_seq, j, i % per_seq))
        out_shape = jax.ShapeDtypeStruct((m // seq_t, n, seq_t), F32)
    if side:
        out_spec = [out_spec, pl.BlockSpec((tm, LANES), lambda i, j: (i, 0))]
        out_shape = [out_shape, jax.ShapeDtypeStruct((m, LANES), F32)]
    return pl.pallas_call(
        body,
        grid=(m // tm, n // tn),
        in_specs=in_specs,
        out_specs=out_spec,
        out_shape=out_shape,
        scratch_shapes=[pltpu.VMEM((tm, x.shape[1]), BF16) for x in xs],
        compiler_params=_cparams("parallel", "arbitrary"),
        name=name,
    )(*args)


SUB = 8


def _conv_body(x_ref, prev_ref, buf_ref, w_ref, b_ref, o_ref):
    first = pl.program_id(1) == 0
    ext = jnp.concatenate([jnp.where(first, buf_ref[0], prev_ref[0]), x_ref[0]], axis=0)
    w = w_ref[...]

    def tap(j):
        back = CONV_K - 1 - j
        shifted = pltpu.roll(ext, back, axis=0) if back else ext
        return shifted[SUB:, :] * w[j:j + 1, :]

    y = tap(0)
    for j in range(1, CONV_K):
        y = y + tap(j)
    y = y + b_ref[...]
    o_ref[0] = _silu(y)


def conv_silu(x, buf, w, b):
    bsz, seq, ch = x.shape
    tt = _pick(seq, (512, 256, 128, 64, 32, 16, 8))
    tc = ch if tt <= 64 else _pick(ch, (1024, 640, 512, 256, 128))
    bufp = jnp.concatenate([jnp.zeros((bsz, SUB - (CONV_K - 1), ch), F32), buf], axis=1)
    wp = jnp.concatenate([w, jnp.zeros((SUB - CONV_K, ch), F32)], axis=0)
    nsub = tt // SUB
    return pl.pallas_call(
        _conv_body,
        grid=(bsz, seq // tt, ch // tc),
        in_specs=[
            pl.BlockSpec((1, tt, tc), lambda bi, ti, ci: (bi, ti, ci)),
            pl.BlockSpec((1, SUB, tc), lambda bi, ti, ci: (bi, jnp.maximum(ti * nsub - 1, 0), ci)),
            pl.BlockSpec((1, SUB, tc), lambda bi, ti, ci: (bi, 0, ci)),
            pl.BlockSpec((SUB, tc), lambda bi, ti, ci: (0, ci)),
            pl.BlockSpec((1, tc), lambda bi, ti, ci: (0, ci)),
        ],
        out_specs=pl.BlockSpec((1, tt, tc), lambda bi, ti, ci: (bi, ti, ci)),
        out_shape=jax.ShapeDtypeStruct((bsz, seq, ch), F32),
        compiler_params=_cparams("parallel", "parallel", "parallel"),
        name="conv_silu",
    )(x, x, bufp, wp, b.reshape(1, ch))


HG = SSD_HEADS // SSD_GROUPS
GW = HG * SSD_HEAD_DIM


def _expand_heads(v, e_bf16):
    hi = v.astype(BF16).astype(F32)
    mid = (v - hi).astype(BF16).astype(F32)
    lo = (v - hi - mid).astype(BF16).astype(F32)
    return _dot(jnp.concatenate([hi, mid, lo], axis=1).astype(BF16), jnp.concatenate([e_bf16] * 3, axis=0))


def _ssd_body(xbc_ref, z_ref, small_ref, dtt_ref, h0_ref, prow_ref, pcol_ref, dexp_ref, norm_ref,
              y_ref, hout_ref, h_sc, y_sc, *, q):
    @pl.when(pl.program_id(1) == 0)
    def _():
        h_sc[...] = h0_ref[0]

    xbc = xbc_ref[0]
    xs = xbc[:, :SSD_INNER]
    bm = xbc[:, SSD_INNER:SSD_INNER + LANES]
    cm = xbc[:, SSD_INNER + LANES:SSD_INNER + 2 * LANES]
    dt_c = _softplus(small_ref[0][:, 0:SSD_HEADS] + prow_ref[0:1, 0:SSD_HEADS])
    a_c = dt_c * (-jnp.exp(prow_ref[1:2, 0:SSD_HEADS]))
    dt_t = _softplus(dtt_ref[0] + pcol_ref[:, 0:1])
    a_t = dt_t * (-jnp.exp(pcol_ref[:, 1:2]))
    ri = _iota((q, q), 0)
    ci = _iota((q, q), 1)
    causal = ci <= ri
    cum = _dot_hi(causal.astype(F32), a_c)
    cum_t = _dot_hi(a_t, (ri <= ci).astype(F32))
    e_heads = (_iota((SSD_HEADS, SSD_INNER), 1) // SSD_HEAD_DIM == _iota((SSD_HEADS, SSD_INNER), 0)).astype(BF16)
    dt_x = _expand_heads(dt_c, e_heads)
    ecum_x = _expand_heads(jnp.exp(cum), e_heads)
    wlast_x = _expand_heads(jnp.exp(cum[q - 1:q, :] - cum), e_heads)
    xdt = xs * dt_x
    xw = (xdt * wlast_x).astype(BF16)
    xdt_b = xdt.astype(BF16)

    lane = _iota((1, LANES), 1)
    low = lane < SSD_STATE
    bm_b = bm.astype(BF16)
    hs = h_sc[...]
    hs_b = hs.astype(BF16)
    y_inter = []
    cbs = []
    upd = []
    for g in range(SSD_GROUPS):
        cm_g = jnp.where(low if g == 0 else jnp.logical_not(low), cm, 0.0).astype(BF16)
        cbs.append(_dot_nt(cm_g, bm_b))
        y_inter.append(_dot(cm_g, hs_b))
        upd.append(_dot_tn(bm_b, xw[:, g * GW:(g + 1) * GW]))
    for j in range(SSD_HEADS // 2):
        g = (2 * j) // HG
        xp = xdt_b[:, j * LANES:(j + 1) * LANES]
        ys = []
        for hh in (2 * j, 2 * j + 1):
            seg = cum[:, hh:hh + 1] - cum_t[hh:hh + 1, :]
            lm = jnp.exp(jnp.where(causal, seg, -jnp.inf))
            ys.append(_dot((cbs[g] * lm).astype(BF16), xp))
        y_sc[:, j * LANES:(j + 1) * LANES] = jnp.where(low, ys[0], ys[1])
    y = y_sc[...] + jnp.concatenate(y_inter, axis=1) * ecum_x + dexp_ref[...] * xs
    y = y * _silu(z_ref[0])
    y = jnp.concatenate([_rms_rows(y[:, g * GW:(g + 1) * GW]) for g in range(SSD_GROUPS)], axis=1)
    y_ref[0] = y * norm_ref[...]

    row_low = _iota((2 * SSD_STATE, 1), 0) < SSD_STATE
    ecl = ecum_x[q - 1:q, :]
    decay = jnp.where(row_low, ecl[:, 0:GW], ecl[:, GW:2 * GW])
    h_new = hs * decay + jnp.where(row_low, upd[0], upd[1])
    h_sc[...] = h_new

    @pl.when(pl.program_id(1) == pl.num_programs(1) - 1)
    def _():
        hout_ref[0] = h_new


def ssd_scan(xbc_c, z, small, h0, dt_bias, a_log, d_skip, norm_w):
    bsz, seq, _ = xbc_c.shape
    q = _pick(seq, (128, 64, 32, 16, 8))
    dtt = jnp.swapaxes(small[:, :, 0:SSD_HEADS], 1, 2)
    hs0 = h0.reshape(bsz, SSD_GROUPS, HG, SSD_HEAD_DIM, SSD_STATE).transpose(0, 1, 4, 2, 3)
    hs0 = hs0.reshape(bsz, SSD_GROUPS * SSD_STATE, GW)
    prow = jnp.zeros((SUB, LANES), F32).at[0, :SSD_HEADS].set(dt_bias).at[1, :SSD_HEADS].set(a_log)
    pcol = jnp.zeros((SSD_HEADS, LANES), F32).at[:, 0].set(dt_bias).at[:, 1].set(a_log)
    dexp = jnp.repeat(d_skip, SSD_HEAD_DIM).reshape(1, SSD_INNER)
    y, hs = pl.pallas_call(
        functools.partial(_ssd_body, q=q),
        grid=(bsz, seq // q),
        in_specs=[
            pl.BlockSpec((1, q, SSD_CONV_DIM), lambda b, c: (b, c, 0)),
            pl.BlockSpec((1, q, SSD_INNER), lambda b, c: (b, c, 0)),
            pl.BlockSpec((1, q, LANES), lambda b, c: (b, c, 0)),
            pl.BlockSpec((1, SSD_HEADS, q), lambda b, c: (b, 0, c)),
            pl.BlockSpec((1, 2 * SSD_STATE, GW), lambda b, c: (b, 0, 0)),
            pl.BlockSpec((SUB, LANES), lambda b, c: (0, 0)),
            pl.BlockSpec((SSD_HEADS, LANES), lambda b, c: (0, 0)),
            pl.BlockSpec((1, SSD_INNER), lambda b, c: (0, 0)),
            pl.BlockSpec((1, SSD_INNER), lambda b, c: (0, 0)),
        ],
        out_specs=[
            pl.BlockSpec((1, q, SSD_INNER), lambda b, c: (b, c, 0)),
            pl.BlockSpec((1, 2 * SSD_STATE, GW), lambda b, c: (b, 0, 0)),
        ],
        out_shape=[
            jax.ShapeDtypeStruct((bsz, seq, SSD_INNER), F32),
            jax.ShapeDtypeStruct((bsz, 2 * SSD_STATE, GW), F32),
        ],
        scratch_shapes=[pltpu.VMEM((2 * SSD_STATE, GW), F32), pltpu.VMEM((q, SSD_INNER), F32)],
        compiler_params=_cparams("parallel", "arbitrary"),
        name="ssd_scan",
    )(xbc_c, z, small, dtt, hs0, prow, pcol, dexp, norm_w.reshape(1, SSD_INNER))
    h_new = hs.reshape(bsz, SSD_GROUPS, SSD_STATE, HG, SSD_HEAD_DIM).transpose(0, 1, 3, 4, 2)
    return y, h_new.reshape(bsz, SSD_HEADS, SSD_HEAD_DIM, SSD_STATE)


def _split2(x):
    hi = x.astype(BF16)
    return hi, (x - hi.astype(F32)).astype(BF16)


def _dot3(a, b):
    ah, al = _split2(a)
    bh, bl = _split2(b)
    return _dot(ah, bh) + (_dot(ah, bl) + _dot(al, bh))


def _gdn_prep_body(q_ref, k_ref, v_ref, small_ref, gbt_ref, prow_ref,
                   u_ref, w_ref, qg_ref, kd_ref, qk_ref, eg_ref, *, c, nch, hp):
    lane = _iota((1, LANES), 1)

    def pick_lane(x, idx):
        return jnp.sum(jnp.where(lane == idx, x, 0.0), axis=1, keepdims=True)

    small = small_ref[0]
    beta_col, g_col, g_row = [], [], []
    for i in range(hp):
        h = pl.program_id(1) * hp + i
        neg_a = -jnp.exp(pick_lane(prow_ref[0:1, :], h))
        dtb = pick_lane(prow_ref[1:2, :], h)
        beta_col.append(_sigmoid(pick_lane(small, SSD_HEADS + h)))
        g_col.append(neg_a * _softplus(pick_lane(small, SSD_HEADS + GDN_HEADS + h) + dtb))
        g_row.append(neg_a * _softplus(gbt_ref[0, pl.ds(GDN_HEADS + h, 1), :] + dtb))

    ri = _iota((c, c), 0)
    ci = _iota((c, c), 1)
    tril = ri >= ci
    tril_f = tril.astype(F32)
    triu_f = (ri <= ci).astype(F32)
    eye = (ri == ci).astype(F32)
    units = [(i, ch) for i in range(hp) for ch in range(nch)]
    rows = [slice(ch * c, (ch + 1) * c) for _, ch in units]
    lanes = [slice(i * GDN_DK, (i + 1) * GDN_DK) for i, _ in units]
    nu = range(len(units))
    qn, kn, kb, gam, dec, nmat = [], [], [], [], [], []
    for n in nu:
        i = units[n][0]
        qc = q_ref[0, rows[n], lanes[n]]
        kc = k_ref[0, rows[n], lanes[n]]
        qn.append(qc * lax.rsqrt(jnp.sum(qc * qc, axis=-1, keepdims=True) + 1e-6) * (GDN_DK ** -0.5))
        kn.append(kc * lax.rsqrt(jnp.sum(kc * kc, axis=-1, keepdims=True) + 1e-6))
        kb.append(kn[n] * beta_col[i][rows[n], :])
        gam.append(_dot_hi(tril_f, jnp.broadcast_to(g_col[i][rows[n], :], (c, LANES))))
    for n in nu:
        gam_r = _dot_hi(jnp.broadcast_to(g_row[units[n][0]][:, rows[n]], (SUB, c)), triu_f)[0:1, :]
        dec.append(jnp.exp(jnp.where(tril, gam[n][:, 0:c] - gam_r, -jnp.inf)))
        nmat.append(jnp.where(ri > ci, _dot_nt(kb[n].astype(BF16), kn[n].astype(BF16)) * dec[n], 0.0) * -1.0)
    tinv = [eye + m for m in nmat]
    npow = [_dot3(m, m) for m in nmat]
    levels = int(math.log2(c))
    for lvl in range(1, levels):
        for n in nu:
            if lvl == levels - 1:
                tinv[n] = tinv[n] + _dot3(npow[n], tinv[n])
            else:
                both = _dot3(npow[n], jnp.concatenate([tinv[n], npow[n]], axis=1))
                tinv[n] = tinv[n] + both[:, :c]
                npow[n] = both[:, c:]
    for n in nu:
        i, ch = units[n]
        egam = jnp.exp(gam[n])
        vb = (v_ref[0, rows[n], lanes[n]] * beta_col[i][rows[n], :]).astype(BF16)
        uw = _dot(tinv[n].astype(BF16), jnp.concatenate([vb, (kb[n] * egam).astype(BF16)], axis=1))
        glast = gam[n][c - 1:c, :]
        u_ref[0, rows[n], lanes[n]] = uw[:, :GDN_DV]
        w_ref[0, rows[n], lanes[n]] = uw[:, GDN_DV:].astype(w_ref.dtype)
        qg_ref[0, rows[n], lanes[n]] = (qn[n] * egam).astype(qg_ref.dtype)
        kd_ref[0, rows[n], lanes[n]] = (kn[n] * jnp.exp(glast - gam[n])).astype(kd_ref.dtype)
        qk_ref[0, i, rows[n], :] = (_dot_nt(qn[n].astype(BF16), kn[n].astype(BF16)) * dec[n]).astype(qk_ref.dtype)
        eg_ref[0, i, ch:ch + 1, :] = jnp.exp(glast)


GDN_GROUP = 8


def _gdn_scan_body(u_ref, w_ref, qg_ref, kd_ref, qk_ref, eg_ref, gate_ref, s0_ref, norm_ref,
                   o_ref, sout_ref, s_sc, *, c, nch):
    @pl.when(pl.program_id(2) == 0)
    def _():
        s_sc[...] = s0_ref[0]

    s = [s_sc[i] for i in range(GDN_GROUP)]
    heads = range(GDN_GROUP)
    lanes = [slice(i * GDN_DV, (i + 1) * GDN_DV) for i in heads]
    for ch in range(nch):
        rows = slice(ch * c, (ch + 1) * c)
        s_b = [s[i].astype(BF16) for i in heads]
        ws = [_dot(jnp.concatenate([w_ref[0, rows, lanes[i]].astype(BF16), qg_ref[0, rows, lanes[i]].astype(BF16)],
                                   axis=0), s_b[i]) for i in heads]
        vn_b = [(u_ref[0, rows, lanes[i]] - ws[i][:c]).astype(BF16) for i in heads]
        o = [ws[i][c:] + _dot(qk_ref[0, i, rows, :].astype(BF16), vn_b[i]) for i in heads]
        s = [s[i] * eg_ref[0, i, ch:ch + 1, :] + _dot_tn(kd_ref[0, rows, lanes[i]].astype(BF16), vn_b[i]) for i in heads]
        for i in heads:
            o_ref[0, rows, lanes[i]] = _rms_rows(o[i]) * norm_ref[...] * _silu(gate_ref[0, rows, lanes[i]])
    for i in range(GDN_GROUP):
        s_sc[i] = s[i]

    @pl.when(pl.program_id(2) == pl.num_programs(2) - 1)
    def _():
        for i in range(GDN_GROUP):
            sout_ref[0, i] = s[i]


def gdn_scan(qkv_c, gate, small, s0, dt_bias, a_log, norm_w):
    bsz, seq, _ = qkv_c.shape
    c = _pick(seq, (64, 32, 16, 8))
    tb = _pick(seq, (512, 256, 128, 64, 32, 16, 8))
    nch = tb // c
    wdt = BF16 if c % 16 == 0 else F32
    gbt = jnp.swapaxes(small[:, :, SSD_HEADS:SSD_HEADS + 2 * GDN_HEADS], 1, 2)
    prow = jnp.zeros((SUB, LANES), F32).at[0, :GDN_HEADS].set(a_log).at[1, :GDN_HEADS].set(dt_bias)
    hp = 2 if nch >= 4 else GDN_HEADS
    ng = GDN_HEADS // hp
    head_blk = pl.BlockSpec((1, tb, hp * GDN_DV), lambda b, h, t: (b, t, h))
    u, w, qg, kd, qk, eg = pl.pallas_call(
        functools.partial(_gdn_prep_body, c=c, nch=nch, hp=hp),
        grid=(bsz, ng, seq // tb),
        in_specs=[
            pl.BlockSpec((1, tb, hp * GDN_DK), lambda b, h, t: (b, t, h)),
            pl.BlockSpec((1, tb, hp * GDN_DK), lambda b, h, t: (b, t, ng + h)),
            pl.BlockSpec((1, tb, hp * GDN_DV), lambda b, h, t: (b, t, 2 * ng + h)),
            pl.BlockSpec((1, tb, LANES), lambda b, h, t: (b, t, 0)),
            pl.BlockSpec((1, 2 * GDN_HEADS, tb), lambda b, h, t: (b, 0, t)),
            pl.BlockSpec((SUB, LANES), lambda b, h, t: (0, 0)),
        ],
        out_specs=[head_blk, head_blk, head_blk, head_blk,
                   pl.BlockSpec((1, hp, tb, c), lambda b, h, t: (b, h, t, 0)),
                   pl.BlockSpec((1, hp, nch, LANES), lambda b, h, t: (b, h, t, 0))],
        out_shape=[
            jax.ShapeDtypeStruct((bsz, seq, GDN_VW), F32),
            jax.ShapeDtypeStruct((bsz, seq, GDN_VW), wdt),
            jax.ShapeDtypeStruct((bsz, seq, GDN_QK), wdt),
            jax.ShapeDtypeStruct((bsz, seq, GDN_QK), wdt),
            jax.ShapeDtypeStruct((bsz, GDN_HEADS, seq, c), wdt),
            jax.ShapeDtypeStruct((bsz, GDN_HEADS, seq // c, LANES), F32),
        ],
        compiler_params=_cparams("parallel", "parallel", "parallel"),
        name="gdn_prep",
    )(qkv_c, qkv_c, qkv_c, small, gbt, prow)
    gw = GDN_GROUP * GDN_DV
    grp_blk = pl.BlockSpec((1, tb, gw), lambda b, g, t: (b, t, g))
    state_blk = pl.BlockSpec((1, GDN_GROUP, GDN_DK, GDN_DV), lambda b, g, t: (b, g, 0, 0))
    o, s_new = pl.pallas_call(
        functools.partial(_gdn_scan_body, c=c, nch=nch),
        grid=(bsz, GDN_HEADS // GDN_GROUP, seq // tb),
        in_specs=[
            grp_blk, grp_blk, grp_blk, grp_blk,
            pl.BlockSpec((1, GDN_GROUP, tb, c), lambda b, g, t: (b, g, t, 0)),
            pl.BlockSpec((1, GDN_GROUP, nch, LANES), lambda b, g, t: (b, g, t, 0)),
            grp_blk,
            state_blk,
            pl.BlockSpec((1, GDN_DV), lambda b, g, t: (0, 0)),
        ],
        out_specs=[grp_blk, state_blk],
        out_shape=[
            jax.ShapeDtypeStruct((bsz, seq, GDN_VW), F32),
            jax.ShapeDtypeStruct((bsz, GDN_HEADS, GDN_DK, GDN_DV), F32),
        ],
        scratch_shapes=[pltpu.VMEM((GDN_GROUP, GDN_DK, GDN_DV), F32)],
        compiler_params=_cparams("parallel", "parallel", "arbitrary"),
        name="gdn_scan",
    )(u, w, qg, kd, qk, eg, gate, s0, norm_w.reshape(1, GDN_DV))
    return o, s_new


def _xattn_body(h_ref, g_ref, wq_ref, mk_ref, mv_ref, wo_ref, o_ref, *, nb, tm):
    d = h_ref.shape[-1]
    x = h_ref[...].reshape(nb * tm, d)
    xn = (_rms_rows(x) * g_ref[...]).astype(BF16)
    qf = _dot(xn, wq_ref[...])
    stacked = len(mk_ref.shape) == 5
    units = [(bi, hd) for bi in range(nb) for hd in range(X_HEADS)]

    def mem(ref, bi, hd):
        m = ref[0, bi, :, hd, :] if stacked else ref[bi, :, hd * X_HEAD_DIM:(hd + 1) * X_HEAD_DIM]
        return m.astype(BF16)

    ss = [_dot_nt(qf[bi * tm:(bi + 1) * tm, hd * X_HEAD_DIM:(hd + 1) * X_HEAD_DIM].astype(BF16), mem(mk_ref, bi, hd))
          * (X_HEAD_DIM ** -0.5) for bi, hd in units]
    ps = []
    for s in ss:
        p = jnp.exp(s - jnp.max(s, axis=-1, keepdims=True))
        ps.append((p / jnp.sum(p, axis=-1, keepdims=True)).astype(BF16))
    os_ = [_dot(ps[n], mem(mv_ref, bi, hd)) for n, (bi, hd) in enumerate(units)]
    outs = [jnp.concatenate(os_[bi * X_HEADS:(bi + 1) * X_HEADS], axis=1) for bi in range(nb)]
    o = jnp.concatenate(outs, axis=0).astype(BF16)
    o_ref[...] = (x + _dot(o, wo_ref[...])).reshape(nb, tm, d)


def cross_attn(h, gain, wq, mk, mv, wo, mem_layer=None):
    bsz, seq, d = h.shape
    tm = _pick(seq, (512, 256, 128, 64, 32, 16, 8))
    nb = _pick(bsz, (8, 4, 2, 1)) if seq <= 64 else 1
    if mem_layer is None:
        mem_spec = pl.BlockSpec((nb, MEM_LEN, X_W), lambda b, i: (b, 0, 0))
    else:
        mem_spec = pl.BlockSpec((1, nb, MEM_LEN, X_HEADS, X_HEAD_DIM), lambda b, i: (mem_layer, b, 0, 0, 0))
    return pl.pallas_call(
        functools.partial(_xattn_body, nb=nb, tm=tm),
        grid=(bsz // nb, seq // tm),
        in_specs=[
            pl.BlockSpec((nb, tm, d), lambda b, i: (b, i, 0)),
            pl.BlockSpec((1, d), lambda b, i: (0, 0)),
            pl.BlockSpec((d, X_W), lambda b, i: (0, 0)),
            mem_spec,
            mem_spec,
            pl.BlockSpec((X_W, d), lambda b, i: (0, 0)),
        ],
        out_specs=pl.BlockSpec((nb, tm, d), lambda b, i: (b, i, 0)),
        out_shape=jax.ShapeDtypeStruct((bsz, seq, d), F32),
        compiler_params=_cparams("parallel", "parallel"),
        name="cross_attn",
    )(h, gain.reshape(1, d), wq, mk, mv, wo)


def _swiglu_body(h_ref, g_ref, w1_ref, w3_ref, w2_ref, *rest, final_norm):
    gf_ref = rest[0] if final_norm else None
    o_ref, xn_sc = rest[-2:]
    j = pl.program_id(1)

    @pl.when(j == 0)
    def _():
        x = h_ref[...]
        xn_sc[...] = (_rms_rows(x) * g_ref[...]).astype(BF16)
        o_ref[...] = x

    xn = xn_sc[...]
    a = _dot(xn, w1_ref[0])
    b = _dot(xn, w3_ref[0])
    o_ref[...] += _dot((_silu(a) * b).astype(BF16), w2_ref[0])

    if final_norm:
        @pl.when(j == pl.num_programs(1) - 1)
        def _():
            o_ref[...] = _rms_rows(o_ref[...]) * gf_ref[...]


def swiglu(h, gain, w1, w3, w2, layer, final_gain=None):
    m, d = h.shape
    ff = w1.shape[2]
    tm = _pick(m, (512, 256, 128, 64, 32, 16, 8))
    tf = _pick(ff, (1408, 1024, 512, 256, 128))
    in_specs = [
        pl.BlockSpec((tm, d), lambda i, j: (i, 0)),
        pl.BlockSpec((1, d), lambda i, j: (0, 0)),
        pl.BlockSpec((1, d, tf), lambda i, j: (layer, 0, j)),
        pl.BlockSpec((1, d, tf), lambda i, j: (layer, 0, j)),
        pl.BlockSpec((1, tf, d), lambda i, j: (layer, j, 0)),
    ]
    args = [h, gain.reshape(1, d), w1, w3, w2]
    if final_gain is not None:
        in_specs.append(pl.BlockSpec((1, d), lambda i, j: (0, 0)))
        args.append(final_gain.reshape(1, d))
    return pl.pallas_call(
        functools.partial(_swiglu_body, final_norm=final_gain is not None),
        grid=(m // tm, ff // tf),
        in_specs=in_specs,
        out_specs=pl.BlockSpec((tm, d), lambda i, j: (i, 0)),
        out_shape=jax.ShapeDtypeStruct((m, d), F32),
        scratch_shapes=[pltpu.VMEM((tm, d), BF16)],
        compiler_params=_cparams("parallel", "arbitrary"),
        name="swiglu",
    )(*args)


def _cumsum_lanes_body(x_ref, o_ref, carry_sc, *, tc):
    @pl.when(pl.program_id(1) == 0)
    def _():
        carry_sc[...] = jnp.zeros_like(carry_sc)

    upper = (_iota((tc, tc), 0) <= _iota((tc, tc), 1)).astype(F32)
    f = _dot_hi(x_ref[0], upper) + carry_sc[:, 0:1]
    o_ref[0] = f
    carry_sc[...] = jnp.broadcast_to(f[:, tc - 1:tc], carry_sc.shape)


def cumsum_lanes(x):
    bsz, r, seq = x.shape
    tc = _pick(seq, (512, 256, 128))
    return pl.pallas_call(
        functools.partial(_cumsum_lanes_body, tc=tc),
        grid=(bsz, seq // tc),
        in_specs=[pl.BlockSpec((1, r, tc), lambda b, i: (b, 0, i))],
        out_specs=pl.BlockSpec((1, r, tc), lambda b, i: (b, 0, i)),
        out_shape=jax.ShapeDtypeStruct((bsz, r, seq), F32),
        scratch_shapes=[pltpu.VMEM((r, LANES), F32)],
        compiler_params=_cparams("parallel", "arbitrary"),
        name="cumsum_lanes",
    )(x)


FOX_ROWS = 512
FOX_PAIRS = 2


LOG2E = 1.4426950408889634


def _fox_body(qi_ref, ki_ref, q_ref, k_ref, v_ref, f_ref, o_ref, m_sc, acc_sc, *, t):
    step = pl.program_id(2)
    qi = qi_ref[step]
    ki = ki_ref[step]
    low = _iota((1, LANES), 1) < FOX_HEAD_DIM
    rb = min(FOX_ROWS, t)

    @pl.when(ki == 0)
    def _():
        m_sc[...] = jnp.full(m_sc.shape, -jnp.inf, F32)
        acc_sc[...] = jnp.zeros_like(acc_sc)

    def update(masked):
        units = [(pr, r, hh) for pr in range(FOX_PAIRS) for r in range(t // rb) for hh in range(2)]
        kbs, vbs = {}, {}

        def nkeys(r):
            return (r + 1) * rb if masked else t

        def qk(u):
            pr, r, hh = u
            lanes = slice(pr * LANES, (pr + 1) * LANES)
            if pr not in kbs:
                kbs[pr] = k_ref[0, lanes, :].astype(BF16)
                v = v_ref[0, lanes, :]
                first = _iota((LANES, 1), 0) < FOX_HEAD_DIM
                vbs[pr] = (jnp.where(first, v, 1.0).astype(BF16), jnp.where(first, 1.0, v).astype(BF16))
            qs = q_ref[0, r * rb:(r + 1) * rb, lanes] * (FOX_HEAD_DIM ** -0.5 * LOG2E)
            qm = jnp.where(low if hh == 0 else jnp.logical_not(low), qs, 0.0).astype(BF16)
            return _dot(qm, kbs[pr][:, :nkeys(r)])

        s_next = qk(units[0])
        for i, (pr, r, hh) in enumerate(units):
            rows = slice(r * rb, (r + 1) * rb)
            nk = nkeys(r)
            s = s_next - f_ref[0, pr, hh:hh + 1, :nk] * LOG2E
            if i + 1 < len(units):
                s_next = qk(units[i + 1])
            if masked:
                s = jnp.where(_iota((rb, nk), 1) <= _iota((rb, nk), 0) + r * rb, s, -jnp.inf)
            m_prev = m_sc[pr, hh, rows, :]
            m_new = jnp.maximum(m_prev, jnp.max(s, axis=-1, keepdims=True))
            alpha = jnp.exp2(m_prev - m_new)
            p = jnp.exp2(s - jnp.concatenate([m_new] * (nk // LANES), axis=1))
            m_sc[pr, hh, rows, :] = m_new
            acc_sc[pr, hh, rows, :] = alpha * acc_sc[pr, hh, rows, :] + _dot_nt(p.astype(BF16), vbs[pr][hh][:, :nk])

    @pl.when(ki < qi)
    def _():
        update(False)

    @pl.when(ki == qi)
    def _():
        update(True)
        for pr in range(FOX_PAIRS):
            outs = [acc_sc[pr, hh] / pltpu.roll(acc_sc[pr, hh], FOX_HEAD_DIM, axis=1) for hh in range(2)]
            o_ref[0, :, pr * LANES:(pr + 1) * LANES] = jnp.where(low, outs[0], outs[1])


def fox_prompt_attn(q, k, v, ft):
    bsz, seq, _ = q.shape
    t = _pick(seq, (1024, 512, 256, 128))
    nb = seq // t
    pairs =[(i, j) for i in range(nb) for j in range(i + 1)]
    qi = jnp.asarray([p[0] for p in pairs], jnp.int32)
    ki = jnp.asarray([p[1] for p in pairs], jnp.int32)
    npair = FOX_HEADS // 2
    w = FOX_PAIRS * LANES
    ft4 = ft.reshape(bsz, npair, 2, seq)
    grid_spec = pltpu.PrefetchScalarGridSpec(
        num_scalar_prefetch=2,
        grid=(bsz, npair // FOX_PAIRS, len(pairs)),
        in_specs=[
            pl.BlockSpec((1, t, w), lambda b, j, s, qi, ki: (b, qi[s], j)),
            pl.BlockSpec((1, w, t), lambda b, j, s, qi, ki: (b, j, ki[s])),
            pl.BlockSpec((1, w, t), lambda b, j, s, qi, ki: (b, j, ki[s])),
            pl.BlockSpec((1, FOX_PAIRS, 2, t), lambda b, j, s, qi, ki: (b, j, 0, ki[s])),
        ],
        out_specs=pl.BlockSpec((1, t, w), lambda b, j, s, qi, ki: (b, qi[s], j)),
        scratch_shapes=[pltpu.VMEM((FOX_PAIRS, 2, t, LANES), F32)] * 2,
    )
    return pl.pallas_call(
        functools.partial(_fox_body, t=t),
        grid_spec=grid_spec,
        out_shape=jax.ShapeDtypeStruct((bsz, seq, FOX_W), F32),
        compiler_params=_cparams("parallel", "parallel", "arbitrary"),
        name="fox_prompt_attn",
    )(qi, ki, q, k, v, ft4)


def _page_suffix_body(x_ref, rin_ref, tot_ref):
    x = x_ref[...]
    after = (_iota((PAGE_SIZE, PAGE_SIZE), 0) > _iota((PAGE_SIZE, PAGE_SIZE), 1)).astype(F32)
    rin_ref[...] = _dot_hi(x, after)
    tot_ref[...] = jnp.broadcast_to(jnp.sum(x, axis=1, keepdims=True), x.shape)


def page_suffix(lft):
    r = lft.shape[0]
    tr = _pick(r, (2048, 1024, 512, 256, 128, 64, 32, 16))
    spec = pl.BlockSpec((tr, PAGE_SIZE), lambda i: (i, 0))
    return pl.pallas_call(
        _page_suffix_body,
        grid=(r // tr,),
        in_specs=[spec],
        out_specs=[spec, spec],
        out_shape=[jax.ShapeDtypeStruct(lft.shape, F32)] * 2,
        compiler_params=_cparams("parallel"),
        name="page_suffix",
    )(lft)


DEC_PAGES = 8


def _rep_rows(x, n):
    r, c = x.shape
    return jnp.broadcast_to(x[:, None, :], (r, n, c)).reshape(r * n, c)


def _fox_decode_body(pt_ref, q_ref, *refs, nt, nsteps):
    k_refs = refs[0:DEC_PAGES]
    v_refs = refs[DEC_PAGES:2 * DEC_PAGES]
    rin_refs = refs[2 * DEC_PAGES:3 * DEC_PAGES]
    tot_refs = refs[3 * DEC_PAGES:4 * DEC_PAGES]
    kn_ref, vn_ref, lfn_ref, o_ref, qbd_sc, m_sc, l_sc, acc_sc, run_sc = refs[4 * DEC_PAGES:]
    j = pl.program_id(1)
    rows = FOX_HEADS * nt

    @pl.when(j == 0)
    def _():
        qt = jnp.concatenate([q_ref[0] * (FOX_HEAD_DIM ** -0.5)] * FOX_HEADS, axis=0)
        own = _iota((rows, FOX_W), 1) // FOX_HEAD_DIM == _iota((rows, FOX_W), 0) // nt
        qbd_sc[...] = jnp.where(own, qt, 0.0).astype(BF16)
        m_sc[...] = jnp.full(m_sc.shape, -jnp.inf, F32)
        l_sc[...] = jnp.zeros_like(l_sc)
        acc_sc[...] = jnp.zeros_like(acc_sc)
        run_sc[...] = jnp.zeros_like(run_sc)

    def update(ss, vals, transposed):
        m_prev = m_sc[...]
        m_new = m_prev
        for s in ss:
            m_new = jnp.maximum(m_new, jnp.max(s, axis=-1, keepdims=True))
        alpha = jnp.exp(m_prev - m_new)
        l_new = alpha * l_sc[...]
        acc = jnp.concatenate([alpha] * (FOX_W // LANES), axis=1) * acc_sc[...]
        for s, val in zip(ss, vals):
            p = jnp.exp(s - m_new)
            l_new = l_new + jnp.sum(p, axis=-1, keepdims=True)
            acc = acc + (_dot_nt(p.astype(BF16), val) if transposed else _dot(p.astype(BF16), val))
        m_sc[...] = m_new
        l_sc[...] = l_new
        acc_sc[...] = acc

    @pl.when(j < nsteps)
    def _():
        qbd = qbd_sc[...]
        raw = [_dot(qbd, k_refs[i][0, 0].reshape(FOX_W, PAGE_SIZE).astype(BF16)) for i in range(DEC_PAGES)]
        run = run_sc[...]
        ss = []
        for i in range(DEC_PAGES):
            ss.append(raw[i] + _rep_rows(rin_refs[i][0] + run, nt))
            run = run + tot_refs[i][0]
        run_sc[...] = run
        update(ss, [v_refs[i][0, 0].reshape(FOX_W, PAGE_SIZE).astype(BF16) for i in range(DEC_PAGES)], True)

    @pl.when(j == nsteps)
    def _():
        pad = jnp.zeros((PAGE_SIZE - nt, FOX_W), F32)
        kb = jnp.concatenate([kn_ref[0], pad], axis=0).astype(BF16)
        vb = jnp.concatenate([vn_ref[0], pad], axis=0).astype(BF16)
        s = _dot_nt(qbd_sc[...], kb)
        incl = (_iota((PAGE_SIZE, PAGE_SIZE), 0) <= _iota((PAGE_SIZE, PAGE_SIZE), 1)).astype(F32)
        fn = _dot_hi(lfn_ref[0], incl)
        tok = _iota((rows, PAGE_SIZE), 0) % nt
        update([jnp.where(_iota((rows, PAGE_SIZE), 1) <= tok, s - _rep_rows(fn, nt), -jnp.inf)], [vb], False)
        o = acc_sc[...] / jnp.concatenate([l_sc[...]] * (FOX_W // LANES), axis=1)
        lane_head = _iota((nt, FOX_W), 1) // FOX_HEAD_DIM
        out = jnp.zeros((nt, FOX_W), F32)
        for hd in range(FOX_HEADS):
            out = out + jnp.where(lane_head == hd, o[hd * nt:(hd + 1) * nt, :], 0.0)
        o_ref[0] = out


def fox_decode_attn(q, k_new, v_new, lfn_t, k_cache_t, v_cache_t, layer, rin, tot, page_table):
    bsz, nt, _ = q.shape
    npages = page_table.shape[1]
    assert npages % DEC_PAGES == 0
    nsteps = npages // DEC_PAGES
    rows = FOX_HEADS * nt

    def page(i):
        return lambda b, j, pt: pt[b, npages - 1 - (jnp.minimum(j, nsteps - 1) * DEC_PAGES + i)]

    cache_specs = [pl.BlockSpec((1, 1, FOX_HEADS, FOX_HEAD_DIM, PAGE_SIZE),
                                functools.partial(lambda b, j, pt, pg: (layer, pg(b, j, pt), 0, 0, 0), pg=page(i)))
                   for i in range(DEC_PAGES)]
    bias_specs = [pl.BlockSpec((1, FOX_HEADS, PAGE_SIZE), functools.partial(lambda b, j, pt, pg: (pg(b, j, pt), 0, 0), pg=page(i)))
                  for i in range(DEC_PAGES)]
    per_seq = lambda shape: pl.BlockSpec(shape, lambda b, j, pt: (b, 0, 0))
    grid_spec = pltpu.PrefetchScalarGridSpec(
        num_scalar_prefetch=1,
        grid=(bsz, nsteps + 1),
        in_specs=[per_seq((1, nt, FOX_W))] + cache_specs + cache_specs + bias_specs + bias_specs
        + [per_seq((1, nt, FOX_W)), per_seq((1, nt, FOX_W)), per_seq((1, FOX_HEADS, PAGE_SIZE))],
        out_specs=per_seq((1, nt, FOX_W)),
        scratch_shapes=[
            pltpu.VMEM((rows, FOX_W), BF16),
            pltpu.VMEM((rows, LANES), F32),
            pltpu.VMEM((rows, LANES), F32),
            pltpu.VMEM((rows, FOX_W), F32),
            pltpu.VMEM((FOX_HEADS, PAGE_SIZE), F32),
        ],
    )
    return pl.pallas_call(
        functools.partial(_fox_decode_body, nt=nt, nsteps=nsteps),
        grid_spec=grid_spec,
        out_shape=jax.ShapeDtypeStruct((bsz, nt, FOX_W), F32),
        compiler_params=_cparams("parallel", "arbitrary"),
        name="fox_decode_attn",
    )(page_table, q, *([k_cache_t] * DEC_PAGES), *([v_cache_t] * DEC_PAGES), *([rin] * DEC_PAGES),
      *([tot] * DEC_PAGES), k_new, v_new, lfn_t)


def _split_hyb_weights(w_in):
    o = 0
    parts = {}
    for name, width in (("z", SSD_INNER), ("xbc", SSD_CONV_DIM), ("dt", SSD_HEADS), ("qkv", GDN_CONV_DIM),
                        ("gate", GDN_VW), ("b", GDN_HEADS), ("a", GDN_HEADS)):
        parts[name] = w_in[:, o:o + width]
        o += width
    small = jnp.concatenate([parts["dt"], parts["b"], parts["a"]], axis=1)
    small = jnp.pad(small, ((0, 0), (0, LANES - small.shape[1])))
    return {k: parts[k].astype(BF16) for k in ("z", "xbc", "qkv", "gate")} | {"small": small.astype(BF16)}


def _hybrid_layer(h, e, W, st):
    bsz, seq, d = h.shape
    m = bsz * seq
    h2 = h.reshape(m, d)
    wp = _split_hyb_weights(W["w_in_hyb"][e])
    gain = W["norm_mix"][2 * e]
    proj = {k: fused_linear([h2], [wp[k]], gain=gain, name="hyb_in_" + k) for k in ("z", "xbc", "qkv")}
    proj["gate"], proj["small"] = fused_linear([h2], [wp["gate"]], gain=gain, name="hyb_in_gate", side_w=wp["small"])
    xbc = proj["xbc"].reshape(bsz, seq, SSD_CONV_DIM)
    qkv = proj["qkv"].reshape(bsz, seq, GDN_CONV_DIM)
    small = proj["small"].reshape(bsz, seq, LANES)
    xbc_c = conv_silu(xbc, st["ssd_conv"], W["ssd_conv_w"][e], W["ssd_conv_b"][e])
    qkv_c = conv_silu(qkv, st["gdn_conv"], W["gdn_conv_w"][e], jnp.zeros((GDN_CONV_DIM,), F32))
    y, ssd_h = ssd_scan(xbc_c, proj["z"].reshape(bsz, seq, SSD_INNER), small, st["ssd"],
                        W["ssd_dt_bias"][e], W["ssd_A_log"][e], W["ssd_D"][e], W["ssd_norm"][e])
    o, gdn_s = gdn_scan(qkv_c, proj["gate"].reshape(bsz, seq, GDN_VW), small, st["gdn"],
                        W["gdn_dt_bias"][e], W["gdn_A_log"][e], W["gdn_norm"][e])
    w_out = W["w_out_hyb"][e].astype(BF16)
    h_new = fused_linear([y.reshape(m, SSD_INNER), o.reshape(m, GDN_VW)], [w_out[:SSD_INNER], w_out[SSD_INNER:]],
                         residual=h2, name="hyb_out")
    new = dict(ssd=ssd_h, ssd_conv=xbc[:, seq - (CONV_K - 1):], gdn=gdn_s, gdn_conv=qkv[:, seq - (CONV_K - 1):])
    return h_new.reshape(bsz, seq, d), new


def _fox_layer(h, o_idx, layer, W, st, prompt):
    bsz, seq, d = h.shape
    m = bsz * seq
    h2 = h.reshape(m, d)
    w_in = W["w_in_fox"][o_idx]
    gain = W["norm_mix"][layer]
    wq, wk, wv = (w_in[:, i * FOX_W:(i + 1) * FOX_W].astype(BF16) for i in range(3))
    wf = jnp.pad(w_in[:, 3 * FOX_W:], ((0, 0), (0, LANES - FOX_HEADS))).astype(BF16)
    bf = jnp.pad(W["b_fox_f"][o_idx], (0, LANES - FOX_HEADS))
    q, lf = fused_linear([h2], [wq], gain=gain, name="fox_q", side_w=wf, side_bias=bf)
    q = q.reshape(bsz, seq, FOX_W)
    lf = lf[:, :FOX_HEADS].reshape(bsz, seq, FOX_HEADS)
    lf_t = jnp.swapaxes(lf, 1, 2)
    shp = (bsz, seq, FOX_HEADS, FOX_HEAD_DIM)
    if prompt:
        k_t = fused_linear([h2], [wk], gain=gain, name="fox_k", seq_t=seq)
        v_t = fused_linear([h2], [wv], gain=gain, name="fox_v", seq_t=seq)
        att = fox_prompt_attn(q, k_t, v_t, cumsum_lanes(lf_t))
        k_out = k_t.reshape(bsz, FOX_HEADS, FOX_HEAD_DIM, seq).transpose(0, 3, 1, 2)
        v_out = v_t.reshape(bsz, FOX_HEADS, FOX_HEAD_DIM, seq).transpose(0, 3, 1, 2)
    else:
        k = fused_linear([h2], [wk], gain=gain, name="fox_k").reshape(bsz, seq, FOX_W)
        v = fused_linear([h2], [wv], gain=gain, name="fox_v").reshape(bsz, seq, FOX_W)
        k_out, v_out = k.reshape(shp), v.reshape(shp)
        n_pool = st["cache_fox_k"].shape[1]
        lfc_t = jnp.swapaxes(st["cache_fox_lf"][o_idx].astype(F32), 1, 2).reshape(n_pool * FOX_HEADS, PAGE_SIZE)
        rin, tot = page_suffix(lfc_t)
        rin = rin.reshape(n_pool, FOX_HEADS, PAGE_SIZE)
        tot = tot.reshape(n_pool, FOX_HEADS, PAGE_SIZE)
        lfn_t = jnp.pad(lf_t, ((0, 0), (0, 0), (0, PAGE_SIZE - seq)))
        kc_t = jnp.transpose(st["cache_fox_k"], (0, 1, 3, 4, 2))
        vc_t = jnp.transpose(st["cache_fox_v"], (0, 1, 3, 4, 2))
        att = fox_decode_attn(q, k, v, lfn_t, kc_t, vc_t, o_idx, rin, tot, st["page_table"])
    h_new = fused_linear([att.reshape(m, FOX_W)], [W["w_out_fox"][o_idx].astype(BF16)], residual=h2, name="fox_out")
    return h_new.reshape(bsz, seq, d), dict(fox_k=k_out, fox_v=v_out, fox_lf=lf)


def _run_group(x, W, st, prompt):
    bsz, seq, d = x.shape
    depth = W["norm_mix"].shape[0]
    names = ("ssd", "ssd_conv", "gdn", "gdn_conv", "fox_k", "fox_v", "fox_lf", "mem_k", "mem_v")
    out = {n: [] for n in names}
    h = x
    for layer in range(depth):
        if layer % 2 == 0:
            e = layer // 2
            if prompt:
                s0 = dict(ssd=jnp.zeros((bsz, SSD_HEADS, SSD_HEAD_DIM, SSD_STATE), F32),
                          ssd_conv=jnp.zeros((bsz, CONV_K - 1, SSD_CONV_DIM), F32),
                          gdn=jnp.zeros((bsz, GDN_HEADS, GDN_DK, GDN_DV), F32),
                          gdn_conv=jnp.zeros((bsz, CONV_K - 1, GDN_CONV_DIM), F32))
            else:
                s0 = dict(ssd=st["state_ssd"][e], ssd_conv=st["state_ssd_conv"][e],
                          gdn=st["state_gdn"][e], gdn_conv=st["state_gdn_conv"][e])
            h, new = _hybrid_layer(h, e, W, s0)
        else:
            h, new = _fox_layer(h, layer // 2, layer, W, st, prompt)
        for n, val in new.items():
            out[n].append(val)
        if prompt:
            mem = st["mem"]
            mem2 = mem.reshape(bsz * MEM_LEN, d)
            mk = fused_linear([mem2], [W["wk_x"][layer].astype(BF16)], gain=W["norm_mem"][layer], name="mem_k")
            mv = fused_linear([mem2], [W["wv_x"][layer].astype(BF16)], gain=W["norm_mem"][layer], name="mem_v")
            mk = mk.reshape(bsz, MEM_LEN, X_W)
            mv = mv.reshape(bsz, MEM_LEN, X_W)
            out["mem_k"].append(mk.reshape(bsz, MEM_LEN, X_HEADS, X_HEAD_DIM))
            out["mem_v"].append(mv.reshape(bsz, MEM_LEN, X_HEADS, X_HEAD_DIM))
        else:
            mk, mv = st["cache_mem_k"], st["cache_mem_v"]
        h = cross_attn(h, W["norm_x"][layer], W["wq_x"][layer].astype(BF16), mk, mv, W["wo_x"][layer].astype(BF16),
                       mem_layer=None if prompt else layer)
        h = swiglu(h.reshape(bsz * seq, d), W["norm_ffn"][layer], W["w1_b"], W["w3_b"], W["w2_b"], layer,
                   final_gain=W["norm_final"] if layer == depth - 1 else None).reshape(bsz, seq, d)
    new = {n: jnp.stack(out[n]) for n in names if out[n]}
    return h, new


def kernel(x_prompt, x_sample, mem_prompt, state_ssd, state_ssd_conv, state_gdn, state_gdn_conv, cache_fox_k, cache_fox_v, cache_fox_lf, page_table, cache_mem_k, cache_mem_v, norm_mix, norm_x, norm_mem, norm_ffn, norm_final, w_in_hyb, w_out_hyb, ssd_conv_w, ssd_conv_b, ssd_dt_bias, ssd_A_log, ssd_D, ssd_norm, gdn_conv_w, gdn_dt_bias, gdn_A_log, gdn_norm, w_in_fox, b_fox_f, w_out_fox, wq_x, wk_x, wv_x, wo_x, w1, w3, w2):
    W = dict(norm_mix=norm_mix, norm_x=norm_x, norm_mem=norm_mem, norm_ffn=norm_ffn, norm_final=norm_final,
             w_in_hyb=w_in_hyb, w_out_hyb=w_out_hyb, ssd_conv_w=ssd_conv_w, ssd_conv_b=ssd_conv_b,
             ssd_dt_bias=ssd_dt_bias, ssd_A_log=ssd_A_log, ssd_D=ssd_D, ssd_norm=ssd_norm,
             gdn_conv_w=gdn_conv_w, gdn_dt_bias=gdn_dt_bias, gdn_A_log=gdn_A_log, gdn_norm=gdn_norm,
             w_in_fox=w_in_fox, b_fox_f=b_fox_f, w_out_fox=w_out_fox,
             wq_x=wq_x, wk_x=wk_x, wv_x=wv_x, wo_x=wo_x,
             w1_b=w1.astype(BF16), w3_b=w3.astype(BF16), w2_b=w2.astype(BF16))
    y_prompt, pn = _run_group(x_prompt, W, dict(mem=mem_prompt), True)
    st = dict(state_ssd=state_ssd, state_ssd_conv=state_ssd_conv, state_gdn=state_gdn,
              state_gdn_conv=state_gdn_conv, cache_fox_k=cache_fox_k, cache_fox_v=cache_fox_v,
              cache_fox_lf=cache_fox_lf, page_table=page_table, cache_mem_k=cache_mem_k, cache_mem_v=cache_mem_v)
    y_sample, sn = _run_group(x_sample, W, st, False)
    return (y_prompt, y_sample,
            pn["ssd"], pn["ssd_conv"], pn["gdn"], pn["gdn_conv"],
            pn["fox_k"], pn["fox_v"], pn["fox_lf"], pn["mem_k"], pn["mem_v"],
            sn["ssd"], sn["ssd_conv"], sn["gdn"], sn["gdn_conv"],
            sn["fox_k"], sn["fox_v"], sn["fox_lf"])
```
